```python
import math
import jax, jax.numpy as jnp
from jax import lax
import numpy as np

D_MODEL = 1024
BATCH = 16
SEQ = 2048
DEPTH = 1
DEC_BATCH = 32
DEC_SEQ = 1
PAST_LEN = 16384
PAGE_SIZE = 128

D_MIX = D_MODEL
ATTN_WIDTH = D_MIX // 2
S5_WIDTH = D_MIX - ATTN_WIDTH
HEAD_DIM = 64
V_DIM = 2 * HEAD_DIM
N_HEADS = ATTN_WIDTH // V_DIM
ROT_DIM = HEAD_DIM // 4
ROPE_THETA = 500000.0
Q_BLOCK = 128
S5_GROUP = 16
S5_GROUPS = S5_WIDTH // S5_GROUP
S5_STATE = 64
N_GROUPS_MOE = 4
EXP_PER_GROUP = 8
N_EXPERTS = N_GROUPS_MOE * EXP_PER_GROUP
TOP_K_IN_GROUP = 2
D_EXPERT = D_MODEL // 4
PROJ_COLS = 2 * N_HEADS * HEAD_DIM * 2 + N_HEADS * V_DIM + S5_WIDTH
EPS = 1e-5
NEG_INF = -1e30

kernel_name = "hymba_diffattn_s5_hmoe_step"


def rmsnorm(x, g):
    xf = x.astype(jnp.float32)
    y = xf * lax.rsqrt(jnp.mean(xf * xf, axis=-1, keepdims=True) + EPS)
    return (y * g.astype(jnp.float32)).astype(x.dtype)


def rope_partial(x, pos):
    inv = ROPE_THETA ** (-jnp.arange(0, ROT_DIM, 2, dtype=jnp.float32) / ROT_DIM)
    ang = pos.astype(jnp.float32)[:, None] * inv[None, :]
    cos = jnp.cos(ang)[None, :, None, :]
    sin = jnp.sin(ang)[None, :, None, :]
    xf = x.astype(jnp.float32)
    x1 = xf[..., :ROT_DIM // 2]
    x2 = xf[..., ROT_DIM // 2:ROT_DIM]
    out = jnp.concatenate([x1 * cos - x2 * sin, x2 * cos + x1 * sin, xf[..., ROT_DIM:]], axis=-1)
    return out.astype(x.dtype)


def diff_attend(q, k, v, q_pos, k_pos, lam, subln_g, lambda_init):
    s = jnp.einsum('bqhcd,bkhcd->bhcqk', q, k).astype(jnp.float32)
    mask = k_pos[None, :] <= q_pos[:, None]
    s = jnp.where(mask, s, NEG_INF)
    p = jax.nn.softmax(s, axis=-1)
    a = p[:, :, 0] - lam * p[:, :, 1]
    o = jnp.einsum('bhqk,bkhe->bqhe', a, v.astype(jnp.float32))
    o = o * lax.rsqrt(jnp.mean(o * o, axis=-1, keepdims=True) + EPS)
    o = o * subln_g.astype(jnp.float32) * (1.0 - lambda_init)
    return o.astype(v.dtype)


def s5_mix(u, h0_re, h0_im, a_re, a_im, log_dt, b_re, b_im, c_re, c_im, d_skip, w_glu, b_glu):
    nb, ns, _ = u.shape
    uf = u.astype(jnp.float32).reshape(nb, ns, S5_GROUPS, S5_GROUP)
    lam = lax.complex(a_re.astype(jnp.float32), a_im.astype(jnp.float32))
    dt = jnp.exp(log_dt.astype(jnp.float32))[:, None]
    lam_bar = jnp.exp(lam * dt)
    b_c = lax.complex(b_re.astype(jnp.float32), b_im.astype(jnp.float32))
    b_bar = ((lam_bar - 1.0) / lam)[:, :, None] * b_c
    bu = jnp.einsum('gph,bsgh->bsgp', b_bar, uf.astype(jnp.complex64))
    a = jnp.broadcast_to(lam_bar, bu.shape)

    def combine(e1, e2):
        a1, x1 = e1
        a2, x2 = e2
        return a1 * a2, a2 * x1 + x2

    a_cum, h = lax.associative_scan(combine, (a, bu), axis=1)
    h0 = lax.complex(h0_re.astype(jnp.float32), h0_im.astype(jnp.float32))
    h = h + a_cum * h0[:, None]
    c_c = lax.complex(c_re.astype(jnp.float32), c_im.astype(jnp.float32))
    y = jnp.real(jnp.einsum('ghp,bsgp->bsgh', c_c, h))
    y = y + d_skip.astype(jnp.float32).reshape(S5_GROUPS, S5_GROUP) * uf
    y = jax.nn.gelu(y.reshape(nb, ns, S5_WIDTH))
    y = y * jax.nn.sigmoid(y @ w_glu.astype(jnp.float32) + b_glu.astype(jnp.float32))
    h_last = h[:, -1]
    return y.astype(u.dtype), jnp.real(h_last), jnp.imag(h_last)


def hier_moe(h, w_rg, b_rg, w_re, b_re, w_gate, w_up, w_down):
    nb, ns, dm = h.shape
    t = h.reshape(-1, dm)
    gl = (t @ w_rg + b_rg).astype(jnp.float32)
    gp = jax.nn.softmax(gl, axis=-1)
    g_idx = jnp.argmax(gl, axis=-1)
    g_w = jnp.take_along_axis(gp, g_idx[:, None], axis=1)
    el = (jnp.einsum('td,gde->tge', t, w_re) + b_re).astype(jnp.float32)
    el = jnp.take_along_axis(el, g_idx[:, None, None], axis=1)[:, 0]
    ep = jax.nn.softmax(el, axis=-1)
    w2, i2 = lax.top_k(ep, TOP_K_IN_GROUP)
    w2 = w2 / jnp.sum(w2, axis=-1, keepdims=True)
    gid = g_idx[:, None] * EXP_PER_GROUP + i2
    comb = jnp.sum(jax.nn.one_hot(gid, N_EXPERTS, dtype=jnp.float32) * (g_w * w2)[..., None], axis=1)
    out = jnp.zeros(t.shape, jnp.float32)
    for e in range(N_EXPERTS):
        he = jax.nn.silu(t @ w_gate[e]) * (t @ w_up[e])
        out = out + comb[:, e:e + 1] * (he @ w_down[e]).astype(jnp.float32)
    return out.astype(h.dtype).reshape(nb, ns, dm)


def decoder_layer(x, pos, past_k, past_v, h0_re, h0_im, lambda_init,
                  ln1_g, w_in, lambda_q1, lambda_k1, lambda_q2, lambda_k2, subln_g,
                  s5_a_re, s5_a_im, s5_log_dt, s5_b_re, s5_b_im, s5_c_re, s5_c_im, s5_d, w_glu, b_glu,
                  w_out, ln2_g, w_router_group, b_router_group, w_router_expert, b_router_expert,
                  w_gate, w_up, w_down):
    nb, ns, _ = x.shape
    hn = rmsnorm(x, ln1_g)
    proj = hn @ w_in
    qk_w = 2 * N_HEADS * HEAD_DIM
    q = proj[..., :qk_w]
    k = proj[..., qk_w:2 * qk_w]
    v = proj[..., 2 * qk_w:2 * qk_w + N_HEADS * V_DIM].reshape(nb, ns, N_HEADS, V_DIM)
    u = proj[..., 2 * qk_w + N_HEADS * V_DIM:]
    q = rope_partial(q.reshape(nb, ns, 2 * N_HEADS, HEAD_DIM), pos) * (HEAD_DIM ** -0.5)
    k = rope_partial(k.reshape(nb, ns, 2 * N_HEADS, HEAD_DIM), pos)
    lam = (jnp.exp(jnp.sum(lambda_q1.astype(jnp.float32) * lambda_k1.astype(jnp.float32)))
           - jnp.exp(jnp.sum(lambda_q2.astype(jnp.float32) * lambda_k2.astype(jnp.float32)))
           + lambda_init)
    q5 = q.reshape(nb, ns, N_HEADS, 2, HEAD_DIM)
    if past_k is None:
        k5 = k.reshape(nb, ns, N_HEADS, 2, HEAD_DIM)
        n_blk = ns // Q_BLOCK
        qb = q5.reshape(nb, n_blk, Q_BLOCK, N_HEADS, 2, HEAD_DIM).transpose(1, 0, 2, 3, 4, 5)
        pb = pos.reshape(n_blk, Q_BLOCK)
        ob = lax.map(lambda qp: diff_attend(qp[0], k5, v, qp[1], pos, lam, subln_g, lambda_init), (qb, pb))
        attn = ob.transpose(1, 0, 2, 3, 4).reshape(nb, ns, ATTN_WIDTH)
    else:
        k_all = jnp.concatenate([past_k, k], axis=1).reshape(nb, -1, N_HEADS, 2, HEAD_DIM)
        v_all = jnp.concatenate([past_v, v], axis=1)
        k_pos = jnp.arange(k_all.shape[1], dtype=jnp.int32)
        attn = diff_attend(q5, k_all, v_all, pos, k_pos, lam, subln_g, lambda_init).reshape(nb, ns, ATTN_WIDTH)
    s5_out, h_re, h_im = s5_mix(u, h0_re, h0_im, s5_a_re, s5_a_im, s5_log_dt, s5_b_re, s5_b_im,
                                s5_c_re, s5_c_im, s5_d, w_glu, b_glu)
    x = x + jnp.concatenate([attn, s5_out], axis=-1) @ w_out
    x = x + hier_moe(rmsnorm(x, ln2_g), w_router_group, b_router_group, w_router_expert,
                     b_router_expert, w_gate, w_up, w_down)
    return x, k, v, h_re, h_im


def setup_inputs(seed: int = 0) -> dict:
    key = jax.random.key(seed)
    ks = jax.random.split(key, 40)
    f32 = jnp.float32
    n_pages = PAST_LEN // PAGE_SIZE
    n_pool = (DEC_BATCH * n_pages * 5) // 4

    def nrm(k, shape, scale):
        return jax.random.normal(k, shape, f32) * scale

    x_prompt = nrm(ks[0], (BATCH, SEQ, D_MODEL), 1.0)
    x_sample = nrm(ks[1], (DEC_BATCH, DEC_SEQ, D_MODEL), 1.0)
    cache_k = nrm(ks[2], (DEPTH, n_pool, PAGE_SIZE, 2 * N_HEADS, HEAD_DIM), 1.0)
    cache_v = nrm(ks[3], (DEPTH, n_pool, PAGE_SIZE, N_HEADS, V_DIM), 1.0)
    state_s5_re = nrm(ks[4], (DEPTH, DEC_BATCH, S5_GROUPS, S5_STATE), 0.5)
    state_s5_im = nrm(ks[5], (DEPTH, DEC_BATCH, S5_GROUPS, S5_STATE), 0.5)
    page_table = jax.random.permutation(ks[6], n_pool)[:DEC_BATCH * n_pages].reshape(DEC_BATCH, n_pages).astype(jnp.int32)
    ln1_g = 1.0 + nrm(ks[7], (DEPTH, D_MODEL), 0.01)
    w_in = nrm(ks[8], (DEPTH, D_MODEL, PROJ_COLS), D_MODEL ** -0.5)
    lambda_q1 = nrm(ks[9], (DEPTH, HEAD_DIM), 0.1)
    lambda_k1 = nrm(ks[10], (DEPTH, HEAD_DIM), 0.1)
    lambda_q2 = nrm(ks[11], (DEPTH, HEAD_DIM), 0.1)
    lambda_k2 = nrm(ks[12], (DEPTH, HEAD_DIM), 0.1)
    subln_g = 1.0 + nrm(ks[13], (DEPTH, V_DIM), 0.01)
    s5_a_re = -0.5 + nrm(ks[14], (DEPTH, S5_GROUPS, S5_STATE), 0.01)
    s5_a_im = math.pi * jnp.arange(S5_STATE, dtype=f32)[None, None, :] + nrm(ks[15], (DEPTH, S5_GROUPS, S5_STATE), 0.01)
    s5_log_dt = jax.random.uniform(ks[16], (DEPTH, S5_GROUPS), f32, minval=math.log(1e-3), maxval=math.log(1e-1))
    s5_b_re = nrm(ks[17], (DEPTH, S5_GROUPS, S5_STATE, S5_GROUP), (2 * S5_GROUP) ** -0.5)
    s5_b_im = nrm(ks[18], (DEPTH, S5_GROUPS, S5_STATE, S5_GROUP), (2 * S5_GROUP) ** -0.5)
    s5_c_re = nrm(ks[19], (DEPTH, S5_GROUPS, S5_GROUP, S5_STATE), S5_STATE ** -0.5)
    s5_c_im = nrm(ks[20], (DEPTH, S5_GROUPS, S5_GROUP, S5_STATE), S5_STATE ** -0.5)
    s5_d = nrm(ks[21], (DEPTH, S5_WIDTH), 0.5)
    w_glu = nrm(ks[22], (DEPTH, S5_WIDTH, S5_WIDTH), S5_WIDTH ** -0.5)
    b_glu = nrm(ks[23], (DEPTH, S5_WIDTH), 0.01)
    w_out = nrm(ks[24], (DEPTH, D_MIX, D_MODEL), D_MIX ** -0.5)
    ln2_g = 1.0 + nrm(ks[25], (DEPTH, D_MODEL), 0.01)
    w_router_group = nrm(ks[26], (DEPTH, D_MODEL, N_GROUPS_MOE), D_MODEL ** -0.5)
    b_router_group = nrm(ks[27], (DEPTH, N_GROUPS_MOE), 0.01)
    w_router_expert = nrm(ks[28], (DEPTH, N_GROUPS_MOE, D_MODEL, EXP_PER_GROUP), D_MODEL ** -0.5)
    b_router_expert = nrm(ks[29], (DEPTH, N_GROUPS_MOE, EXP_PER_GROUP), 0.01)
    w_gate = nrm(ks[30], (DEPTH, N_EXPERTS, D_MODEL, D_EXPERT), D_MODEL ** -0.5)
    w_up = nrm(ks[31], (DEPTH, N_EXPERTS, D_MODEL, D_EXPERT), D_MODEL ** -0.5)
    w_down = nrm(ks[32], (DEPTH, N_EXPERTS, D_EXPERT, D_MODEL), D_EXPERT ** -0.5)
    ln_f_g = 1.0 + nrm(ks[33], (D_MODEL,), 0.01)
    return {"x_prompt": x_prompt, "x_sample": x_sample, "cache_k": cache_k, "cache_v": cache_v,
            "state_s5_re": state_s5_re, "state_s5_im": state_s5_im, "page_table": page_table,
            "ln1_g": ln1_g, "w_in": w_in, "lambda_q1": lambda_q1, "lambda_k1": lambda_k1,
            "lambda_q2": lambda_q2, "lambda_k2": lambda_k2, "subln_g": subln_g,
            "s5_a_re": s5_a_re, "s5_a_im": s5_a_im, "s5_log_dt": s5_log_dt,
            "s5_b_re": s5_b_re, "s5_b_im": s5_b_im, "s5_c_re": s5_c_re, "s5_c_im": s5_c_im,
            "s5_d": s5_d, "w_glu": w_glu, "b_glu": b_glu, "w_out": w_out, "ln2_g": ln2_g,
            "w_router_group": w_router_group, "b_router_group": b_router_group,
            "w_router_expert": w_router_expert, "b_router_expert": b_router_expert,
            "w_gate": w_gate, "w_up": w_up, "w_down": w_down, "ln_f_g": ln_f_g}


def reference(x_prompt, x_sample, cache_k, cache_v, state_s5_re, state_s5_im, page_table,
              ln1_g, w_in, lambda_q1, lambda_k1, lambda_q2, lambda_k2, subln_g,
              s5_a_re, s5_a_im, s5_log_dt, s5_b_re, s5_b_im, s5_c_re, s5_c_im, s5_d, w_glu, b_glu,
              w_out, ln2_g, w_router_group, b_router_group, w_router_expert, b_router_expert,
              w_gate, w_up, w_down, ln_f_g):
    n_prompt, seq, _ = x_prompt.shape
    n_dec, dec_seq, _ = x_sample.shape
    past_len = page_table.shape[1] * cache_k.shape[2]
    pos_p = jnp.arange(seq, dtype=jnp.int32)
    pos_s = past_len + jnp.arange(dec_seq, dtype=jnp.int32)
    zeros_state = jnp.zeros((n_prompt, S5_GROUPS, S5_STATE), jnp.float32)
    xp, xs = x_prompt, x_sample
    kp_l, vp_l, hrp_l, hip_l, ks_l, vs_l, hrs_l, his_l = [], [], [], [], [], [], [], []
    for l in range(DEPTH):
        lambda_init = 0.8 - 0.6 * math.exp(-0.3 * l)
        lw = (ln1_g[l], w_in[l], lambda_q1[l], lambda_k1[l], lambda_q2[l], lambda_k2[l], subln_g[l],
              s5_a_re[l], s5_a_im[l], s5_log_dt[l], s5_b_re[l], s5_b_im[l], s5_c_re[l], s5_c_im[l],
              s5_d[l], w_glu[l], b_glu[l], w_out[l], ln2_g[l], w_router_group[l], b_router_group[l],
              w_router_expert[l], b_router_expert[l], w_gate[l], w_up[l], w_down[l])
        xp, kp, vp, hrp, hip = decoder_layer(xp, pos_p, None, None, zeros_state, zeros_state, lambda_init, *lw)
        past_k = cache_k[l, page_table].reshape(n_dec, past_len, 2 * N_HEADS, HEAD_DIM)
        past_v = cache_v[l, page_table].reshape(n_dec, past_len, N_HEADS, V_DIM)
        xs, ksn, vsn, hrs, his = decoder_layer(xs, pos_s, past_k, past_v, state_s5_re[l], state_s5_im[l],
                                               lambda_init, *lw)
        kp_l.append(kp); vp_l.append(vp); hrp_l.append(hrp); hip_l.append(hip)
        ks_l.append(ksn); vs_l.append(vsn); hrs_l.append(hrs); his_l.append(his)
    y_prompt = rmsnorm(xp, ln_f_g)
    y_sample = rmsnorm(xs, ln_f_g)
    new_k_prompt = jnp.stack(kp_l, axis=0)
    new_v_prompt = jnp.stack(vp_l, axis=0)
    new_s5_re_prompt = jnp.stack(hrp_l, axis=0)
    new_s5_im_prompt = jnp.stack(hip_l, axis=0)
    new_k_sample = jnp.stack(ks_l, axis=0)
    new_v_sample = jnp.stack(vs_l, axis=0)
    new_s5_re_sample = jnp.stack(hrs_l, axis=0)
    new_s5_im_sample = jnp.stack(his_l, axis=0)
    return (y_prompt, y_sample, new_k_prompt, new_v_prompt, new_s5_re_prompt, new_s5_im_prompt,
            new_k_sample, new_v_sample, new_s5_re_sample, new_s5_im_sample)
```

```python
import functools
import math

import jax
import jax.numpy as jnp
from jax import lax
from jax.experimental import pallas as pl
from jax.experimental.pallas import tpu as pltpu

F32 = jnp.float32
BF16 = jnp.bfloat16

HEAD_DIM = 64
V_DIM = 128
ROT_DIM = 16
ROPE_THETA = 500000.0
S5_GROUP = 16
S5_STATE = 64
S5_CHUNK = 16
SUBLANES = 8
N_GROUPS_MOE = 4
EXP_PER_GROUP = 8
EPS = 1e-5
NEG_INF = -1e30
LANES = 128
VMEM_LIMIT = 56 * 1024 * 1024

_HI = lax.Precision.HIGHEST


def _cparams(sem):
    return pltpu.CompilerParams(dimension_semantics=sem, vmem_limit_bytes=VMEM_LIMIT)


def _nt_dot(a, b, **kw):
    return lax.dot_general(a, b, (((1,), (1,)), ((), ())), preferred_element_type=F32, **kw)


def _inproj_kernel(x_ref, g_ref, w_ref, c_ref, s1_ref, s2_ref,
                   q_ref, kf_ref, vf_ref, kb_ref, vb_ref, u_ref):
    x = x_ref[...]
    ms = jnp.mean(x * x, axis=-1, keepdims=True)
    hn = (x * lax.rsqrt(ms + EPS) * g_ref[...]).astype(BF16)
    proj = jnp.dot(hn, w_ref[...], preferred_element_type=F32)
    w = q_ref.shape[-1]
    c = c_ref[...]
    s1 = s1_ref[...]
    s2 = s2_ref[...]
    for j in range(w // LANES):
        sl = slice(j * LANES, (j + 1) * LANES)
        zq = proj[:, j * LANES:(j + 1) * LANES]
        zk = proj[:, w + j * LANES:w + (j + 1) * LANES]
        rq = zq * c + pltpu.roll(zq, ROT_DIM // 2, 1) * s1 + pltpu.roll(zq, LANES - ROT_DIM // 2, 1) * s2
        rk = zk * c + pltpu.roll(zk, ROT_DIM // 2, 1) * s1 + pltpu.roll(zk, LANES - ROT_DIM // 2, 1) * s2
        q_ref[:, sl] = (rq * (HEAD_DIM ** -0.5)).astype(q_ref.dtype)
        kf_ref[:, sl] = rk
        kb_ref[:, sl] = rk.astype(BF16)
    v = proj[:, 2 * w:3 * w]
    vf_ref[...] = v
    vb_ref[...] = v.astype(BF16)
    u_ref[...] = proj[:, 3 * w:4 * w].astype(u_ref.dtype)


def _inproj(x2, ln_g, w_bf, rc, rs1, rs2, tm, q_dtype, u_dtype):
    t, d = x2.shape
    w = w_bf.shape[1] // 4
    nt = t // tm
    npos = rc.shape[0] // tm
    row = lambda i: (i, 0)
    pos = lambda i: (i % npos, 0)
    fixed = lambda i: (0, 0)
    outs = [jax.ShapeDtypeStruct((t, w), q_dtype), jax.ShapeDtypeStruct((t, w), F32),
            jax.ShapeDtypeStruct((t, w), F32), jax.ShapeDtypeStruct((t, w), BF16),
            jax.ShapeDtypeStruct((t, w), BF16), jax.ShapeDtypeStruct((t, w), u_dtype)]
    return pl.pallas_call(
        _inproj_kernel,
        grid=(nt,),
        in_specs=[pl.BlockSpec((tm, d), row), pl.BlockSpec((1, d), fixed),
                  pl.BlockSpec((d, 4 * w), fixed),
                  pl.BlockSpec((tm, LANES), pos), pl.BlockSpec((tm, LANES), pos),
                  pl.BlockSpec((tm, LANES), pos)],
        out_specs=[pl.BlockSpec((tm, w), row)] * 6,
        out_shape=outs,
        compiler_params=_cparams(("parallel",)),
        name="inproj",
    )(x2, ln_g, w_bf, rc, rs1, rs2)


def _rope_tables(pos):
    half = ROT_DIM // 2
    inv = ROPE_THETA ** (-jnp.arange(0, ROT_DIM, 2, dtype=F32) / ROT_DIM)
    ang = pos.astype(F32)[:, None] * inv[None, :]
    cos, sin = jnp.cos(ang), jnp.sin(ang)
    n = pos.shape[0]
    pad = jnp.zeros((n, HEAD_DIM - ROT_DIM), F32)
    c = jnp.concatenate([cos, cos, pad + 1.0], axis=1)
    s1 = jnp.concatenate([jnp.zeros((n, half), F32), sin, pad], axis=1)
    s2 = jnp.concatenate([-sin, jnp.zeros((n, half), F32), pad], axis=1)
    rep = LANES // HEAD_DIM
    return tuple(jnp.tile(a, (1, rep)) for a in (c, s1, s2))


def _diff_lambda(lam_ref, lambda_init):
    l = lam_ref[...]
    a = jnp.sum(l[0:1] * l[1:2], axis=-1, keepdims=True)
    b = jnp.sum(l[2:3] * l[3:4], axis=-1, keepdims=True)
    return jnp.exp(a) - jnp.exp(b) + lambda_init


def _attn_kernel(q_ref, k_ref, v_ref, lam_ref, g_ref, o_ref, *, tq, lambda_init):
    qi = pl.program_id(2)
    q = q_ref[0]
    lane = lax.broadcasted_iota(jnp.int32, q.shape, 1)
    zero = jnp.zeros_like(q)
    qc = (jnp.where(lane < HEAD_DIM, q, zero), jnp.where(lane >= HEAD_DIM, q, zero))

    def tile(j, carry, masked):
        k = k_ref[0, pl.ds(pl.multiple_of(j * tq, tq), tq), :]
        v = v_ref[0, pl.ds(pl.multiple_of(j * tq, tq), tq), :]
        new = []
        for c in range(2):
            m, l, acc = carry[3 * c:3 * c + 3]
            s = _nt_dot(qc[c], k)
            if masked:
                r = lax.broadcasted_iota(jnp.int32, s.shape, 0)
                cc = lax.broadcasted_iota(jnp.int32, s.shape, 1)
                s = jnp.where(cc <= r, s, NEG_INF)
            m_new = jnp.maximum(m, jnp.max(s, axis=-1, keepdims=True))
            p = jnp.exp(s - m_new)
            alpha = jnp.exp(m - m_new)
            l = alpha * l + jnp.sum(p, axis=-1, keepdims=True)
            acc = alpha * acc + jnp.dot(p.astype(BF16), v, preferred_element_type=F32)
            new += [m_new, l, acc]
        return tuple(new)

    init = (jnp.full((tq, 1), NEG_INF, F32), jnp.zeros((tq, 1), F32), jnp.zeros((tq, V_DIM), F32)) * 2
    carry = lax.fori_loop(0, qi, lambda j, c: tile(j, c, False), init)
    m0, l0, a0, m1, l1, a1 = tile(qi, carry, True)
    lam = _diff_lambda(lam_ref, lambda_init)
    o = a0 / l0 - lam * (a1 / l1)
    o = o * lax.rsqrt(jnp.mean(o * o, axis=-1, keepdims=True) + EPS)
    o_ref[0] = (o * g_ref[...] * (1.0 - lambda_init)).astype(o_ref.dtype)


def _attn(q, k, v, lam4, subln_g, n_heads, tq, lambda_init):
    b, s, w = q.shape
    qmap = lambda bi, h, i: (bi, i, h)
    kvmap = lambda bi, h, i: (bi, 0, h)
    fixed = lambda bi, h, i: (0, 0)
    return pl.pallas_call(
        functools.partial(_attn_kernel, tq=tq, lambda_init=lambda_init),
        grid=(b, n_heads, s // tq),
        in_specs=[pl.BlockSpec((1, tq, LANES), qmap), pl.BlockSpec((1, s, LANES), kvmap),
                  pl.BlockSpec((1, s, LANES), kvmap), pl.BlockSpec((4, HEAD_DIM), fixed),
                  pl.BlockSpec((1, V_DIM), fixed)],
        out_specs=pl.BlockSpec((1, tq, LANES), qmap),
        out_shape=jax.ShapeDtypeStruct((b, s, w), BF16),
        compiler_params=_cparams(("parallel", "parallel", "arbitrary")),
        name="attn",
    )(q, k, v, lam4, subln_g)


def _split3(x):
    hi = x.astype(BF16)
    r = x - hi.astype(F32)
    mid = r.astype(BF16)
    lo = (r - mid.astype(F32)).astype(BF16)
    return hi, mid, lo


def _expand(x, e_bf):
    return sum(jnp.dot(t, e_bf, preferred_element_type=F32) for t in _split3(x))


def _decode_kernel(pt_ref, qb_ref, kn_ref, vn_ref, e_ref, lam_ref, g_ref, *refs, pp, lambda_init):
    del pt_ref
    k_refs, v_refs = refs[:pp], refs[pp:2 * pp]
    o_ref = refs[2 * pp]
    m_ref, l_ref, a0_ref, a1_ref = refs[2 * pp + 1:]
    j = pl.program_id(1)

    @pl.when(j == 0)
    def _():
        m_ref[...] = jnp.full_like(m_ref, NEG_INF)
        l_ref[...] = jnp.zeros_like(l_ref)
        a0_ref[...] = jnp.zeros_like(a0_ref)
        a1_ref[...] = jnp.zeros_like(a1_ref)

    qb = qb_ref[0]
    e = (e_ref[0], e_ref[1])
    s = jnp.concatenate([jnp.dot(kr[...].astype(BF16), qb, preferred_element_type=F32)
                         for kr in k_refs], axis=0)
    v = jnp.concatenate([vr[...] for vr in v_refs], axis=0)
    m_old = m_ref[0:1]
    m_new = jnp.maximum(m_old, jnp.max(s, axis=0, keepdims=True))
    p = jnp.exp(s - m_new)
    alpha = jnp.exp(m_old - m_new)
    l_ref[...] = jnp.broadcast_to(alpha * l_ref[0:1] + jnp.sum(p, axis=0, keepdims=True), l_ref.shape)
    m_ref[...] = jnp.broadcast_to(m_new, m_ref.shape)
    p_bf = p.astype(BF16)
    alpha8 = jnp.broadcast_to(alpha, (SUBLANES, LANES))
    for c, a_ref in ((0, a0_ref), (1, a1_ref)):
        pb = jnp.dot(p_bf, e[c], preferred_element_type=F32)
        contrib = jnp.sum((pb * v).reshape(-1, SUBLANES, v.shape[1]), axis=0)
        a_ref[...] = _expand(alpha8, e[c]) * a_ref[...] + contrib

    @pl.when(j == pl.num_programs(1) - 1)
    def _():
        kn = jnp.broadcast_to(kn_ref[0], (SUBLANES, kn_ref.shape[-1]))
        s_new = jnp.dot(kn.astype(BF16), qb, preferred_element_type=F32)[0:1]
        m_old = m_ref[0:1]
        m_f = jnp.maximum(m_old, s_new)
        p_new = jnp.broadcast_to(jnp.exp(s_new - m_f), (SUBLANES, LANES))
        alpha = jnp.broadcast_to(jnp.exp(m_old - m_f), (SUBLANES, LANES))
        l_f = alpha * l_ref[...] + p_new
        vn = vn_ref[0]
        outs = []
        for c, a_ref in ((0, a0_ref), (1, a1_ref)):
            acc = jnp.sum(a_ref[...], axis=0, keepdims=True)
            num = _expand(alpha, e[c])[0:1] * acc + _expand(p_new, e[c])[0:1] * vn
            outs.append(num / _expand(l_f, e[c])[0:1])
        lam = _diff_lambda(lam_ref, lambda_init)
        o = outs[0] - lam * outs[1]
        g = g_ref[...]
        for h in range(o.shape[1] // V_DIM):
            oh = o[:, h * V_DIM:(h + 1) * V_DIM]
            oh = oh * lax.rsqrt(jnp.mean(oh * oh, axis=-1, keepdims=True) + EPS)
            o_ref[0, :, h * V_DIM:(h + 1) * V_DIM] = (oh * g * (1.0 - lambda_init)).astype(o_ref.dtype)


def _decode_attn(page_table, qblk, k_new, v_new, e2, lam4, subln_g, cache_k, cache_v, pp, lambda_init):
    bd, n_pages = page_table.shape
    _, page, w = cache_k.shape
    pt_flat = page_table.reshape(-1)
    fixed2 = lambda b, j, pt: (0, 0)
    perb = lambda b, j, pt: (b, 0, 0)

    def page_spec(i):
        return pl.BlockSpec((None, page, w), lambda b, j, pt: (pt[b * n_pages + j * pp + i], 0, 0))

    grid_spec = pltpu.PrefetchScalarGridSpec(
        num_scalar_prefetch=1,
        grid=(bd, n_pages // pp),
        in_specs=[pl.BlockSpec((1, w, LANES), perb), pl.BlockSpec((1, 1, w), perb),
                  pl.BlockSpec((1, 1, w), perb),
                  pl.BlockSpec((2, LANES, w), lambda b, j, pt: (0, 0, 0)),
                  pl.BlockSpec((4, HEAD_DIM), fixed2), pl.BlockSpec((1, V_DIM), fixed2)]
                 + [page_spec(i) for i in range(pp)] * 2,
        out_specs=pl.BlockSpec((1, 1, w), perb),
        scratch_shapes=[pltpu.VMEM((SUBLANES, LANES), F32), pltpu.VMEM((SUBLANES, LANES), F32),
                        pltpu.VMEM((SUBLANES, w), F32), pltpu.VMEM((SUBLANES, w), F32)],
    )
    return pl.pallas_call(
        functools.partial(_decode_kernel, pp=pp, lambda_init=lambda_init),
        grid_spec=grid_spec,
        out_shape=jax.ShapeDtypeStruct((bd, 1, w), BF16),
        compiler_params=_cparams(("parallel", "arbitrary")),
        name="decode_attn",
    )(pt_flat, qblk, k_new, v_new, e2, lam4, subln_g, *([cache_k] * pp), *([cache_v] * pp))


def _s5prep_kernel(arow_ref, acol_ref, ldt_ref, bt_ref, ct_ref, d_ref,
                   mt_ref, w_ref, vt_ref, lam_ref):
    L, hg, p = S5_CHUNK, S5_GROUP, S5_STATE
    dt = jnp.exp(ldt_ref[0])
    ar, ai = arow_ref[0, 0:1], arow_ref[0, 1:2]
    arc, aic = acol_ref[0, :, 0:1], acol_ref[0, :, 1:2]

    def powers(a_r, a_i, j):
        mag = jnp.exp(j * (a_r * dt))
        return mag * jnp.cos(j * (a_i * dt)), mag * jnp.sin(j * (a_i * dt))

    l1r, l1i = powers(ar, ai, 1.0)
    llr, lli = powers(ar, ai, float(L))
    lam_ref[0] = jnp.concatenate([llr, lli, l1r, l1i, jnp.zeros((4, p), F32)], axis=0)
    den = ar * ar + ai * ai
    cr = ((l1r - 1.0) * ar + l1i * ai) / den
    ci = (l1i * ar - (l1r - 1.0) * ai) / den
    btr, bti = bt_ref[0, 0], bt_ref[0, 1]
    bbr = btr * cr - bti * ci
    bbi = btr * ci + bti * cr
    jrow = lax.broadcasted_iota(jnp.int32, (L, p), 0).astype(F32)
    pr, pi = powers(ar, ai, jrow)
    for s in range(L):
        qr, qi = pr[L - 1 - s:L - s], pi[L - 1 - s:L - s]
        w_ref[0, 0, s * hg:(s + 1) * hg, :] = bbr * qr - bbi * qi
        w_ref[0, 1, s * hg:(s + 1) * hg, :] = bbr * qi + bbi * qr
    jl = (lax.broadcasted_iota(jnp.int32, (p, L * hg), 1) // hg).astype(F32)
    lpr, lpi = powers(arc, aic, jl)
    ctr, cti = ct_ref[0, 0], ct_ref[0, 1]
    cjr = ctr * lpr - cti * lpi
    cji = ctr * lpi + cti * lpr
    c1r, c1i = powers(arc, aic, 1.0)
    vt_ref[0, 0] = cjr * c1r - cji * c1i
    vt_ref[0, 1] = -(cjr * c1i + cji * c1r)
    kt = (jnp.dot(bbr, cjr, preferred_element_type=F32, precision=_HI)
          - jnp.dot(bbi, cji, preferred_element_type=F32, precision=_HI))
    n = L * hg
    ri = lax.broadcasted_iota(jnp.int32, (n, n), 0)
    cidx = lax.broadcasted_iota(jnp.int32, (n, n), 1)
    dtile = d_ref[0]
    for s in range(L):
        shift = (cidx - ri == s * hg).astype(F32)
        blk = jnp.dot(kt, shift, preferred_element_type=F32, precision=_HI)
        rr = lax.broadcasted_iota(jnp.int32, (hg, n), 0) + s * hg
        cc = lax.broadcasted_iota(jnp.int32, (hg, n), 1)
        mt_ref[0, s * hg:(s + 1) * hg, :] = blk + jnp.where(rr == cc, dtile, 0.0)


def _s5prep(arow, acol, ldt, bt, ct, dt_tiled):
    g = arow.shape[0]
    L, hg, p = S5_CHUNK, S5_GROUP, S5_STATE
    n = L * hg
    i3 = lambda i: (i, 0, 0)
    i4 = lambda i: (i, 0, 0, 0)
    return pl.pallas_call(
        _s5prep_kernel,
        grid=(g,),
        in_specs=[pl.BlockSpec((1, 2, p), i3), pl.BlockSpec((1, p, 2), i3), pl.BlockSpec((1, 1, 1), i3),
                  pl.BlockSpec((1, 2, hg, p), i4), pl.BlockSpec((1, 2, p, n), i4),
                  pl.BlockSpec((1, 1, n), i3)],
        out_specs=[pl.BlockSpec((1, n, n), i3), pl.BlockSpec((1, 2, n, p), i4),
                   pl.BlockSpec((1, 2, p, n), i4), pl.BlockSpec((1, SUBLANES, p), i3)],
        out_shape=[jax.ShapeDtypeStruct((g, n, n), F32), jax.ShapeDtypeStruct((g, 2, n, p), F32),
                   jax.ShapeDtypeStruct((g, 2, p, n), F32), jax.ShapeDtypeStruct((g, SUBLANES, p), F32)],
        compiler_params=_cparams(("parallel",)),
        name="s5prep",
    )(arow, acol, ldt, bt, ct, dt_tiled)


def _s5chunk_kernel(u_ref, mt_ref, w_ref, vt_ref, lam_ref, y_ref, hl_ref, sr_ref, si_ref):
    u = u_ref[0, 0]
    sr_ref[...] = jnp.dot(u, w_ref[0, 0].astype(BF16), preferred_element_type=F32)
    si_ref[...] = jnp.dot(u, w_ref[0, 1].astype(BF16), preferred_element_type=F32)
    lr = jnp.broadcast_to(lam_ref[0, 0:1], (SUBLANES, S5_STATE))
    li = jnp.broadcast_to(lam_ref[0, 1:2], (SUBLANES, S5_STATE))
    nc = u.shape[0] // SUBLANES

    def step(c, h):
        hr, hi = h
        rows = pl.ds(pl.multiple_of(c * SUBLANES, SUBLANES), SUBLANES)
        s_r, s_i = sr_ref[rows, :], si_ref[rows, :]
        sr_ref[rows, :] = hr
        si_ref[rows, :] = hi
        return lr * hr - li * hi + s_r, lr * hi + li * hr + s_i

    z = jnp.zeros((SUBLANES, S5_STATE), F32)
    hr, hi = lax.fori_loop(0, nc, step, (z, z), unroll=8)
    hl_ref[0, 0, 0] = hr
    hl_ref[0, 0, 1] = hi
    y = jnp.dot(u, mt_ref[0].astype(BF16), preferred_element_type=F32)
    y += jnp.dot(sr_ref[...].astype(BF16), vt_ref[0, 0].astype(BF16), preferred_element_type=F32)
    y += jnp.dot(si_ref[...].astype(BF16), vt_ref[0, 1].astype(BF16), preferred_element_type=F32)
    y_ref[0, 0] = y.astype(y_ref.dtype)


def _s5chunk(u4, mt, w, vt, lam):
    nbh, g, rows, n = u4.shape
    p = S5_STATE
    um = lambda b, gi: (b, gi, 0, 0)
    g3 = lambda b, gi: (gi, 0, 0)
    g4 = lambda b, gi: (gi, 0, 0, 0)
    return pl.pallas_call(
        _s5chunk_kernel,
        grid=(nbh, g),
        in_specs=[pl.BlockSpec((1, 1, rows, n), um), pl.BlockSpec((1, n, n), g3),
                  pl.BlockSpec((1, 2, n, p), g4), pl.BlockSpec((1, 2, p, n), g4),
                  pl.BlockSpec((1, SUBLANES, p), g3)],
        out_specs=[pl.BlockSpec((1, 1, rows, n), um),
                   pl.BlockSpec((1, 1, 2, SUBLANES, p), lambda b, gi: (b, gi, 0, 0, 0))],
        out_shape=[jax.ShapeDtypeStruct(u4.shape, BF16),
                   jax.ShapeDtypeStruct((nbh, g, 2, SUBLANES, p), F32)],
        scratch_shapes=[pltpu.VMEM((rows, p), F32), pltpu.VMEM((rows, p), F32)],
        compiler_params=_cparams(("parallel", "parallel")),
        name="s5chunk",
    )(u4, mt, w, vt, lam)


def _s5step_kernel(u_ref, h0_ref, bb_ref, lam_ref, c_ref, d_ref, y_ref, h_ref):
    u = u_ref[0]
    h0r, h0i = h0_ref[0, 0], h0_ref[0, 1]
    l1r, l1i = lam_ref[0, 2:3], lam_ref[0, 3:4]
    bur = jnp.dot(u, bb_ref[0, 0], preferred_element_type=F32, precision=_HI)
    bui = jnp.dot(u, bb_ref[0, 1], preferred_element_type=F32, precision=_HI)
    hr = l1r * h0r - l1i * h0i + bur
    hi = l1r * h0i + l1i * h0r + bui
    h_ref[0, 0] = hr
    h_ref[0, 1] = hi
    y = _nt_dot(hr, c_ref[0, 0], precision=_HI) - _nt_dot(hi, c_ref[0, 1], precision=_HI)
    y_ref[0] = y + d_ref[0] * u


def _s5step(u3, h0, bb, lam, c2, d3):
    g, bd, hg = u3.shape
    p = S5_STATE
    i3 = lambda i: (i, 0, 0)
    i4 = lambda i: (i, 0, 0, 0)
    return pl.pallas_call(
        _s5step_kernel,
        grid=(g,),
        in_specs=[pl.BlockSpec((1, bd, hg), i3), pl.BlockSpec((1, 2, bd, p), i4),
                  pl.BlockSpec((1, 2, hg, p), i4), pl.BlockSpec((1, SUBLANES, p), i3),
                  pl.BlockSpec((1, 2, hg, p), i4), pl.BlockSpec((1, 1, hg), i3)],
        out_specs=[pl.BlockSpec((1, bd, hg), i3), pl.BlockSpec((1, 2, bd, p), i4)],
        out_shape=[jax.ShapeDtypeStruct((g, bd, hg), F32), jax.ShapeDtypeStruct((g, 2, bd, p), F32)],
        compiler_params=_cparams(("parallel",)),
        name="s5step",
    )(u3, h0, bb, lam, c2, d3)


def _postmix_kernel(x_ref, a_ref, y_ref, wglu_ref, bglu_ref, wout_ref, g2_ref, wr_ref, br_ref,
                    x1_ref, h2_ref, comb_ref):
    wa = a_ref.shape[-1]
    y = jax.nn.gelu(y_ref[...].astype(F32))
    z = jnp.dot(y.astype(BF16), wglu_ref[...], preferred_element_type=F32) + bglu_ref[...]
    s5o = y * jax.nn.sigmoid(z)
    mix = jnp.dot(a_ref[...].astype(BF16), wout_ref[0:wa, :], preferred_element_type=F32)
    mix += jnp.dot(s5o.astype(BF16), wout_ref[wa:, :], preferred_element_type=F32)
    x1 = x_ref[...] + mix
    x1_ref[...] = x1
    h2 = x1 * lax.rsqrt(jnp.mean(x1 * x1, axis=-1, keepdims=True) + EPS) * g2_ref[...]
    h2_ref[...] = h2.astype(h2_ref.dtype)
    logits = jnp.dot(h2, wr_ref[...], preferred_element_type=F32, precision=_HI) + br_ref[...]
    ne = N_GROUPS_MOE * EXP_PER_GROUP
    lane = lax.broadcasted_iota(jnp.int32, logits.shape, 1).astype(F32)
    big = jnp.float32(1 << 20)
    gmask = (lane >= ne) & (lane < ne + N_GROUPS_MOE)
    gl = jnp.where(gmask, logits, NEG_INF)
    gmax = jnp.max(gl, axis=-1, keepdims=True)
    gidx = jnp.min(jnp.where(gl == gmax, lane, big), axis=-1, keepdims=True) - ne
    g_w = 1.0 / jnp.sum(jnp.where(gmask, jnp.exp(logits - gmax), 0.0), axis=-1, keepdims=True)
    lo = gidx * EXP_PER_GROUP
    el = jnp.where((lane >= lo) & (lane < lo + EXP_PER_GROUP), logits, NEG_INF)
    e1 = jnp.max(el, axis=-1, keepdims=True)
    i1 = jnp.min(jnp.where(el == e1, lane, big), axis=-1, keepdims=True)
    el2 = jnp.where(lane == i1, NEG_INF, el)
    e2 = jnp.max(el2, axis=-1, keepdims=True)
    i2 = jnp.min(jnp.where(el2 == e2, lane, big), axis=-1, keepdims=True)
    r = jnp.exp(e2 - e1)
    w1 = g_w / (1.0 + r)
    w2 = g_w * r / (1.0 + r)
    comb_ref[...] = jnp.where(lane == i1, w1, 0.0) + jnp.where(lane == i2, w2, 0.0)


def _postmix(x2, attn, ys5, wglu_bf, bglu, wout_bf, ln2_g, w_router, b_router, tm):
    t, d = x2.shape
    wa = attn.shape[1]
    ws = ys5.shape[1]
    row = lambda i: (i, 0)
    fixed = lambda i: (0, 0)
    return pl.pallas_call(
        _postmix_kernel,
        grid=(t // tm,),
        in_specs=[pl.BlockSpec((tm, d), row), pl.BlockSpec((tm, wa), row), pl.BlockSpec((tm, ws), row),
                  pl.BlockSpec((ws, ws), fixed), pl.BlockSpec((1, ws), fixed),
                  pl.BlockSpec((wa + ws, d), fixed), pl.BlockSpec((1, d), fixed),
                  pl.BlockSpec((d, LANES), fixed), pl.BlockSpec((1, LANES), fixed)],
        out_specs=[pl.BlockSpec((tm, d), row), pl.BlockSpec((tm, d), row), pl.BlockSpec((tm, LANES), row)],
        out_shape=[jax.ShapeDtypeStruct((t, d), F32), jax.ShapeDtypeStruct((t, d), BF16),
                   jax.ShapeDtypeStruct((t, LANES), F32)],
        compiler_params=_cparams(("parallel",)),
        name="postmix",
    )(x2, attn, ys5, wglu_bf, bglu, wout_bf, ln2_g, w_router, b_router)


def _moe_kernel(h_ref, x1_ref, comb_ref, wg_ref, wu_ref, wd_ref, gf_ref, o_ref, acc_ref):
    e = pl.program_id(1)

    @pl.when(e == 0)
    def _():
        acc_ref[...] = jnp.zeros_like(acc_ref)

    h = h_ref[...]
    he = jax.nn.silu(jnp.dot(h, wg_ref[0], preferred_element_type=F32))
    he = he * jnp.dot(h, wu_ref[0], preferred_element_type=F32)
    comb = comb_ref[...]
    lane = lax.broadcasted_iota(jnp.int32, comb.shape, 1)
    ce = jnp.sum(jnp.where(lane == e, comb, 0.0), axis=-1, keepdims=True)
    acc_ref[...] += jnp.dot((he * ce).astype(BF16), wd_ref[0], preferred_element_type=F32)

    @pl.when(e == pl.num_programs(1) - 1)
    def _():
        x2 = x1_ref[...] + acc_ref[...]
        o_ref[...] = x2 * lax.rsqrt(jnp.mean(x2 * x2, axis=-1, keepdims=True) + EPS) * gf_ref[...]


def _moe(h2, x1, comb, wg_bf, wu_bf, wd_bf, lnf_g, tm):
    t, d = x1.shape
    ne, _, de = wg_bf.shape
    row = lambda i, e: (i, 0)
    return pl.pallas_call(
        _moe_kernel,
        grid=(t // tm, ne),
        in_specs=[pl.BlockSpec((tm, d), row), pl.BlockSpec((tm, d), row), pl.BlockSpec((tm, LANES), row),
                  pl.BlockSpec((1, d, de), lambda i, e: (e, 0, 0)),
                  pl.BlockSpec((1, d, de), lambda i, e: (e, 0, 0)),
                  pl.BlockSpec((1, de, d), lambda i, e: (e, 0, 0)),
                  pl.BlockSpec((1, d), lambda i, e: (0, 0))],
        out_specs=pl.BlockSpec((tm, d), row),
        out_shape=jax.ShapeDtypeStruct((t, d), F32),
        scratch_shapes=[pltpu.VMEM((tm, d), F32)],
        compiler_params=_cparams(("parallel", "arbitrary")),
        name="moe",
    )(h2, x1, comb, wg_bf, wu_bf, wd_bf, lnf_g)


def _tile(n, want):
    t = min(n, want)
    while n % t:
        t //= 2
    return t


def kernel(x_prompt, x_sample, cache_k, cache_v, state_s5_re, state_s5_im, page_table, ln1_g, w_in, lambda_q1, lambda_k1, lambda_q2, lambda_k2, subln_g, s5_a_re, s5_a_im, s5_log_dt, s5_b_re, s5_b_im, s5_c_re, s5_c_im, s5_d, w_glu, b_glu, w_out, ln2_g, w_router_group, b_router_group, w_router_expert, b_router_expert, w_gate, w_up, w_down, ln_f_g):
    depth = ln1_g.shape[0]
    assert depth == 1, "single-layer step"
    b, s, d = x_prompt.shape
    bd, ds, _ = x_sample.shape
    assert ds == 1
    n_pages = page_table.shape[1]
    page = cache_k.shape[2]
    past_len = n_pages * page
    wq = w_in.shape[2] // 4
    n_heads = wq // V_DIM
    g = s5_a_re.shape[1]
    p = S5_STATE
    hg = S5_GROUP
    L = S5_CHUNK
    n = L * hg
    assert b % SUBLANES == 0 and s % L == 0 and g * hg == wq
    lambda_init = 0.8 - 0.6 * math.exp(-0.3 * 0)

    w_in_bf = w_in[0].astype(BF16)
    wglu_bf = w_glu[0].astype(BF16)
    wout_bf = w_out[0].astype(BF16)
    wg_bf, wu_bf, wd_bf = (w[0].astype(BF16) for w in (w_gate, w_up, w_down))
    ne = w_gate.shape[1]
    w_router = jnp.concatenate(
        [jnp.transpose(w_router_expert[0], (1, 0, 2)).reshape(d, ne), w_router_group[0],
         jnp.zeros((d, LANES - ne - N_GROUPS_MOE), F32)], axis=1)
    b_router = jnp.concatenate([b_router_expert[0].reshape(ne), b_router_group[0],
                                jnp.zeros((LANES - ne - N_GROUPS_MOE,), F32)])[None, :]
    lam4 = jnp.concatenate([lambda_q1, lambda_k1, lambda_q2, lambda_k2], axis=0)
    ln1 = ln1_g[0][None, :]
    ln2 = ln2_g[0][None, :]
    lnf = ln_f_g[None, :]
    subg = subln_g[0][None, :]
    bglu = b_glu[0][None, :]

    arow = jnp.stack([s5_a_re[0], s5_a_im[0]], axis=1)
    acol = jnp.stack([s5_a_re[0], s5_a_im[0]], axis=2)
    ldt = s5_log_dt[0].reshape(g, 1, 1)
    bt = jnp.stack([jnp.swapaxes(s5_b_re[0], 1, 2), jnp.swapaxes(s5_b_im[0], 1, 2)], axis=1)
    ct = jnp.stack([jnp.tile(jnp.swapaxes(s5_c_re[0], 1, 2), (1, 1, L)),
                    jnp.tile(jnp.swapaxes(s5_c_im[0], 1, 2), (1, 1, L))], axis=1)
    d_g = s5_d[0].reshape(g, 1, hg)
    mt, w_s5, vt, lam_s5 = _s5prep(arow, acol, ldt, bt, ct, jnp.tile(d_g, (1, 1, L)))

    tm = _tile(b * s, 512)
    x2 = x_prompt.reshape(b * s, d)
    rc, rs1, rs2 = _rope_tables(jnp.arange(s, dtype=jnp.int32))
    q, kf, vf, kb, vb, u = _inproj(x2, ln1, w_in_bf, rc, rs1, rs2, _tile(s, tm), BF16, BF16)
    tq = _tile(s, 256)
    attn = _attn(q.reshape(b, s, wq), kb.reshape(b, s, wq), vb.reshape(b, s, wq), lam4, subg,
                 n_heads, tq, lambda_init)
    nbh, nc = b // SUBLANES, s // L
    u4 = u.reshape(nbh, SUBLANES, nc, L, g, hg).transpose(0, 4, 2, 1, 3, 5).reshape(nbh, g, nc * SUBLANES, n)
    y4, hl = _s5chunk(u4, mt, w_s5, vt, lam_s5)
    ys5 = y4.reshape(nbh, g, nc, SUBLANES, L, hg).transpose(0, 3, 2, 4, 1, 5).reshape(b * s, wq)
    hl = hl.transpose(2, 0, 3, 1, 4).reshape(2, b, g, p)
    x1, h2, comb = _postmix(x2, attn.reshape(b * s, wq), ys5, wglu_bf, bglu, wout_bf, ln2,
                            w_router, b_router, tm)
    y_prompt = _moe(h2, x1, comb, wg_bf, wu_bf, wd_bf, lnf, tm).reshape(b, s, d)

    xs2 = x_sample.reshape(bd, d)
    pos_s = jnp.full((bd,), past_len, jnp.int32)
    sc, ss1, ss2 = _rope_tables(pos_s)
    qs, kfs, vfs, _, _, us = _inproj(xs2, ln1, w_in_bf, sc, ss1, ss2, bd, F32, F32)
    seg = (jnp.arange(wq)[:, None] // HEAD_DIM) == jnp.arange(LANES)[None, :]
    qblk = jnp.where(seg[None], qs[:, :, None], 0.0).astype(BF16)
    hc = jnp.arange(LANES)[:, None]
    col_head = jnp.arange(wq)[None, :] // V_DIM
    e2 = jnp.stack([(hc == 2 * col_head), (hc == 2 * col_head + 1)]).astype(BF16)
    pp = _tile(n_pages, 8)
    attn_s = _decode_attn(page_table, qblk, kfs[:, None, :], vfs[:, None, :], e2, lam4, subg,
                          cache_k[0].reshape(-1, page, wq), cache_v[0].reshape(-1, page, wq),
                          pp, lambda_init)
    u3 = us.reshape(bd, g, hg).transpose(1, 0, 2)
    h0 = jnp.stack([state_s5_re[0], state_s5_im[0]]).transpose(2, 0, 1, 3)
    bb = w_s5[:, :, (L - 1) * hg:, :]
    c2 = jnp.stack([s5_c_re[0], s5_c_im[0]], axis=1)
    ys3, hs = _s5step(u3, h0, bb, lam_s5, c2, d_g)
    ys_s5 = ys3.transpose(1, 0, 2).reshape(bd, wq)
    x1s, h2s, combs = _postmix(xs2, attn_s.reshape(bd, wq), ys_s5, wglu_bf, bglu, wout_bf, ln2,
                               w_router, b_router, bd)
    y_sample = _moe(h2s, x1s, combs, wg_bf, wu_bf, wd_bf, lnf, bd).reshape(bd, 1, d)
    hs = hs.transpose(1, 2, 0, 3)

    return (y_prompt, y_sample,
            kf.reshape(1, b, s, 2 * n_heads, HEAD_DIM), vf.reshape(1, b, s, n_heads, V_DIM),
            hl[0][None], hl[1][None],
            kfs.reshape(1, bd, 1, 2 * n_heads, HEAD_DIM), vfs.reshape(1, bd, 1, n_heads, V_DIM),
            hs[0][None], hs[1][None])
```

```python
import functools
import math

import jax
import jax.numpy as jnp
from jax import lax
from jax.experimental import pallas as pl
from jax.experimental.pallas import tpu as pltpu

F32 = jnp.float32
BF16 = jnp.bfloat16

HEAD_DIM = 64
V_DIM = 128
ROT_DIM = 16
ROPE_THETA = 500000.0
S5_GROUP = 16
S5_STATE = 64
S5_CHUNK = 16
SUBLANES = 8
N_GROUPS_MOE = 4
EXP_PER_GROUP = 8
EPS = 1e-5
NEG_INF = -1e30
LANES = 128
GID_LANE = LANES - 1
MOE_CHUNK = 128
VMEM_LIMIT = 56 * 1024 * 1024

_HI = lax.Precision.HIGHEST


def _cparams(sem):
    return pltpu.CompilerParams(dimension_semantics=sem, vmem_limit_bytes=VMEM_LIMIT)


def _nt_dot(a, b, **kw):
    return lax.dot_general(a, b, (((1,), (1,)), ((), ())), preferred_element_type=F32, **kw)


def _inproj_kernel(x_ref, g_ref, w_ref, c_ref, s1_ref, s2_ref,
                   q_ref, kf_ref, vf_ref, kb_ref, vb_ref, u_ref):
    x = x_ref[...]
    ms = jnp.mean(x * x, axis=-1, keepdims=True)
    hn = (x * lax.rsqrt(ms + EPS) * g_ref[...]).astype(BF16)
    proj = jnp.dot(hn, w_ref[...], preferred_element_type=F32)
    w = q_ref.shape[-1]
    c = c_ref[...]
    s1 = s1_ref[...]
    s2 = s2_ref[...]
    for j in range(w // LANES):
        sl = slice(j * LANES, (j + 1) * LANES)
        zq = proj[:, j * LANES:(j + 1) * LANES]
        zk = proj[:, w + j * LANES:w + (j + 1) * LANES]
        rq = zq * c + pltpu.roll(zq, ROT_DIM // 2, 1) * s1 + pltpu.roll(zq, LANES - ROT_DIM // 2, 1) * s2
        rk = zk * c + pltpu.roll(zk, ROT_DIM // 2, 1) * s1 + pltpu.roll(zk, LANES - ROT_DIM // 2, 1) * s2
        q_ref[:, sl] = (rq * (HEAD_DIM ** -0.5)).astype(q_ref.dtype)
        kf_ref[:, sl] = rk
        kb_ref[:, sl] = rk.astype(BF16)
    v = proj[:, 2 * w:3 * w]
    vf_ref[...] = v
    vb_ref[...] = v.astype(BF16)
    u_ref[...] = proj[:, 3 * w:4 * w].astype(u_ref.dtype)


def _inproj(x2, ln_g, w_bf, rc, rs1, rs2, tm, q_dtype, u_dtype):
    t, d = x2.shape
    w = w_bf.shape[1] // 4
    nt = t // tm
    npos = rc.shape[0] // tm
    row = lambda i: (i, 0)
    pos = lambda i: (i % npos, 0)
    fixed = lambda i: (0, 0)
    outs = [jax.ShapeDtypeStruct((t, w), q_dtype), jax.ShapeDtypeStruct((t, w), F32),
            jax.ShapeDtypeStruct((t, w), F32), jax.ShapeDtypeStruct((t, w), BF16),
            jax.ShapeDtypeStruct((t, w), BF16), jax.ShapeDtypeStruct((t, w), u_dtype)]
    return pl.pallas_call(
        _inproj_kernel,
        grid=(nt,),
        in_specs=[pl.BlockSpec((tm, d), row), pl.BlockSpec((1, d), fixed),
                  pl.BlockSpec((d, 4 * w), fixed),
                  pl.BlockSpec((tm, LANES), pos), pl.BlockSpec((tm, LANES), pos),
                  pl.BlockSpec((tm, LANES), pos)],
        out_specs=[pl.BlockSpec((tm, w), row)] * 6,
        out_shape=outs,
        compiler_params=_cparams(("parallel",)),
        name="inproj",
    )(x2, ln_g, w_bf, rc, rs1, rs2)


def _rope_tables(pos):
    half = ROT_DIM // 2
    inv = ROPE_THETA ** (-jnp.arange(0, ROT_DIM, 2, dtype=F32) / ROT_DIM)
    ang = pos.astype(F32)[:, None] * inv[None, :]
    cos, sin = jnp.cos(ang), jnp.sin(ang)
    n = pos.shape[0]
    pad = jnp.zeros((n, HEAD_DIM - ROT_DIM), F32)
    c = jnp.concatenate([cos, cos, pad + 1.0], axis=1)
    s1 = jnp.concatenate([jnp.zeros((n, half), F32), sin, pad], axis=1)
    s2 = jnp.concatenate([-sin, jnp.zeros((n, half), F32), pad], axis=1)
    rep = LANES // HEAD_DIM
    return tuple(jnp.tile(a, (1, rep)) for a in (c, s1, s2))


def _diff_lambda(lam_ref, lambda_init):
    l = lam_ref[...]
    a = jnp.sum(l[0:1] * l[1:2], axis=-1, keepdims=True)
    b = jnp.sum(l[2:3] * l[3:4], axis=-1, keepdims=True)
    return jnp.exp(a) - jnp.exp(b) + lambda_init


def _attn_kernel(q_ref, k_ref, v_ref, lam_ref, g_ref, o_ref, *, tq, lambda_init):
    qi = pl.program_id(2)
    q = q_ref[0]
    lane = lax.broadcasted_iota(jnp.int32, q.shape, 1)
    zero = jnp.zeros_like(q)
    qc = (jnp.where(lane < HEAD_DIM, q, zero), jnp.where(lane >= HEAD_DIM, q, zero))

    def tile(j, carry, masked):
        k = k_ref[0, pl.ds(pl.multiple_of(j * tq, tq), tq), :]
        v = v_ref[0, pl.ds(pl.multiple_of(j * tq, tq), tq), :]
        new = []
        for c in range(2):
            m, l, acc = carry[3 * c:3 * c + 3]
            s = _nt_dot(qc[c], k)
            if masked:
                r = lax.broadcasted_iota(jnp.int32, s.shape, 0)
                cc = lax.broadcasted_iota(jnp.int32, s.shape, 1)
                s = jnp.where(cc <= r, s, NEG_INF)
            m_new = jnp.maximum(m, jnp.max(s, axis=-1, keepdims=True))
            p = jnp.exp(s - m_new)
            alpha = jnp.exp(m - m_new)
            l = alpha * l + jnp.sum(p, axis=-1, keepdims=True)
            acc = alpha * acc + jnp.dot(p.astype(BF16), v, preferred_element_type=F32)
            new += [m_new, l, acc]
        return tuple(new)

    init = (jnp.full((tq, 1), NEG_INF, F32), jnp.zeros((tq, 1), F32), jnp.zeros((tq, V_DIM), F32)) * 2
    carry = lax.fori_loop(0, qi, lambda j, c: tile(j, c, False), init)
    m0, l0, a0, m1, l1, a1 = tile(qi, carry, True)
    lam = _diff_lambda(lam_ref, lambda_init)
    o = a0 / l0 - lam * (a1 / l1)
    o = o * lax.rsqrt(jnp.mean(o * o, axis=-1, keepdims=True) + EPS)
    o_ref[0] = (o * g_ref[...] * (1.0 - lambda_init)).astype(o_ref.dtype)


def _attn(q, k, v, lam4, subln_g, n_heads, tq, lambda_init):
    b, s, w = q.shape
    qmap = lambda bi, h, i: (bi, i, h)
    kvmap = lambda bi, h, i: (bi, 0, h)
    fixed = lambda bi, h, i: (0, 0)
    return pl.pallas_call(
        functools.partial(_attn_kernel, tq=tq, lambda_init=lambda_init),
        grid=(b, n_heads, s // tq),
        in_specs=[pl.BlockSpec((1, tq, LANES), qmap), pl.BlockSpec((1, s, LANES), kvmap),
                  pl.BlockSpec((1, s, LANES), kvmap), pl.BlockSpec((4, HEAD_DIM), fixed),
                  pl.BlockSpec((1, V_DIM), fixed)],
        out_specs=pl.BlockSpec((1, tq, LANES), qmap),
        out_shape=jax.ShapeDtypeStruct((b, s, w), BF16),
        compiler_params=_cparams(("parallel", "parallel", "arbitrary")),
        name="attn",
    )(q, k, v, lam4, subln_g)


def _split2(x):
    hi = x.astype(BF16)
    return hi, (x - hi.astype(F32)).astype(BF16)


def _head_scores(k, qcol):
    prod = k * qcol
    return jnp.sum(prod.reshape(prod.shape[0] // HEAD_DIM, HEAD_DIM, prod.shape[1]), axis=1)


def _decode_kernel(pt_ref, q_ref, kn_ref, vn_ref, lam_ref, g_ref, *refs, pp, n_heads, lambda_init):
    del pt_ref
    k_refs, v_refs = refs[:pp], refs[pp:2 * pp]
    o_ref = refs[2 * pp]
    m_ref, l_ref, acc_ref, tmp_ref = refs[2 * pp + 1:]
    j = pl.program_id(1)
    nhc = 2 * n_heads

    @pl.when(j == 0)
    def _():
        m_ref[...] = jnp.full_like(m_ref, NEG_INF)
        l_ref[...] = jnp.zeros_like(l_ref)
        acc_ref[...] = jnp.zeros_like(acc_ref)

    qcol = q_ref[0]
    s = [_head_scores(kr[...], qcol) for kr in k_refs]
    m_old = m_ref[:, 0:1]
    smax = functools.reduce(jnp.maximum, s)
    m_new = jnp.maximum(m_old, jnp.max(smax, axis=-1, keepdims=True))
    alpha = jnp.exp(m_old - m_new)
    p = [jnp.exp(si - m_new) for si in s]
    psum = functools.reduce(lambda a, b: a + b, p)
    l_ref[...] = jnp.broadcast_to(alpha * l_ref[:, 0:1] + jnp.sum(psum, axis=-1, keepdims=True), l_ref.shape)
    m_ref[...] = jnp.broadcast_to(m_new, m_ref.shape)
    row_head = lax.broadcasted_iota(jnp.int32, (nhc, V_DIM), 0) // 2
    acc = alpha * acc_ref[...]
    page = k_refs[0].shape[1]
    for pi, vr in zip(p, v_refs):
        pb = pi.astype(BF16)
        for h in range(n_heads):
            vh = vr[pl.ds(h, page, stride=n_heads), :].astype(BF16)
            r = jnp.dot(pb, vh, preferred_element_type=F32)
            acc = acc + jnp.where(row_head == h, r, 0.0)
    acc_ref[...] = acc

    @pl.when(j == pl.num_programs(1) - 1)
    def _():
        s_new = _head_scores(kn_ref[0], qcol)[:, 0:1]
        m_old = m_ref[:, 0:1]
        m_f = jnp.maximum(m_old, s_new)
        a = jnp.exp(m_old - m_f)
        pn = jnp.exp(s_new - m_f)
        l_f = a * l_ref[:, 0:1] + pn
        tmp_ref[...] = (a * acc_ref[...] + pn * vn_ref[0]) / l_f
        o0 = tmp_ref[pl.ds(0, n_heads, stride=2), :]
        o1 = tmp_ref[pl.ds(1, n_heads, stride=2), :]
        o = o0 - _diff_lambda(lam_ref, lambda_init) * o1
        o = o * lax.rsqrt(jnp.mean(o * o, axis=-1, keepdims=True) + EPS)
        o_ref[0] = o * g_ref[...] * (1.0 - lambda_init)


def _decode_attn(page_table, qcol, kn_col, vn2, lam4, subln_g, cache_kt, cache_vr, pp, lambda_init):
    bd, n_pages = page_table.shape
    _, w, page = cache_kt.shape
    n_heads = cache_vr.shape[1] // page
    nhc = 2 * n_heads
    pt_flat = page_table.reshape(-1)
    fixed2 = lambda b, j, pt: (0, 0)
    perb = lambda b, j, pt: (b, 0, 0)

    def page_spec(i, rows, cols):
        return pl.BlockSpec((None, rows, cols), lambda b, j, pt: (pt[b * n_pages + j * pp + i], 0, 0))

    grid_spec = pltpu.PrefetchScalarGridSpec(
        num_scalar_prefetch=1,
        grid=(bd, n_pages // pp),
        in_specs=[pl.BlockSpec((1, w, LANES), perb), pl.BlockSpec((1, w, LANES), perb),
                  pl.BlockSpec((1, nhc, V_DIM), perb),
                  pl.BlockSpec((4, HEAD_DIM), fixed2), pl.BlockSpec((1, V_DIM), fixed2)]
                 + [page_spec(i, w, page) for i in range(pp)]
                 + [page_spec(i, page * n_heads, V_DIM) for i in range(pp)],
        out_specs=pl.BlockSpec((1, n_heads, V_DIM), perb),
        scratch_shapes=[pltpu.VMEM((nhc, LANES), F32), pltpu.VMEM((nhc, LANES), F32),
                        pltpu.VMEM((nhc, V_DIM), F32), pltpu.VMEM((nhc, V_DIM), F32)],
    )
    return pl.pallas_call(
        functools.partial(_decode_kernel, pp=pp, n_heads=n_heads, lambda_init=lambda_init),
        grid_spec=grid_spec,
        out_shape=jax.ShapeDtypeStruct((bd, n_heads, V_DIM), F32),
        compiler_params=_cparams(("parallel", "arbitrary")),
        name="decode_attn",
    )(pt_flat, qcol, kn_col, vn2, lam4, subln_g, *([cache_kt] * pp), *([cache_vr] * pp))


def _s5prep_kernel(arow_ref, acol_ref, ldt_ref, bt_ref, ct_ref, d_ref,
                   mt_ref, w_ref, vt_ref, lam_ref):
    L, hg, p = S5_CHUNK, S5_GROUP, S5_STATE
    dt = jnp.exp(ldt_ref[0])
    ar, ai = arow_ref[0, 0:1], arow_ref[0, 1:2]
    arc, aic = acol_ref[0, :, 0:1], acol_ref[0, :, 1:2]

    def powers(a_r, a_i, j):
        mag = jnp.exp(j * (a_r * dt))
        return mag * jnp.cos(j * (a_i * dt)), mag * jnp.sin(j * (a_i * dt))

    l1r, l1i = powers(ar, ai, 1.0)
    llr, lli = powers(ar, ai, float(L))
    lam_ref[0] = jnp.concatenate([llr, lli, l1r, l1i, jnp.zeros((4, p), F32)], axis=0)
    den = ar * ar + ai * ai
    cr = ((l1r - 1.0) * ar + l1i * ai) / den
    ci = (l1i * ar - (l1r - 1.0) * ai) / den
    btr, bti = bt_ref[0, 0], bt_ref[0, 1]
    bbr = btr * cr - bti * ci
    bbi = btr * ci + bti * cr
    jrow = lax.broadcasted_iota(jnp.int32, (L, p), 0).astype(F32)
    pr, pi = powers(ar, ai, jrow)
    for s in range(L):
        qr, qi = pr[L - 1 - s:L - s], pi[L - 1 - s:L - s]
        w_ref[0, 0, s * hg:(s + 1) * hg, :] = bbr * qr - bbi * qi
        w_ref[0, 1, s * hg:(s + 1) * hg, :] = bbr * qi + bbi * qr
    jl = (lax.broadcasted_iota(jnp.int32, (p, L * hg), 1) // hg).astype(F32)
    lpr, lpi = powers(arc, aic, jl)
    ctr, cti = ct_ref[0, 0], ct_ref[0, 1]
    cjr = ctr * lpr - cti * lpi
    cji = ctr * lpi + cti * lpr
    c1r, c1i = powers(arc, aic, 1.0)
    vt_ref[0, 0] = cjr * c1r - cji * c1i
    vt_ref[0, 1] = -(cjr * c1i + cji * c1r)
    kt = (jnp.dot(bbr, cjr, preferred_element_type=F32, precision=_HI)
          - jnp.dot(bbi, cji, preferred_element_type=F32, precision=_HI))
    n = L * hg
    ri = lax.broadcasted_iota(jnp.int32, (n, n), 0)
    cidx = lax.broadcasted_iota(jnp.int32, (n, n), 1)
    dtile = d_ref[0]
    for s in range(L):
        shift = (cidx - ri == s * hg).astype(F32)
        blk = jnp.dot(kt, shift, preferred_element_type=F32, precision=_HI)
        rr = lax.broadcasted_iota(jnp.int32, (hg, n), 0) + s * hg
        cc = lax.broadcasted_iota(jnp.int32, (hg, n), 1)
        mt_ref[0, s * hg:(s + 1) * hg, :] = blk + jnp.where(rr == cc, dtile, 0.0)


def _s5prep(arow, acol, ldt, bt, ct, dt_tiled):
    g = arow.shape[0]
    L, hg, p = S5_CHUNK, S5_GROUP, S5_STATE
    n = L * hg
    i3 = lambda i: (i, 0, 0)
    i4 = lambda i: (i, 0, 0, 0)
    return pl.pallas_call(
        _s5prep_kernel,
        grid=(g,),
        in_specs=[pl.BlockSpec((1, 2, p), i3), pl.BlockSpec((1, p, 2), i3), pl.BlockSpec((1, 1, 1), i3),
                  pl.BlockSpec((1, 2, hg, p), i4), pl.BlockSpec((1, 2, p, n), i4),
                  pl.BlockSpec((1, 1, n), i3)],
        out_specs=[pl.BlockSpec((1, n, n), i3), pl.BlockSpec((1, 2, n, p), i4),
                   pl.BlockSpec((1, 2, p, n), i4), pl.BlockSpec((1, SUBLANES, p), i3)],
        out_shape=[jax.ShapeDtypeStruct((g, n, n), F32), jax.ShapeDtypeStruct((g, 2, n, p), F32),
                   jax.ShapeDtypeStruct((g, 2, p, n), F32), jax.ShapeDtypeStruct((g, SUBLANES, p), F32)],
        compiler_params=_cparams(("parallel",)),
        name="s5prep",
    )(arow, acol, ldt, bt, ct, dt_tiled)


def _s5chunk_kernel(u_ref, mt_ref, w_ref, vt_ref, lam_ref, y_ref, hl_ref, sr_ref, si_ref):
    u = u_ref[0, 0]
    sr_ref[...] = jnp.dot(u, w_ref[0, 0].astype(BF16), preferred_element_type=F32)
    si_ref[...] = jnp.dot(u, w_ref[0, 1].astype(BF16), preferred_element_type=F32)
    lr = jnp.broadcast_to(lam_ref[0, 0:1], (SUBLANES, S5_STATE))
    li = jnp.broadcast_to(lam_ref[0, 1:2], (SUBLANES, S5_STATE))
    nc = u.shape[0] // SUBLANES

    def step(c, h):
        hr, hi = h
        rows = pl.ds(pl.multiple_of(c * SUBLANES, SUBLANES), SUBLANES)
        s_r, s_i = sr_ref[rows, :], si_ref[rows, :]
        sr_ref[rows, :] = hr
        si_ref[rows, :] = hi
        return lr * hr - li * hi + s_r, lr * hi + li * hr + s_i

    z = jnp.zeros((SUBLANES, S5_STATE), F32)
    hr, hi = lax.fori_loop(0, nc, step, (z, z), unroll=8)
    hl_ref[0, 0, 0] = hr
    hl_ref[0, 0, 1] = hi
    y = jnp.dot(u, mt_ref[0].astype(BF16), preferred_element_type=F32)
    y += jnp.dot(sr_ref[...].astype(BF16), vt_ref[0, 0].astype(BF16), preferred_element_type=F32)
    y += jnp.dot(si_ref[...].astype(BF16), vt_ref[0, 1].astype(BF16), preferred_element_type=F32)
    y_ref[0, 0] = y.astype(y_ref.dtype)


def _s5chunk(u4, mt, w, vt, lam):
    nbh, g, rows, n = u4.shape
    p = S5_STATE
    um = lambda b, gi: (b, gi, 0, 0)
    g3 = lambda b, gi: (gi, 0, 0)
    g4 = lambda b, gi: (gi, 0, 0, 0)
    return pl.pallas_call(
        _s5chunk_kernel,
        grid=(nbh, g),
        in_specs=[pl.BlockSpec((1, 1, rows, n), um), pl.BlockSpec((1, n, n), g3),
                  pl.BlockSpec((1, 2, n, p), g4), pl.BlockSpec((1, 2, p, n), g4),
                  pl.BlockSpec((1, SUBLANES, p), g3)],
        out_specs=[pl.BlockSpec((1, 1, rows, n), um),
                   pl.BlockSpec((1, 1, 2, SUBLANES, p), lambda b, gi: (b, gi, 0, 0, 0))],
        out_shape=[jax.ShapeDtypeStruct(u4.shape, BF16),
                   jax.ShapeDtypeStruct((nbh, g, 2, SUBLANES, p), F32)],
        scratch_shapes=[pltpu.VMEM((rows, p), F32), pltpu.VMEM((rows, p), F32)],
        compiler_params=_cparams(("parallel", "parallel")),
        name="s5chunk",
    )(u4, mt, w, vt, lam)


def _s5step_kernel(u_ref, h0_ref, bb_ref, lam_ref, c_ref, d_ref, y_ref, h_ref):
    u = u_ref[0]
    h0r, h0i = h0_ref[0, 0], h0_ref[0, 1]
    l1r, l1i = lam_ref[0, 2:3], lam_ref[0, 3:4]
    bur = jnp.dot(u, bb_ref[0, 0], preferred_element_type=F32, precision=_HI)
    bui = jnp.dot(u, bb_ref[0, 1], preferred_element_type=F32, precision=_HI)
    hr = l1r * h0r - l1i * h0i + bur
    hi = l1r * h0i + l1i * h0r + bui
    h_ref[0, 0] = hr
    h_ref[0, 1] = hi
    y = _nt_dot(hr, c_ref[0, 0], precision=_HI) - _nt_dot(hi, c_ref[0, 1], precision=_HI)
    y_ref[0] = y + d_ref[0] * u


def _s5step(u3, h0, bb, lam, c2, d3):
    g, bd, hg = u3.shape
    p = S5_STATE
    i3 = lambda i: (i, 0, 0)
    i4 = lambda i: (i, 0, 0, 0)
    return pl.pallas_call(
        _s5step_kernel,
        grid=(g,),
        in_specs=[pl.BlockSpec((1, bd, hg), i3), pl.BlockSpec((1, 2, bd, p), i4),
                  pl.BlockSpec((1, 2, hg, p), i4), pl.BlockSpec((1, SUBLANES, p), i3),
                  pl.BlockSpec((1, 2, hg, p), i4), pl.BlockSpec((1, 1, hg), i3)],
        out_specs=[pl.BlockSpec((1, bd, hg), i3), pl.BlockSpec((1, 2, bd, p), i4)],
        out_shape=[jax.ShapeDtypeStruct((g, bd, hg), F32), jax.ShapeDtypeStruct((g, 2, bd, p), F32)],
        compiler_params=_cparams(("parallel",)),
        name="s5step",
    )(u3, h0, bb, lam, c2, d3)


def _postmix_kernel(x_ref, a_ref, y_ref, wglu_ref, bglu_ref, wout_ref, g2_ref, wr_ref, br_ref,
                    x1_ref, h2_ref, comb_ref):
    wa = a_ref.shape[-1]
    y = jax.nn.gelu(y_ref[...].astype(F32))
    z = jnp.dot(y.astype(BF16), wglu_ref[...], preferred_element_type=F32) + bglu_ref[...]
    s5o = y * jax.nn.sigmoid(z)
    mix = jnp.dot(a_ref[...].astype(BF16), wout_ref[0:wa, :], preferred_element_type=F32)
    mix += jnp.dot(s5o.astype(BF16), wout_ref[wa:, :], preferred_element_type=F32)
    x1 = x_ref[...] + mix
    x1_ref[...] = x1
    h2 = x1 * lax.rsqrt(jnp.mean(x1 * x1, axis=-1, keepdims=True) + EPS) * g2_ref[...]
    h2_ref[...] = h2.astype(h2_ref.dtype)
    logits = jnp.dot(h2, wr_ref[...], preferred_element_type=F32, precision=_HI) + br_ref[...]
    ne = N_GROUPS_MOE * EXP_PER_GROUP
    lane = lax.broadcasted_iota(jnp.int32, logits.shape, 1).astype(F32)
    big = jnp.float32(1 << 20)
    gmask = (lane >= ne) & (lane < ne + N_GROUPS_MOE)
    gl = jnp.where(gmask, logits, NEG_INF)
    gmax = jnp.max(gl, axis=-1, keepdims=True)
    gidx = jnp.min(jnp.where(gl == gmax, lane, big), axis=-1, keepdims=True) - ne
    g_w = 1.0 / jnp.sum(jnp.where(gmask, jnp.exp(logits - gmax), 0.0), axis=-1, keepdims=True)
    lo = gidx * EXP_PER_GROUP
    el = jnp.where((lane >= lo) & (lane < lo + EXP_PER_GROUP), logits, NEG_INF)
    e1 = jnp.max(el, axis=-1, keepdims=True)
    i1 = jnp.min(jnp.where(el == e1, lane, big), axis=-1, keepdims=True)
    el2 = jnp.where(lane == i1, NEG_INF, el)
    e2 = jnp.max(el2, axis=-1, keepdims=True)
    i2 = jnp.min(jnp.where(el2 == e2, lane, big), axis=-1, keepdims=True)
    r = jnp.exp(e2 - e1)
    w1 = g_w / (1.0 + r)
    w2 = g_w * r / (1.0 + r)
    comb_ref[...] = (jnp.where(lane == i1, w1, 0.0) + jnp.where(lane == i2, w2, 0.0)
                     + jnp.where(lane == GID_LANE, gidx, 0.0))


def _postmix(x2, attn, ys5, wglu_bf, bglu, wout_bf, ln2_g, w_router, b_router, tm):
    t, d = x2.shape
    wa = attn.shape[1]
    ws = ys5.shape[1]
    row = lambda i: (i, 0)
    fixed = lambda i: (0, 0)
    return pl.pallas_call(
        _postmix_kernel,
        grid=(t // tm,),
        in_specs=[pl.BlockSpec((tm, d), row), pl.BlockSpec((tm, wa), row), pl.BlockSpec((tm, ws), row),
                  pl.BlockSpec((ws, ws), fixed), pl.BlockSpec((1, ws), fixed),
                  pl.BlockSpec((wa + ws, d), fixed), pl.BlockSpec((1, d), fixed),
                  pl.BlockSpec((d, LANES), fixed), pl.BlockSpec((1, LANES), fixed)],
        out_specs=[pl.BlockSpec((tm, d), row), pl.BlockSpec((tm, d), row), pl.BlockSpec((tm, LANES), row)],
        out_shape=[jax.ShapeDtypeStruct((t, d), F32), jax.ShapeDtypeStruct((t, d), BF16),
                   jax.ShapeDtypeStruct((t, LANES), F32)],
        compiler_params=_cparams(("parallel",)),
        name="postmix",
    )(x2, attn, ys5, wglu_bf, bglu, wout_bf, ln2_g, w_router, b_router)


def _moe_kernel(h_ref, x1_ref, comb_ref, wg_ref, wu_ref, wd_ref, gf_ref, o_ref, acc_ref):
    e = pl.program_id(1)

    @pl.when(e == 0)
    def _():
        acc_ref[...] = jnp.zeros_like(acc_ref)

    h = h_ref[...]
    he = jax.nn.silu(jnp.dot(h, wg_ref[0], preferred_element_type=F32))
    he = he * jnp.dot(h, wu_ref[0], preferred_element_type=F32)
    comb = comb_ref[...]
    lane = lax.broadcasted_iota(jnp.int32, comb.shape, 1)
    ce = jnp.sum(jnp.where(lane == e, comb, 0.0), axis=-1, keepdims=True)
    acc_ref[...] += jnp.dot((he * ce).astype(BF16), wd_ref[0], preferred_element_type=F32)

    @pl.when(e == pl.num_programs(1) - 1)
    def _():
        x2 = x1_ref[...] + acc_ref[...]
        o_ref[...] = x2 * lax.rsqrt(jnp.mean(x2 * x2, axis=-1, keepdims=True) + EPS) * gf_ref[...]


def _moe(h2, x1, comb, wg_bf, wu_bf, wd_bf, lnf_g, tm):
    t, d = x1.shape
    ne, _, de = wg_bf.shape
    row = lambda i, e: (i, 0)
    return pl.pallas_call(
        _moe_kernel,
        grid=(t // tm, ne),
        in_specs=[pl.BlockSpec((tm, d), row), pl.BlockSpec((tm, d), row), pl.BlockSpec((tm, LANES), row),
                  pl.BlockSpec((1, d, de), lambda i, e: (e, 0, 0)),
                  pl.BlockSpec((1, d, de), lambda i, e: (e, 0, 0)),
                  pl.BlockSpec((1, de, d), lambda i, e: (e, 0, 0)),
                  pl.BlockSpec((1, d), lambda i, e: (0, 0))],
        out_specs=pl.BlockSpec((tm, d), row),
        out_shape=jax.ShapeDtypeStruct((t, d), F32),
        scratch_shapes=[pltpu.VMEM((tm, d), F32)],
        compiler_params=_cparams(("parallel", "arbitrary")),
        name="moe",
    )(h2, x1, comb, wg_bf, wu_bf, wd_bf, lnf_g)


def _moe_grouped_kernel(h_ref, x1_ref, comb_ref, ut_ref, lt_ref, wg_ref, wu_ref, wd_ref, gf_ref, o_ref):
    g = pl.program_id(1)
    tm, d = h_ref.shape
    gf32 = g.astype(F32)
    comb = comb_ref[...]
    combt = comb.T
    memc = comb[:, GID_LANE:GID_LANE + 1] == gf32
    memr = combt[GID_LANE:GID_LANE + 1, :] == gf32
    memr8 = jnp.broadcast_to(jnp.where(memr, 1.0, 0.0), (SUBLANES, tm)).astype(BF16)
    memcb = jnp.broadcast_to(jnp.where(memc, 1.0, 0.0), (tm, LANES)).astype(BF16)
    rank_row = jnp.dot(memr8, ut_ref[...], preferred_element_type=F32)[0:1]
    rank_col = jnp.dot(lt_ref[...], memcb, preferred_element_type=F32)[:, 0:1]
    posr = jnp.where(memr, rank_row - 1.0, -1.0)
    posc = jnp.where(memc, rank_col - 1.0, -1.0)
    n_rows = jnp.max(rank_row).astype(jnp.int32)
    big = 2 * MOE_CHUNK
    n_big = n_rows // big
    n_small = (n_rows - n_big * big + (MOE_CHUNK - 1)) // MOE_CHUNK
    chi, clo = _split2(combt)

    @pl.when(g == 0)
    def _():
        o_ref[...] = jnp.zeros_like(o_ref)

    def chunk(row0, ch):
        base = row0.astype(F32)
        rid = lax.broadcasted_iota(jnp.int32, (ch, tm), 0).astype(F32) + base
        cid = lax.broadcasted_iota(jnp.int32, (tm, ch), 1).astype(F32) + base
        sel = jnp.where(posr == rid, 1.0, 0.0).astype(BF16)
        selt = jnp.where(posc == cid, 1.0, 0.0).astype(BF16)
        xg = jnp.dot(sel, h_ref[...], preferred_element_type=F32).astype(BF16)
        cg = _nt_dot(sel, chi) + _nt_dot(sel, clo)
        lane = lax.broadcasted_iota(jnp.int32, cg.shape, 1).astype(F32)
        z = jnp.zeros((ch, d), F32)
        for e in range(EXP_PER_GROUP):
            he = jax.nn.silu(jnp.dot(xg, wg_ref[e], preferred_element_type=F32))
            he = he * jnp.dot(xg, wu_ref[e], preferred_element_type=F32)
            ce = jnp.sum(jnp.where(lane == gf32 * EXP_PER_GROUP + e, cg, 0.0), axis=-1, keepdims=True)
            z = z + jnp.dot((he * ce).astype(BF16), wd_ref[e], preferred_element_type=F32)
        o_ref[...] += jnp.dot(selt, z.astype(BF16), preferred_element_type=F32)

    def big_body(k, carry):
        chunk(k * big, big)
        return carry

    def small_body(k, carry):
        chunk(n_big * big + k * MOE_CHUNK, MOE_CHUNK)
        return carry

    lax.fori_loop(0, n_big, big_body, 0)
    lax.fori_loop(0, n_small, small_body, 0)

    @pl.when(g == pl.num_programs(1) - 1)
    def _():
        x2 = x1_ref[...] + o_ref[...]
        o_ref[...] = x2 * lax.rsqrt(jnp.mean(x2 * x2, axis=-1, keepdims=True) + EPS) * gf_ref[...]


def _moe_grouped(h2, x1, comb, wg_bf, wu_bf, wd_bf, lnf_g, tm):
    t, d = x1.shape
    ne, _, de = wg_bf.shape
    ng = ne // EXP_PER_GROUP
    tri = jnp.arange(tm)[:, None] <= jnp.arange(tm)[None, :]
    ut = tri.astype(BF16)
    lt = tri.T.astype(BF16)
    once = pl.Buffered(1)
    tile = lambda i, g: (i, 0)
    fixed = lambda i, g: (0, 0)
    wmap = lambda i, g: (g, 0, 0)
    return pl.pallas_call(
        _moe_grouped_kernel,
        grid=(t // tm, ng),
        in_specs=[pl.BlockSpec((tm, d), tile), pl.BlockSpec((tm, d), tile),
                  pl.BlockSpec((tm, LANES), tile),
                  pl.BlockSpec((tm, tm), fixed, pipeline_mode=once),
                  pl.BlockSpec((tm, tm), fixed, pipeline_mode=once),
                  pl.BlockSpec((EXP_PER_GROUP, d, de), wmap),
                  pl.BlockSpec((EXP_PER_GROUP, d, de), wmap),
                  pl.BlockSpec((EXP_PER_GROUP, de, d), wmap),
                  pl.BlockSpec((1, d), fixed)],
        out_specs=pl.BlockSpec((tm, d), tile),
        out_shape=jax.ShapeDtypeStruct((t, d), F32),
        compiler_params=_cparams(("parallel", "arbitrary")),
        name="moe_grouped",
    )(h2, x1, comb, ut, lt, wg_bf, wu_bf, wd_bf, lnf_g)


def _tile(n, want):
    t = min(n, want)
    while n % t:
        t //= 2
    return t


def kernel(x_prompt, x_sample, cache_k, cache_v, state_s5_re, state_s5_im, page_table, ln1_g, w_in, lambda_q1, lambda_k1, lambda_q2, lambda_k2, subln_g, s5_a_re, s5_a_im, s5_log_dt, s5_b_re, s5_b_im, s5_c_re, s5_c_im, s5_d, w_glu, b_glu, w_out, ln2_g, w_router_group, b_router_group, w_router_expert, b_router_expert, w_gate, w_up, w_down, ln_f_g):
    depth = ln1_g.shape[0]
    assert depth == 1, "single-layer step"
    b, s, d = x_prompt.shape
    bd, ds, _ = x_sample.shape
    assert ds == 1
    n_pages = page_table.shape[1]
    page = cache_k.shape[2]
    past_len = n_pages * page
    wq = w_in.shape[2] // 4
    n_heads = wq // V_DIM
    g = s5_a_re.shape[1]
    p = S5_STATE
    hg = S5_GROUP
    L = S5_CHUNK
    n = L * hg
    assert b % SUBLANES == 0 and s % L == 0 and g * hg == wq
    lambda_init = 0.8 - 0.6 * math.exp(-0.3 * 0)

    w_in_bf = w_in[0].astype(BF16)
    wglu_bf = w_glu[0].astype(BF16)
    wout_bf = w_out[0].astype(BF16)
    wg_bf, wu_bf, wd_bf = (w[0].astype(BF16) for w in (w_gate, w_up, w_down))
    ne = w_gate.shape[1]
    w_router = jnp.concatenate(
        [jnp.transpose(w_router_expert[0], (1, 0, 2)).reshape(d, ne), w_router_group[0],
         jnp.zeros((d, LANES - ne - N_GROUPS_MOE), F32)], axis=1)
    b_router = jnp.concatenate([b_router_expert[0].reshape(ne), b_router_group[0],
                                jnp.zeros((LANES - ne - N_GROUPS_MOE,), F32)])[None, :]
    lam4 = jnp.concatenate([lambda_q1, lambda_k1, lambda_q2, lambda_k2], axis=0)
    ln1 = ln1_g[0][None, :]
    ln2 = ln2_g[0][None, :]
    lnf = ln_f_g[None, :]
    subg = subln_g[0][None, :]
    bglu = b_glu[0][None, :]

    arow = jnp.stack([s5_a_re[0], s5_a_im[0]], axis=1)
    acol = jnp.stack([s5_a_re[0], s5_a_im[0]], axis=2)
    ldt = s5_log_dt[0].reshape(g, 1, 1)
    bt = jnp.stack([jnp.swapaxes(s5_b_re[0], 1, 2), jnp.swapaxes(s5_b_im[0], 1, 2)], axis=1)
    ct = jnp.stack([jnp.tile(jnp.swapaxes(s5_c_re[0], 1, 2), (1, 1, L)),
                    jnp.tile(jnp.swapaxes(s5_c_im[0], 1, 2), (1, 1, L))], axis=1)
    d_g = s5_d[0].reshape(g, 1, hg)
    mt, w_s5, vt, lam_s5 = _s5prep(arow, acol, ldt, bt, ct, jnp.tile(d_g, (1, 1, L)))

    tm = _tile(b * s, 512)
    x2 = x_prompt.reshape(b * s, d)
    rc, rs1, rs2 = _rope_tables(jnp.arange(s, dtype=jnp.int32))
    q, kf, vf, kb, vb, u = _inproj(x2, ln1, w_in_bf, rc, rs1, rs2, _tile(s, tm), BF16, BF16)
    tq = _tile(s, 256)
    attn = _attn(q.reshape(b, s, wq), kb.reshape(b, s, wq), vb.reshape(b, s, wq), lam4, subg,
                 n_heads, tq, lambda_init)
    nbh, nc = b // SUBLANES, s // L
    u4 = u.reshape(nbh, SUBLANES, nc, L, g, hg).transpose(0, 4, 2, 1, 3, 5).reshape(nbh, g, nc * SUBLANES, n)
    y4, hl = _s5chunk(u4, mt, w_s5, vt, lam_s5)
    ys5 = y4.reshape(nbh, g, nc, SUBLANES, L, hg).transpose(0, 3, 2, 4, 1, 5).reshape(b * s, wq)
    hl = hl.transpose(2, 0, 3, 1, 4).reshape(2, b, g, p)
    x1, h2, comb = _postmix(x2, attn.reshape(b * s, wq), ys5, wglu_bf, bglu, wout_bf, ln2,
                            w_router, b_router, tm)
    y_prompt = _moe_grouped(h2, x1, comb, wg_bf, wu_bf, wd_bf, lnf, _tile(b * s, 1024)).reshape(b, s, d)

    xs2 = x_sample.reshape(bd, d)
    pos_s = jnp.full((bd,), past_len, jnp.int32)
    sc, ss1, ss2 = _rope_tables(pos_s)
    qs, kfs, vfs, _, _, us = _inproj(xs2, ln1, w_in_bf, sc, ss1, ss2, bd, F32, F32)
    cache_kt = jnp.transpose(cache_k[0], (0, 2, 3, 1)).reshape(-1, wq, page)
    cache_vr = cache_v[0].reshape(-1, page * n_heads, V_DIM)
    qcol = jnp.broadcast_to(qs[:, :, None], (bd, wq, LANES))
    kn_col = jnp.broadcast_to(kfs[:, :, None], (bd, wq, LANES))
    vn2 = jnp.repeat(vfs.reshape(bd, n_heads, V_DIM), 2, axis=1)
    pp = _tile(n_pages, 8)
    attn_s = _decode_attn(page_table, qcol, kn_col, vn2, lam4, subg, cache_kt, cache_vr, pp, lambda_init)
    u3 = us.reshape(bd, g, hg).transpose(1, 0, 2)
    h0 = jnp.stack([state_s5_re[0], state_s5_im[0]]).transpose(2, 0, 1, 3)
    bb = w_s5[:, :, (L - 1) * hg:, :]
    c2 = jnp.stack([s5_c_re[0], s5_c_im[0]], axis=1)
    ys3, hs = _s5step(u3, h0, bb, lam_s5, c2, d_g)
    ys_s5 = ys3.transpose(1, 0, 2).reshape(bd, wq)
    x1s, h2s, combs = _postmix(xs2, attn_s.reshape(bd, wq), ys_s5, wglu_bf, bglu, wout_bf, ln2,
                               w_router, b_router, bd)
    y_sample = _moe(h2s, x1s, combs, wg_bf, wu_bf, wd_bf, lnf, bd).reshape(bd, 1, d)
    hs = hs.transpose(1, 2, 0, 3)

    return (y_prompt, y_sample,
            kf.reshape(1, b, s, 2 * n_heads, HEAD_DIM), vf.reshape(1, b, s, n_heads, V_DIM),
            hl[0][None], hl[1][None],
            kfs.reshape(1, bd, 1, 2 * n_heads, HEAD_DIM), vfs.reshape(1, bd, 1, n_heads, V_DIM),
            hs[0][None], hs[1][None])
```

```python
import functools
import math

import jax
import jax.numpy as jnp
from jax import lax
from jax.experimental import pallas as pl
from jax.experimental.pallas import tpu as pltpu

F32 = jnp.float32
BF16 = jnp.bfloat16

HEAD_DIM = 64
V_DIM = 128
ROT_DIM = 16
ROPE_THETA = 500000.0
S5_GROUP = 16
S5_STATE = 64
S5_CHUNK = 16
SUBLANES = 8
N_GROUPS_MOE = 4
EXP_PER_GROUP = 8
EPS = 1e-5
NEG_INF = -1e30
LANES = 128
GID_LANE = LANES - 1
MOE_CHUNK = 128
VMEM_LIMIT = 56 * 1024 * 1024

_HI = lax.Precision.HIGHEST


def _cparams(sem):
    return pltpu.CompilerParams(dimension_semantics=sem, vmem_limit_bytes=VMEM_LIMIT)


def _nt_dot(a, b, **kw):
    return lax.dot_general(a, b, (((1,), (1,)), ((), ())), preferred_element_type=F32, **kw)


def _inproj_kernel(x_ref, g_ref, w_ref, c_ref, s1_ref, s2_ref,
                   q_ref, kf_ref, vf_ref, kb_ref, vb_ref, u_ref):
    x = x_ref[...]
    ms = jnp.mean(x * x, axis=-1, keepdims=True)
    hn = (x * lax.rsqrt(ms + EPS) * g_ref[...]).astype(BF16)
    proj = jnp.dot(hn, w_ref[...], preferred_element_type=F32)
    w = q_ref.shape[-1]
    c = c_ref[...]
    s1 = s1_ref[...]
    s2 = s2_ref[...]
    for j in range(w // LANES):
        sl = slice(j * LANES, (j + 1) * LANES)
        zq = proj[:, j * LANES:(j + 1) * LANES]
        zk = proj[:, w + j * LANES:w + (j + 1) * LANES]
        rq = zq * c + pltpu.roll(zq, ROT_DIM // 2, 1) * s1 + pltpu.roll(zq, LANES - ROT_DIM // 2, 1) * s2
        rk = zk * c + pltpu.roll(zk, ROT_DIM // 2, 1) * s1 + pltpu.roll(zk, LANES - ROT_DIM // 2, 1) * s2
        q_ref[:, sl] = (rq * (HEAD_DIM ** -0.5)).astype(q_ref.dtype)
        kf_ref[:, sl] = rk
        kb_ref[:, sl] = rk.astype(BF16)
    v = proj[:, 2 * w:3 * w]
    vf_ref[...] = v
    vb_ref[...] = v.astype(BF16)
    u_ref[...] = proj[:, 3 * w:4 * w].astype(u_ref.dtype)


def _inproj(x2, ln_g, w_bf, rc, rs1, rs2, tm, q_dtype, u_dtype):
    t, d = x2.shape
    w = w_bf.shape[1] // 4
    nt = t // tm
    npos = rc.shape[0] // tm
    row = lambda i: (i, 0)
    pos = lambda i: (i % npos, 0)
    fixed = lambda i: (0, 0)
    outs = [jax.ShapeDtypeStruct((t, w), q_dtype), jax.ShapeDtypeStruct((t, w), F32),
            jax.ShapeDtypeStruct((t, w), F32), jax.ShapeDtypeStruct((t, w), BF16),
            jax.ShapeDtypeStruct((t, w), BF16), jax.ShapeDtypeStruct((t, w), u_dtype)]
    return pl.pallas_call(
        _inproj_kernel,
        grid=(nt,),
        in_specs=[pl.BlockSpec((tm, d), row), pl.BlockSpec((1, d), fixed),
                  pl.BlockSpec((d, 4 * w), fixed),
                  pl.BlockSpec((tm, LANES), pos), pl.BlockSpec((tm, LANES), pos),
                  pl.BlockSpec((tm, LANES), pos)],
        out_specs=[pl.BlockSpec((tm, w), row)] * 6,
        out_shape=outs,
        compiler_params=_cparams(("parallel",)),
        name="inproj",
    )(x2, ln_g, w_bf, rc, rs1, rs2)


def _inproj_prompt_kernel(x_ref, g_ref, w_ref, wkt_ref, c_ref, s1_ref, s2_ref, ct_ref, st_ref,
                          q_ref, vf_ref, vb_ref, ut_ref, ktf_ref, ktb_ref):
    x = x_ref[0]
    ms = jnp.mean(x * x, axis=-1, keepdims=True)
    hn = (x * lax.rsqrt(ms + EPS) * g_ref[...]).astype(BF16)
    proj = jnp.dot(hn, w_ref[...], preferred_element_type=F32)
    w = q_ref.shape[-1]
    ktu = _nt_dot(wkt_ref[...], hn)
    kt = ktu[:w]
    ut_ref[0] = ktu[w:].astype(ut_ref.dtype)
    c, s1, s2 = c_ref[...], s1_ref[...], s2_ref[...]
    for j in range(w // LANES):
        sl = slice(j * LANES, (j + 1) * LANES)
        zq = proj[:, sl]
        rq = zq * c + pltpu.roll(zq, ROT_DIM // 2, 1) * s1 + pltpu.roll(zq, LANES - ROT_DIM // 2, 1) * s2
        q_ref[0, :, sl] = (rq * (HEAD_DIM ** -0.5)).astype(q_ref.dtype)
    v = proj[:, w:2 * w]
    vf_ref[0] = v
    vb_ref[0] = v.astype(BF16)
    ct, st = ct_ref[...], st_ref[...]
    half = ROT_DIM // 2
    tk = ktb_ref.shape[-1]
    for hc in range(w // HEAD_DIM):
        base = hc * HEAD_DIM
        x1, x2 = kt[base:base + half], kt[base + half:base + ROT_DIM]
        blk = jnp.concatenate([x1 * ct - x2 * st, x2 * ct + x1 * st, kt[base + ROT_DIM:base + HEAD_DIM]], axis=0)
        ktf_ref[0, base:base + HEAD_DIM, :] = blk
        for t in range(blk.shape[1] // tk):
            ktb_ref[0, t, base:base + HEAD_DIM, :] = blk[:, t * tk:(t + 1) * tk].astype(BF16)


def _inproj_prompt(x3, ln_g, w_qv_bf, wkut_bf, rc, rs1, rs2, rct, rst, tm, tk):
    b, s, d = x3.shape
    w = wkut_bf.shape[0] // 2
    half = ROT_DIM // 2
    tok = lambda bi, i: (bi, i, 0)
    pos = lambda bi, i: (i, 0)
    fixed = lambda bi, i: (0, 0)
    outs = [jax.ShapeDtypeStruct((b, s, w), BF16), jax.ShapeDtypeStruct((b, s, w), F32),
            jax.ShapeDtypeStruct((b, s, w), BF16), jax.ShapeDtypeStruct((b, w, s), BF16),
            jax.ShapeDtypeStruct((b, w, s), F32), jax.ShapeDtypeStruct((b, s // tk, w, tk), BF16)]
    return pl.pallas_call(
        _inproj_prompt_kernel,
        grid=(b, s // tm),
        in_specs=[pl.BlockSpec((1, tm, d), tok), pl.BlockSpec((1, d), fixed),
                  pl.BlockSpec((d, 2 * w), fixed), pl.BlockSpec((2 * w, d), fixed),
                  pl.BlockSpec((tm, LANES), pos), pl.BlockSpec((tm, LANES), pos), pl.BlockSpec((tm, LANES), pos),
                  pl.BlockSpec((half, tm), lambda bi, i: (0, i)), pl.BlockSpec((half, tm), lambda bi, i: (0, i))],
        out_specs=[pl.BlockSpec((1, tm, w), tok)] * 3
                  + [pl.BlockSpec((1, w, tm), lambda bi, i: (bi, 0, i))] * 2
                  + [
                     pl.BlockSpec((1, tm // tk, w, tk), lambda bi, i: (bi, i, 0, 0))],
        out_shape=outs,
        compiler_params=_cparams(("parallel", "parallel")),
        name="inproj_prompt",
    )(x3, ln_g, w_qv_bf, wkut_bf, rc, rs1, rs2, rct, rst)


def _rope_tables_t(pos):
    inv = ROPE_THETA ** (-jnp.arange(0, ROT_DIM, 2, dtype=F32) / ROT_DIM)
    ang = inv[:, None] * pos.astype(F32)[None, :]
    return jnp.cos(ang), jnp.sin(ang)


def _rope_tables(pos):
    half = ROT_DIM // 2
    inv = ROPE_THETA ** (-jnp.arange(0, ROT_DIM, 2, dtype=F32) / ROT_DIM)
    ang = pos.astype(F32)[:, None] * inv[None, :]
    cos, sin = jnp.cos(ang), jnp.sin(ang)
    n = pos.shape[0]
    pad = jnp.zeros((n, HEAD_DIM - ROT_DIM), F32)
    c = jnp.concatenate([cos, cos, pad + 1.0], axis=1)
    s1 = jnp.concatenate([jnp.zeros((n, half), F32), sin, pad], axis=1)
    s2 = jnp.concatenate([-sin, jnp.zeros((n, half), F32), pad], axis=1)
    rep = LANES // HEAD_DIM
    return tuple(jnp.tile(a, (1, rep)) for a in (c, s1, s2))


def _diff_lambda(lam_ref, lambda_init):
    l = lam_ref[...]
    a = jnp.sum(l[0:1] * l[1:2], axis=-1, keepdims=True)
    b = jnp.sum(l[2:3] * l[3:4], axis=-1, keepdims=True)
    return jnp.exp(a) - jnp.exp(b) + lambda_init


def _attn_kernel(q_ref, kt_ref, v_ref, lam_ref, g_ref, o_ref, s_ref, m_ref, l_ref, acc_ref, *, tq, lambda_init):
    qi = pl.program_id(2)
    q = q_ref[0]
    lane = lax.broadcasted_iota(jnp.int32, q.shape, 1)
    zero = jnp.zeros_like(q)
    qc = (jnp.where(lane < HEAD_DIM, q, zero), jnp.where(lane >= HEAD_DIM, q, zero))
    nl = tq // LANES
    m_ref[...] = jnp.full_like(m_ref, NEG_INF)
    l_ref[...] = jnp.zeros_like(l_ref)
    acc_ref[...] = jnp.zeros_like(acc_ref)

    def scores(j, masked):
        kt = kt_ref[0, j]
        for c in range(2):
            s = jnp.dot(qc[c], kt, preferred_element_type=F32)
            if masked:
                r = lax.broadcasted_iota(jnp.int32, s.shape, 0)
                cc = lax.broadcasted_iota(jnp.int32, s.shape, 1)
                s = jnp.where(cc <= r, s, NEG_INF)
            s_ref[c, j] = s
            m = m_ref[c]
            for t in range(nl):
                m = jnp.maximum(m, s[:, t * LANES:(t + 1) * LANES])
            m_ref[c] = m

    def score_body(j, carry):
        scores(j, False)
        return carry

    lax.fori_loop(0, qi, score_body, 0)
    scores(qi, True)
    for c in range(2):
        m_ref[c] = jnp.broadcast_to(jnp.max(m_ref[c], axis=-1, keepdims=True), (tq, LANES))

    def accumulate(j, carry):
        v = v_ref[0, pl.ds(pl.multiple_of(j * tq, tq), tq), :]
        for c in range(2):
            m = m_ref[c]
            ps = [jnp.exp(s_ref[c, j, :, t * LANES:(t + 1) * LANES] - m) for t in range(nl)]
            l_ref[c] += functools.reduce(lambda a, b: a + b, ps)
            p = jnp.concatenate(ps, axis=1).astype(BF16)
            acc_ref[c] += jnp.dot(p, v, preferred_element_type=F32)
        return carry

    lax.fori_loop(0, qi + 1, accumulate, 0)
    l0 = jnp.sum(l_ref[0], axis=-1, keepdims=True)
    l1 = jnp.sum(l_ref[1], axis=-1, keepdims=True)
    lam = _diff_lambda(lam_ref, lambda_init)
    o = acc_ref[0] / l0 - lam * (acc_ref[1] / l1)
    o = o * lax.rsqrt(jnp.mean(o * o, axis=-1, keepdims=True) + EPS)
    o_ref[0] = (o * g_ref[...] * (1.0 - lambda_init)).astype(o_ref.dtype)


def _attn(q, kt4, v, lam4, subln_g, n_heads, lambda_init):
    b, s, w = q.shape
    nkt, tq = kt4.shape[1], kt4.shape[3]
    qmap = lambda bi, h, i: (bi, i, h)
    fixed = lambda bi, h, i: (0, 0)
    return pl.pallas_call(
        functools.partial(_attn_kernel, tq=tq, lambda_init=lambda_init),
        grid=(b, n_heads, s // tq),
        in_specs=[pl.BlockSpec((1, tq, LANES), qmap),
                  pl.BlockSpec((1, nkt, LANES, tq), lambda bi, h, i: (bi, 0, h, 0)),
                  pl.BlockSpec((1, s, LANES), lambda bi, h, i: (bi, 0, h)),
                  pl.BlockSpec((4, HEAD_DIM), fixed), pl.BlockSpec((1, V_DIM), fixed)],
        out_specs=pl.BlockSpec((1, tq, LANES), qmap),
        out_shape=jax.ShapeDtypeStruct((b, s, w), BF16),
        scratch_shapes=[pltpu.VMEM((2, nkt, tq, tq), F32), pltpu.VMEM((2, tq, LANES), F32),
                        pltpu.VMEM((2, tq, LANES), F32), pltpu.VMEM((2, tq, V_DIM), F32)],
        compiler_params=_cparams(("parallel", "parallel", "arbitrary")),
        name="attn",
    )(q, kt4, v, lam4, subln_g)


def _split2(x):
    hi = x.astype(BF16)
    return hi, (x - hi.astype(F32)).astype(BF16)


def _head_scores(k, qcol):
    prod = k * qcol
    return jnp.sum(prod.reshape(prod.shape[0] // HEAD_DIM, HEAD_DIM, prod.shape[1]), axis=1)


def _decode_kernel(pt_ref, q_ref, kn_ref, vn_ref, lam_ref, g_ref, *refs, pp, n_heads, lambda_init):
    del pt_ref
    k_refs, v_refs = refs[:pp], refs[pp:2 * pp]
    o_ref = refs[2 * pp]
    m_ref, l_ref, acc_ref, tmp_ref = refs[2 * pp + 1:]
    j = pl.program_id(1)
    nhc = 2 * n_heads

    @pl.when(j == 0)
    def _():
        m_ref[...] = jnp.full_like(m_ref, NEG_INF)
        l_ref[...] = jnp.zeros_like(l_ref)
        acc_ref[...] = jnp.zeros_like(acc_ref)

    qcol = q_ref[0]
    s = [_head_scores(kr[...], qcol) for kr in k_refs]
    m_old = m_ref[:, 0:1]
    smax = functools.reduce(jnp.maximum, s)
    m_new = jnp.maximum(m_old, jnp.max(smax, axis=-1, keepdims=True))
    alpha = jnp.exp(m_old - m_new)
    p = [jnp.exp(si - m_new) for si in s]
    psum = functools.reduce(lambda a, b: a + b, p)
    l_ref[...] = jnp.broadcast_to(alpha * l_ref[:, 0:1] + jnp.sum(psum, axis=-1, keepdims=True), l_ref.shape)
    m_ref[...] = jnp.broadcast_to(m_new, m_ref.shape)
    row_head = lax.broadcasted_iota(jnp.int32, (nhc, V_DIM), 0) // 2
    acc = alpha * acc_ref[...]
    page = k_refs[0].shape[1]
    for pi, vr in zip(p, v_refs):
        pb = pi.astype(BF16)
        for h in range(n_heads):
            vh = vr[pl.ds(h, page, stride=n_heads), :].astype(BF16)
            r = jnp.dot(pb, vh, preferred_element_type=F32)
            acc = acc + jnp.where(row_head == h, r, 0.0)
    acc_ref[...] = acc

    @pl.when(j == pl.num_programs(1) - 1)
    def _():
        s_new = _head_scores(kn_ref[0], qcol)[:, 0:1]
        m_old = m_ref[:, 0:1]
        m_f = jnp.maximum(m_old, s_new)
        a = jnp.exp(m_old - m_f)
        pn = jnp.exp(s_new - m_f)
        l_f = a * l_ref[:, 0:1] + pn
        tmp_ref[...] = (a * acc_ref[...] + pn * vn_ref[0]) / l_f
        o0 = tmp_ref[pl.ds(0, n_heads, stride=2), :]
        o1 = tmp_ref[pl.ds(1, n_heads, stride=2), :]
        o = o0 - _diff_lambda(lam_ref, lambda_init) * o1
        o = o * lax.rsqrt(jnp.mean(o * o, axis=-1, keepdims=True) + EPS)
        o_ref[0] = o * g_ref[...] * (1.0 - lambda_init)


def _decode_attn(page_table, qcol, kn_col, vn2, lam4, subln_g, cache_kt, cache_vr, pp, lambda_init):
    bd, n_pages = page_table.shape
    _, w, page = cache_kt.shape
    n_heads = cache_vr.shape[1] // page
    nhc = 2 * n_heads
    pt_flat = page_table.reshape(-1)
    fixed2 = lambda b, j, pt: (0, 0)
    perb = lambda b, j, pt: (b, 0, 0)

    def page_spec(i, rows, cols):
        return pl.BlockSpec((None, rows, cols), lambda b, j, pt: (pt[b * n_pages + j * pp + i], 0, 0))

    grid_spec = pltpu.PrefetchScalarGridSpec(
        num_scalar_prefetch=1,
        grid=(bd, n_pages // pp),
        in_specs=[pl.BlockSpec((1, w, LANES), perb), pl.BlockSpec((1, w, LANES), perb),
                  pl.BlockSpec((1, nhc, V_DIM), perb),
                  pl.BlockSpec((4, HEAD_DIM), fixed2), pl.BlockSpec((1, V_DIM), fixed2)]
                 + [page_spec(i, w, page) for i in range(pp)]
                 + [page_spec(i, page * n_heads, V_DIM) for i in range(pp)],
        out_specs=pl.BlockSpec((1, n_heads, V_DIM), perb),
        scratch_shapes=[pltpu.VMEM((nhc, LANES), F32), pltpu.VMEM((nhc, LANES), F32),
                        pltpu.VMEM((nhc, V_DIM), F32), pltpu.VMEM((nhc, V_DIM), F32)],
    )
    return pl.pallas_call(
        functools.partial(_decode_kernel, pp=pp, n_heads=n_heads, lambda_init=lambda_init),
        grid_spec=grid_spec,
        out_shape=jax.ShapeDtypeStruct((bd, n_heads, V_DIM), F32),
        compiler_params=_cparams(("parallel", "arbitrary")),
        name="decode_attn",
    )(pt_flat, qcol, kn_col, vn2, lam4, subln_g, *([cache_kt] * pp), *([cache_vr] * pp))


def _s5prep_kernel(arow_ref, acol_ref, ldt_ref, bt_ref, ct_ref, d_ref,
                   mt_ref, w_ref, vt_ref, lam_ref):
    L, hg, p = S5_CHUNK, S5_GROUP, S5_STATE
    dt = jnp.exp(ldt_ref[0])
    ar, ai = arow_ref[0, 0:1], arow_ref[0, 1:2]
    arc, aic = acol_ref[0, :, 0:1], acol_ref[0, :, 1:2]

    def powers(a_r, a_i, j):
        mag = jnp.exp(j * (a_r * dt))
        return mag * jnp.cos(j * (a_i * dt)), mag * jnp.sin(j * (a_i * dt))

    l1r, l1i = powers(ar, ai, 1.0)
    llr, lli = powers(ar, ai, float(L))
    lam_ref[0] = jnp.concatenate([llr, lli, l1r, l1i, jnp.zeros((4, p), F32)], axis=0)
    den = ar * ar + ai * ai
    cr = ((l1r - 1.0) * ar + l1i * ai) / den
    ci = (l1i * ar - (l1r - 1.0) * ai) / den
    btr, bti = bt_ref[0, 0], bt_ref[0, 1]
    bbr = btr * cr - bti * ci
    bbi = btr * ci + bti * cr
    jrow = lax.broadcasted_iota(jnp.int32, (L, p), 0).astype(F32)
    pr, pi = powers(ar, ai, jrow)
    for s in range(L):
        qr, qi = pr[L - 1 - s:L - s], pi[L - 1 - s:L - s]
        w_ref[0, 0, s * hg:(s + 1) * hg, :] = bbr * qr - bbi * qi
        w_ref[0, 1, s * hg:(s + 1) * hg, :] = bbr * qi + bbi * qr
    jl = (lax.broadcasted_iota(jnp.int32, (p, L * hg), 1) // hg).astype(F32)
    lpr, lpi = powers(arc, aic, jl)
    ctr, cti = ct_ref[0, 0], ct_ref[0, 1]
    cjr = ctr * lpr - cti * lpi
    cji = ctr * lpi + cti * lpr
    c1r, c1i = powers(arc, aic, 1.0)
    vt_ref[0, 0] = cjr * c1r - cji * c1i
    vt_ref[0, 1] = -(cjr * c1i + cji * c1r)
    kt = (jnp.dot(bbr, cjr, preferred_element_type=F32, precision=_HI)
          - jnp.dot(bbi, cji, preferred_element_type=F32, precision=_HI))
    n = L * hg
    ri = lax.broadcasted_iota(jnp.int32, (n, n), 0)
    cidx = lax.broadcasted_iota(jnp.int32, (n, n), 1)
    dtile = d_ref[0]
    for s in range(L):
        shift = (cidx - ri == s * hg).astype(F32)
        blk = jnp.dot(kt, shift, preferred_element_type=F32, precision=_HI)
        rr = lax.broadcasted_iota(jnp.int32, (hg, n), 0) + s * hg
        cc = lax.broadcasted_iota(jnp.int32, (hg, n), 1)
        mt_ref[0, s * hg:(s + 1) * hg, :] = blk + jnp.where(rr == cc, dtile, 0.0)


def _s5prep(arow, acol, ldt, bt, ct, dt_tiled):
    g = arow.shape[0]
    L, hg, p = S5_CHUNK, S5_GROUP, S5_STATE
    n = L * hg
    i3 = lambda i: (i, 0, 0)
    i4 = lambda i: (i, 0, 0, 0)
    return pl.pallas_call(
        _s5prep_kernel,
        grid=(g,),
        in_specs=[pl.BlockSpec((1, 2, p), i3), pl.BlockSpec((1, p, 2), i3), pl.BlockSpec((1, 1, 1), i3),
                  pl.BlockSpec((1, 2, hg, p), i4), pl.BlockSpec((1, 2, p, n), i4),
                  pl.BlockSpec((1, 1, n), i3)],
        out_specs=[pl.BlockSpec((1, n, n), i3), pl.BlockSpec((1, 2, n, p), i4),
                   pl.BlockSpec((1, 2, p, n), i4), pl.BlockSpec((1, SUBLANES, p), i3)],
        out_shape=[jax.ShapeDtypeStruct((g, n, n), F32), jax.ShapeDtypeStruct((g, 2, n, p), F32),
                   jax.ShapeDtypeStruct((g, 2, p, n), F32), jax.ShapeDtypeStruct((g, SUBLANES, p), F32)],
        compiler_params=_cparams(("parallel",)),
        name="s5prep",
    )(arow, acol, ldt, bt, ct, dt_tiled)


def _s5in_kernel(ut_ref, perm_ref, o_ref):
    g, nc, n = o_ref.shape[1:]
    hg, L = S5_GROUP, S5_CHUNK
    up = jnp.dot(ut_ref[0], perm_ref[...], preferred_element_type=F32)
    for gi in range(g):
        ugt = jnp.concatenate([up[gi * hg:(gi + 1) * hg, s * nc:(s + 1) * nc] for s in range(L)], axis=0)
        o_ref[0, gi] = ugt.T.astype(o_ref.dtype)


def _s5in(ut, perm):
    b, w, s = ut.shape
    g, nc, n = w // S5_GROUP, s // S5_CHUNK, S5_CHUNK * S5_GROUP
    return pl.pallas_call(
        _s5in_kernel,
        grid=(b,),
        in_specs=[pl.BlockSpec((1, w, s), lambda i: (i, 0, 0)),
                  pl.BlockSpec((s, s), lambda i: (0, 0), pipeline_mode=pl.Buffered(1))],
        out_specs=pl.BlockSpec((1, g, None, nc, n), lambda i: (i // SUBLANES, 0, i % SUBLANES, 0, 0)),
        out_shape=jax.ShapeDtypeStruct((b // SUBLANES, g, SUBLANES, nc, n), BF16),
        compiler_params=_cparams(("parallel",)),
        name="s5in",
    )(ut, perm)


def _s5out_kernel(y_ref, permt_ref, o_ref, ypt_ref):
    g, nc, n = y_ref.shape[1:]
    hg, L = S5_GROUP, S5_CHUNK
    for gi in range(g):
        ygt = y_ref[0, gi].astype(F32).T
        for t in range(L):
            ypt_ref[gi * hg:(gi + 1) * hg, t * nc:(t + 1) * nc] = ygt[t * hg:(t + 1) * hg, :].astype(BF16)
    yt = jnp.dot(ypt_ref[...], permt_ref[...], preferred_element_type=F32)
    o_ref[0] = yt.T.astype(o_ref.dtype)


def _s5out(y5, permt):
    nbh, g, _, nc, n = y5.shape
    b, s, w = nbh * SUBLANES, nc * S5_CHUNK, g * S5_GROUP
    return pl.pallas_call(
        _s5out_kernel,
        grid=(b,),
        in_specs=[pl.BlockSpec((1, g, None, nc, n), lambda i: (i // SUBLANES, 0, i % SUBLANES, 0, 0)),
                  pl.BlockSpec((s, s), lambda i: (0, 0), pipeline_mode=pl.Buffered(1))],
        out_specs=pl.BlockSpec((1, s, w), lambda i: (i, 0, 0)),
        out_shape=jax.ShapeDtypeStruct((b, s, w), BF16),
        scratch_shapes=[pltpu.VMEM((w, s), BF16)],
        compiler_params=_cparams(("parallel",)),
        name="s5out",
    )(y5, permt)


def _s5chunk_kernel(u_ref, mt_ref, w_ref, wsw_ref, vt_ref, lam_ref, y_ref, hl_ref, t1_ref, t2_ref, hs_ref):
    u = u_ref[0, 0]
    nc = u.shape[0] // SUBLANES
    p = S5_STATE
    t1_ref[...] = jnp.dot(u, w_ref[0].astype(BF16), preferred_element_type=F32)
    t2_ref[...] = jnp.dot(u, wsw_ref[0].astype(BF16), preferred_element_type=F32)
    lane = lax.broadcasted_iota(jnp.int32, (SUBLANES, 2 * p), 1)
    la = jnp.broadcast_to(lam_ref[0, 0:1], (SUBLANES, 2 * p))
    li2 = jnp.broadcast_to(lam_ref[0, 1:2], (SUBLANES, 2 * p))
    lb = jnp.where(lane < p, -li2, li2)

    def step(c, h):
        a, b = h
        s1 = t1_ref[pl.ds(c, SUBLANES, stride=nc), :]
        s2 = t2_ref[pl.ds(c, SUBLANES, stride=nc), :]
        hs_ref[pl.ds(pl.multiple_of(c * SUBLANES, SUBLANES), SUBLANES), :] = a
        return a * la + b * lb + s1, b * la - a * lb + s2

    z = jnp.zeros((SUBLANES, 2 * p), F32)
    a, _ = lax.fori_loop(0, nc, step, (z, z), unroll=8)
    hl_ref[0, 0] = a
    mt = mt_ref[0].astype(BF16)
    vt = vt_ref[0].astype(BF16)
    for bi in range(SUBLANES):
        rows = slice(bi * nc, (bi + 1) * nc)
        hb = hs_ref[pl.ds(bi, nc, stride=SUBLANES), :].astype(BF16)
        y = jnp.dot(u[rows], mt, preferred_element_type=F32) + jnp.dot(hb, vt, preferred_element_type=F32)
        y_ref[0, 0, rows, :] = y.astype(y_ref.dtype)


def _s5chunk(u4, mt, wcat, wsw, vtcat, lam2):
    nbh, g, rows, n = u4.shape
    p2 = 2 * S5_STATE
    um = lambda b, gi: (b, gi, 0, 0)
    g3 = lambda b, gi: (gi, 0, 0)
    return pl.pallas_call(
        _s5chunk_kernel,
        grid=(nbh, g),
        in_specs=[pl.BlockSpec((1, 1, rows, n), um), pl.BlockSpec((1, n, n), g3),
                  pl.BlockSpec((1, n, p2), g3), pl.BlockSpec((1, n, p2), g3), pl.BlockSpec((1, p2, n), g3),
                  pl.BlockSpec((1, SUBLANES, p2), g3)],
        out_specs=[pl.BlockSpec((1, 1, rows, n), um), pl.BlockSpec((1, 1, SUBLANES, p2), um)],
        out_shape=[jax.ShapeDtypeStruct(u4.shape, BF16),
                   jax.ShapeDtypeStruct((nbh, g, SUBLANES, p2), F32)],
        scratch_shapes=[pltpu.VMEM((rows, p2), F32), pltpu.VMEM((rows, p2), F32), pltpu.VMEM((rows, p2), F32)],
        compiler_params=_cparams(("parallel", "parallel")),
        name="s5chunk",
    )(u4, mt, wcat, wsw, vtcat, lam2)


def _s5step_kernel(u_ref, h0_ref, bb_ref, lam_ref, c_ref, d_ref, y_ref, h_ref):
    u = u_ref[0]
    h0r, h0i = h0_ref[0, 0], h0_ref[0, 1]
    l1r, l1i = lam_ref[0, 2:3], lam_ref[0, 3:4]
    bur = jnp.dot(u, bb_ref[0, 0], preferred_element_type=F32, precision=_HI)
    bui = jnp.dot(u, bb_ref[0, 1], preferred_element_type=F32, precision=_HI)
    hr = l1r * h0r - l1i * h0i + bur
    hi = l1r * h0i + l1i * h0r + bui
    h_ref[0, 0] = hr
    h_ref[0, 1] = hi
    y = _nt_dot(hr, c_ref[0, 0], precision=_HI) - _nt_dot(hi, c_ref[0, 1], precision=_HI)
    y_ref[0] = y + d_ref[0] * u


def _s5step(u3, h0, bb, lam, c2, d3):
    g, bd, hg = u3.shape
    p = S5_STATE
    i3 = lambda i: (i, 0, 0)
    i4 = lambda i: (i, 0, 0, 0)
    return pl.pallas_call(
        _s5step_kernel,
        grid=(g,),
        in_specs=[pl.BlockSpec((1, bd, hg), i3), pl.BlockSpec((1, 2, bd, p), i4),
                  pl.BlockSpec((1, 2, hg, p), i4), pl.BlockSpec((1, SUBLANES, p), i3),
                  pl.BlockSpec((1, 2, hg, p), i4), pl.BlockSpec((1, 1, hg), i3)],
        out_specs=[pl.BlockSpec((1, bd, hg), i3), pl.BlockSpec((1, 2, bd, p), i4)],
        out_shape=[jax.ShapeDtypeStruct((g, bd, hg), F32), jax.ShapeDtypeStruct((g, 2, bd, p), F32)],
        compiler_params=_cparams(("parallel",)),
        name="s5step",
    )(u3, h0, bb, lam, c2, d3)


def _postmix_kernel(x_ref, a_ref, y_ref, wglu_ref, bglu_ref, wout_ref, g2_ref, wr_ref, br_ref,
                    x1_ref, h2_ref, comb_ref):
    wa = a_ref.shape[-1]
    y = jax.nn.gelu(y_ref[...].astype(F32))
    z = jnp.dot(y.astype(BF16), wglu_ref[...], preferred_element_type=F32) + bglu_ref[...]
    s5o = y * jax.nn.sigmoid(z)
    mix = jnp.dot(a_ref[...].astype(BF16), wout_ref[0:wa, :], preferred_element_type=F32)
    mix += jnp.dot(s5o.astype(BF16), wout_ref[wa:, :], preferred_element_type=F32)
    x1 = x_ref[...] + mix
    x1_ref[...] = x1
    h2 = x1 * lax.rsqrt(jnp.mean(x1 * x1, axis=-1, keepdims=True) + EPS) * g2_ref[...]
    h2_ref[...] = h2.astype(h2_ref.dtype)
    logits = jnp.dot(h2, wr_ref[...], preferred_element_type=F32, precision=_HI) + br_ref[...]
    ne = N_GROUPS_MOE * EXP_PER_GROUP
    lane = lax.broadcasted_iota(jnp.int32, logits.shape, 1).astype(F32)
    big = jnp.float32(1 << 20)
    gmask = (lane >= ne) & (lane < ne + N_GROUPS_MOE)
    gl = jnp.where(gmask, logits, NEG_INF)
    gmax = jnp.max(gl, axis=-1, keepdims=True)
    gidx = jnp.min(jnp.where(gl == gmax, lane, big), axis=-1, keepdims=True) - ne
    g_w = 1.0 / jnp.sum(jnp.where(gmask, jnp.exp(logits - gmax), 0.0), axis=-1, keepdims=True)
    lo = gidx * EXP_PER_GROUP
    el = jnp.where((lane >= lo) & (lane < lo + EXP_PER_GROUP), logits, NEG_INF)
    e1 = jnp.max(el, axis=-1, keepdims=True)
    i1 = jnp.min(jnp.where(el == e1, lane, big), axis=-1, keepdims=True)
    el2 = jnp.where(lane == i1, NEG_INF, el)
    e2 = jnp.max(el2, axis=-1, keepdims=True)
    i2 = jnp.min(jnp.where(el2 == e2, lane, big), axis=-1, keepdims=True)
    r = jnp.exp(e2 - e1)
    w1 = g_w / (1.0 + r)
    w2 = g_w * r / (1.0 + r)
    comb_ref[...] = (jnp.where(lane == i1, w1, 0.0) + jnp.where(lane == i2, w2, 0.0)
                     + jnp.where(lane == GID_LANE, gidx, 0.0))


def _postmix(x2, attn, ys5, wglu_bf, bglu, wout_bf, ln2_g, w_router, b_router, tm):
    t, d = x2.shape
    wa = attn.shape[1]
    ws = ys5.shape[1]
    row = lambda i: (i, 0)
    fixed = lambda i: (0, 0)
    return pl.pallas_call(
        _postmix_kernel,
        grid=(t // tm,),
        in_specs=[pl.BlockSpec((tm, d), row), pl.BlockSpec((tm, wa), row), pl.BlockSpec((tm, ws), row),
                  pl.BlockSpec((ws, ws), fixed), pl.BlockSpec((1, ws), fixed),
                  pl.BlockSpec((wa + ws, d), fixed), pl.BlockSpec((1, d), fixed),
                  pl.BlockSpec((d, LANES), fixed), pl.BlockSpec((1, LANES), fixed)],
        out_specs=[pl.BlockSpec((tm, d), row), pl.BlockSpec((tm, d), row), pl.BlockSpec((tm, LANES), row)],
        out_shape=[jax.ShapeDtypeStruct((t, d), F32), jax.ShapeDtypeStruct((t, d), BF16),
                   jax.ShapeDtypeStruct((t, LANES), F32)],
        compiler_params=_cparams(("parallel",)),
        name="postmix",
    )(x2, attn, ys5, wglu_bf, bglu, wout_bf, ln2_g, w_router, b_router)


def _moe_kernel(h_ref, x1_ref, comb_ref, wg_ref, wu_ref, wd_ref, gf_ref, o_ref, acc_ref):
    e = pl.program_id(1)

    @pl.when(e == 0)
    def _():
        acc_ref[...] = jnp.zeros_like(acc_ref)

    h = h_ref[...]
    he = jax.nn.silu(jnp.dot(h, wg_ref[0], preferred_element_type=F32))
    he = he * jnp.dot(h, wu_ref[0], preferred_element_type=F32)
    comb = comb_ref[...]
    lane = lax.broadcasted_iota(jnp.int32, comb.shape, 1)
    ce = jnp.sum(jnp.where(lane == e, comb, 0.0), axis=-1, keepdims=True)
    acc_ref[...] += jnp.dot((he * ce).astype(BF16), wd_ref[0], preferred_element_type=F32)

    @pl.when(e == pl.num_programs(1) - 1)
    def _():
        x2 = x1_ref[...] + acc_ref[...]
        o_ref[...] = x2 * lax.rsqrt(jnp.mean(x2 * x2, axis=-1, keepdims=True) + EPS) * gf_ref[...]


def _moe(h2, x1, comb, wg_bf, wu_bf, wd_bf, lnf_g, tm):
    t, d = x1.shape
    ne, _, de = wg_bf.shape
    row = lambda i, e: (i, 0)
    return pl.pallas_call(
        _moe_kernel,
        grid=(t // tm, ne),
        in_specs=[pl.BlockSpec((tm, d), row), pl.BlockSpec((tm, d), row), pl.BlockSpec((tm, LANES), row),
                  pl.BlockSpec((1, d, de), lambda i, e: (e, 0, 0)),
                  pl.BlockSpec((1, d, de), lambda i, e: (e, 0, 0)),
                  pl.BlockSpec((1, de, d), lambda i, e: (e, 0, 0)),
                  pl.BlockSpec((1, d), lambda i, e: (0, 0))],
        out_specs=pl.BlockSpec((tm, d), row),
        out_shape=jax.ShapeDtypeStruct((t, d), F32),
        scratch_shapes=[pltpu.VMEM((tm, d), F32)],
        compiler_params=_cparams(("parallel", "arbitrary")),
        name="moe",
    )(h2, x1, comb, wg_bf, wu_bf, wd_bf, lnf_g)


def _moe_grouped_kernel(h_ref, x1_ref, comb_ref, ut_ref, lt_ref, wg_ref, wu_ref, wd_ref, gf_ref, o_ref):
    g = pl.program_id(1)
    tm, d = h_ref.shape
    gf32 = g.astype(F32)
    comb = comb_ref[...]
    combt = comb.T
    memc = comb[:, GID_LANE:GID_LANE + 1] == gf32
    memr = combt[GID_LANE:GID_LANE + 1, :] == gf32
    memr8 = jnp.broadcast_to(jnp.where(memr, 1.0, 0.0), (SUBLANES, tm)).astype(BF16)
    memcb = jnp.broadcast_to(jnp.where(memc, 1.0, 0.0), (tm, LANES)).astype(BF16)
    rank_row = jnp.dot(memr8, ut_ref[...], preferred_element_type=F32)[0:1]
    rank_col = jnp.dot(lt_ref[...], memcb, preferred_element_type=F32)[:, 0:1]
    posr = jnp.where(memr, rank_row - 1.0, -1.0)
    posc = jnp.where(memc, rank_col - 1.0, -1.0)
    n_rows = jnp.max(rank_row).astype(jnp.int32)
    big = 2 * MOE_CHUNK
    n_big = n_rows // big
    n_small = (n_rows - n_big * big + (MOE_CHUNK - 1)) // MOE_CHUNK
    chi, clo = _split2(combt)

    @pl.when(g == 0)
    def _():
        o_ref[...] = jnp.zeros_like(o_ref)

    def chunk(row0, ch):
        base = row0.astype(F32)
        rid = lax.broadcasted_iota(jnp.int32, (ch, tm), 0).astype(F32) + base
        cid = lax.broadcasted_iota(jnp.int32, (tm, ch), 1).astype(F32) + base
        sel = jnp.where(posr == rid, 1.0, 0.0).astype(BF16)
        selt = jnp.where(posc == cid, 1.0, 0.0).astype(BF16)
        xg = jnp.dot(sel, h_ref[...], preferred_element_type=F32).astype(BF16)
        cg = _nt_dot(sel, chi) + _nt_dot(sel, clo)
        lane = lax.broadcasted_iota(jnp.int32, cg.shape, 1).astype(F32)
        z = jnp.zeros((ch, d), F32)
        for e in range(EXP_PER_GROUP):
            he = jax.nn.silu(jnp.dot(xg, wg_ref[e], preferred_element_type=F32))
            he = he * jnp.dot(xg, wu_ref[e], preferred_element_type=F32)
            ce = jnp.sum(jnp.where(lane == gf32 * EXP_PER_GROUP + e, cg, 0.0), axis=-1, keepdims=True)
            z = z + jnp.dot((he * ce).astype(BF16), wd_ref[e], preferred_element_type=F32)
        o_ref[...] += jnp.dot(selt, z.astype(BF16), preferred_element_type=F32)

    def big_body(k, carry):
        chunk(k * big, big)
        return carry

    def small_body(k, carry):
        chunk(n_big * big + k * MOE_CHUNK, MOE_CHUNK)
        return carry

    lax.fori_loop(0, n_big, big_body, 0)
    lax.fori_loop(0, n_small, small_body, 0)

    @pl.when(g == pl.num_programs(1) - 1)
    def _():
        x2 = x1_ref[...] + o_ref[...]
        o_ref[...] = x2 * lax.rsqrt(jnp.mean(x2 * x2, axis=-1, keepdims=True) + EPS) * gf_ref[...]


def _moe_grouped(h2, x1, comb, wg_bf, wu_bf, wd_bf, lnf_g, tm):
    t, d = x1.shape
    ne, _, de = wg_bf.shape
    ng = ne // EXP_PER_GROUP
    tri = jnp.arange(tm)[:, None] <= jnp.arange(tm)[None, :]
    ut = tri.astype(BF16)
    lt = tri.T.astype(BF16)
    once = pl.Buffered(1)
    tile = lambda i, g: (i, 0)
    fixed = lambda i, g: (0, 0)
    wmap = lambda i, g: (g, 0, 0)
    return pl.pallas_call(
        _moe_grouped_kernel,
        grid=(t // tm, ng),
        in_specs=[pl.BlockSpec((tm, d), tile), pl.BlockSpec((tm, d), tile),
                  pl.BlockSpec((tm, LANES), tile),
                  pl.BlockSpec((tm, tm), fixed, pipeline_mode=once),
                  pl.BlockSpec((tm, tm), fixed, pipeline_mode=once),
                  pl.BlockSpec((EXP_PER_GROUP, d, de), wmap),
                  pl.BlockSpec((EXP_PER_GROUP, d, de), wmap),
                  pl.BlockSpec((EXP_PER_GROUP, de, d), wmap),
                  pl.BlockSpec((1, d), fixed)],
        out_specs=pl.BlockSpec((tm, d), tile),
        out_shape=jax.ShapeDtypeStruct((t, d), F32),
        compiler_params=_cparams(("parallel", "arbitrary")),
        name="moe_grouped",
    )(h2, x1, comb, ut, lt, wg_bf, wu_bf, wd_bf, lnf_g)


def _tile(n, want):
    t = min(n, want)
    while n % t:
        t //= 2
    return t


def kernel(x_prompt, x_sample, cache_k, cache_v, state_s5_re, state_s5_im, page_table, ln1_g, w_in, lambda_q1, lambda_k1, lambda_q2, lambda_k2, subln_g, s5_a_re, s5_a_im, s5_log_dt, s5_b_re, s5_b_im, s5_c_re, s5_c_im, s5_d, w_glu, b_glu, w_out, ln2_g, w_router_group, b_router_group, w_router_expert, b_router_expert, w_gate, w_up, w_down, ln_f_g):
    depth = ln1_g.shape[0]
    assert depth == 1, "single-layer step"
    b, s, d = x_prompt.shape
    bd, ds, _ = x_sample.shape
    assert ds == 1
    n_pages = page_table.shape[1]
    page = cache_k.shape[2]
    past_len = n_pages * page
    wq = w_in.shape[2] // 4
    n_heads = wq // V_DIM
    g = s5_a_re.shape[1]
    p = S5_STATE
    hg = S5_GROUP
    L = S5_CHUNK
    n = L * hg
    assert b % SUBLANES == 0 and s % L == 0 and g * hg == wq
    lambda_init = 0.8 - 0.6 * math.exp(-0.3 * 0)

    w_in_bf = w_in[0].astype(BF16)
    wglu_bf = w_glu[0].astype(BF16)
    wout_bf = w_out[0].astype(BF16)
    wg_bf, wu_bf, wd_bf = (w[0].astype(BF16) for w in (w_gate, w_up, w_down))
    ne = w_gate.shape[1]
    w_router = jnp.concatenate(
        [jnp.transpose(w_router_expert[0], (1, 0, 2)).reshape(d, ne), w_router_group[0],
         jnp.zeros((d, LANES - ne - N_GROUPS_MOE), F32)], axis=1)
    b_router = jnp.concatenate([b_router_expert[0].reshape(ne), b_router_group[0],
                                jnp.zeros((LANES - ne - N_GROUPS_MOE,), F32)])[None, :]
    lam4 = jnp.concatenate([lambda_q1, lambda_k1, lambda_q2, lambda_k2], axis=0)
    ln1 = ln1_g[0][None, :]
    ln2 = ln2_g[0][None, :]
    lnf = ln_f_g[None, :]
    subg = subln_g[0][None, :]
    bglu = b_glu[0][None, :]

    arow = jnp.stack([s5_a_re[0], s5_a_im[0]], axis=1)
    acol = jnp.stack([s5_a_re[0], s5_a_im[0]], axis=2)
    ldt = s5_log_dt[0].reshape(g, 1, 1)
    bt = jnp.stack([jnp.swapaxes(s5_b_re[0], 1, 2), jnp.swapaxes(s5_b_im[0], 1, 2)], axis=1)
    ct = jnp.stack([jnp.tile(jnp.swapaxes(s5_c_re[0], 1, 2), (1, 1, L)),
                    jnp.tile(jnp.swapaxes(s5_c_im[0], 1, 2), (1, 1, L))], axis=1)
    d_g = s5_d[0].reshape(g, 1, hg)
    mt, w_s5, vt, lam_s5 = _s5prep(arow, acol, ldt, bt, ct, jnp.tile(d_g, (1, 1, L)))

    tm = _tile(b * s, 512)
    x2 = x_prompt.reshape(b * s, d)
    pos_p = jnp.arange(s, dtype=jnp.int32)
    rc, rs1, rs2 = _rope_tables(pos_p)
    rct, rst = _rope_tables_t(pos_p)
    w_qv_bf = jnp.concatenate([w_in_bf[:, :wq], w_in_bf[:, 2 * wq:3 * wq]], axis=1)
    wkut_bf = jnp.concatenate([w_in_bf[:, wq:2 * wq], w_in_bf[:, 3 * wq:]], axis=1).T
    tq = _tile(s, 512)
    q, vf, vb, ut, ktf, ktb = _inproj_prompt(x_prompt, ln1, w_qv_bf, wkut_bf, rc, rs1, rs2, rct, rst,
                                             _tile(s, tm), tq)
    attn = _attn(q, ktb, vb, lam4, subg, n_heads, lambda_init)
    kf = jnp.transpose(ktf.reshape(b, 2 * n_heads, HEAD_DIM, s), (0, 3, 1, 2))
    nbh, nc = b // SUBLANES, s // L
    tok = jnp.arange(s)
    perm = (((tok % L) * nc + tok // L)[:, None] == tok[None, :]).astype(BF16)
    u5 = _s5in(ut, perm)
    wcat = jnp.concatenate([w_s5[:, 0], w_s5[:, 1]], axis=-1)
    wsw = jnp.concatenate([w_s5[:, 1], w_s5[:, 0]], axis=-1)
    vtcat = jnp.concatenate([vt[:, 0], vt[:, 1]], axis=1)
    lam2 = jnp.concatenate([lam_s5, lam_s5], axis=-1)
    y5, hl = _s5chunk(u5.reshape(nbh, g, SUBLANES * nc, n), mt, wcat, wsw, vtcat, lam2)
    ys5 = _s5out(y5.reshape(nbh, g, SUBLANES, nc, n), perm.T).reshape(b * s, wq)
    hl = hl.transpose(0, 2, 1, 3).reshape(b, g, 2, p).transpose(2, 0, 1, 3)
    x1, h2, comb = _postmix(x2, attn.reshape(b * s, wq), ys5, wglu_bf, bglu, wout_bf, ln2,
                            w_router, b_router, tm)
    y_prompt = _moe_grouped(h2, x1, comb, wg_bf, wu_bf, wd_bf, lnf, _tile(b * s, 1024)).reshape(b, s, d)

    xs2 = x_sample.reshape(bd, d)
    pos_s = jnp.full((bd,), past_len, jnp.int32)
    sc, ss1, ss2 = _rope_tables(pos_s)
    qs, kfs, vfs, _, _, us = _inproj(xs2, ln1, w_in_bf, sc, ss1, ss2, bd, F32, F32)
    cache_kt = jnp.transpose(cache_k[0], (0, 2, 3, 1)).reshape(-1, wq, page)
    cache_vr = cache_v[0].reshape(-1, page * n_heads, V_DIM)
    qcol = jnp.broadcast_to(qs[:, :, None], (bd, wq, LANES))
    kn_col = jnp.broadcast_to(kfs[:, :, None], (bd, wq, LANES))
    vn2 = jnp.repeat(vfs.reshape(bd, n_heads, V_DIM), 2, axis=1)
    pp = _tile(n_pages, 8)
    attn_s = _decode_attn(page_table, qcol, kn_col, vn2, lam4, subg, cache_kt, cache_vr, pp, lambda_init)
    u3 = us.reshape(bd, g, hg).transpose(1, 0, 2)
    h0 = jnp.stack([state_s5_re[0], state_s5_im[0]]).transpose(2, 0, 1, 3)
    bb = w_s5[:, :, (L - 1) * hg:, :]
    c2 = jnp.stack([s5_c_re[0], s5_c_im[0]], axis=1)
    ys3, hs = _s5step(u3, h0, bb, lam_s5, c2, d_g)
    ys_s5 = ys3.transpose(1, 0, 2).reshape(bd, wq)
    x1s, h2s, combs = _postmix(xs2, attn_s.reshape(bd, wq), ys_s5, wglu_bf, bglu, wout_bf, ln2,
                               w_router, b_router, bd)
    y_sample = _moe(h2s, x1s, combs, wg_bf, wu_bf, wd_bf, lnf, bd).reshape(bd, 1, d)
    hs = hs.transpose(1, 2, 0, 3)

    return (y_prompt, y_sample,
            kf.reshape(1, b, s, 2 * n_heads, HEAD_DIM), vf.reshape(1, b, s, n_heads, V_DIM),
            hl[0][None], hl[1][None],
            kfs.reshape(1, bd, 1, 2 * n_heads, HEAD_DIM), vfs.reshape(1, bd, 1, n_heads, V_DIM),
            hs[0][None], hs[1][None])
```

```python
import functools
import math

import jax
import jax.numpy as jnp
from jax import lax
from jax.experimental import pallas as pl
from jax.experimental.pallas import tpu as pltpu

F32 = jnp.float32
BF16 = jnp.bfloat16

HEAD_DIM = 64
V_DIM = 128
ROT_DIM = 16
ROPE_THETA = 500000.0
S5_GROUP = 16
S5_STATE = 64
S5_CHUNK = 16
SUBLANES = 8
N_GROUPS_MOE = 4
EXP_PER_GROUP = 8
EPS = 1e-5
NEG_INF = -1e30
LANES = 128
GID_LANE = LANES - 1
MOE_CHUNK = 128
VMEM_LIMIT = 56 * 1024 * 1024

_HI = lax.Precision.HIGHEST


def _cparams(sem):
    return pltpu.CompilerParams(dimension_semantics=sem, vmem_limit_bytes=VMEM_LIMIT)


def _nt_dot(a, b, **kw):
    return lax.dot_general(a, b, (((1,), (1,)), ((), ())), preferred_element_type=F32, **kw)


def _inproj_kernel(x_ref, g_ref, w_ref, c_ref, s1_ref, s2_ref,
                   q_ref, kf_ref, vf_ref, kb_ref, vb_ref, u_ref):
    x = x_ref[...]
    ms = jnp.mean(x * x, axis=-1, keepdims=True)
    hn = x * lax.rsqrt(ms + EPS) * g_ref[...]
    proj = jnp.dot(hn, w_ref[...], preferred_element_type=F32, precision=_HI)
    w = q_ref.shape[-1]
    c = c_ref[...]
    s1 = s1_ref[...]
    s2 = s2_ref[...]
    for j in range(w // LANES):
        sl = slice(j * LANES, (j + 1) * LANES)
        zq = proj[:, j * LANES:(j + 1) * LANES]
        zk = proj[:, w + j * LANES:w + (j + 1) * LANES]
        rq = zq * c + pltpu.roll(zq, ROT_DIM // 2, 1) * s1 + pltpu.roll(zq, LANES - ROT_DIM // 2, 1) * s2
        rk = zk * c + pltpu.roll(zk, ROT_DIM // 2, 1) * s1 + pltpu.roll(zk, LANES - ROT_DIM // 2, 1) * s2
        q_ref[:, sl] = (rq * (HEAD_DIM ** -0.5)).astype(q_ref.dtype)
        kf_ref[:, sl] = rk
        kb_ref[:, sl] = rk.astype(BF16)
    v = proj[:, 2 * w:3 * w]
    vf_ref[...] = v
    vb_ref[...] = v.astype(BF16)
    u_ref[...] = proj[:, 3 * w:4 * w].astype(u_ref.dtype)


def _inproj(x2, ln_g, w_bf, rc, rs1, rs2, tm, q_dtype, u_dtype):
    t, d = x2.shape
    w = w_bf.shape[1] // 4
    nt = t // tm
    npos = rc.shape[0] // tm
    row = lambda i: (i, 0)
    pos = lambda i: (i % npos, 0)
    fixed = lambda i: (0, 0)
    outs = [jax.ShapeDtypeStruct((t, w), q_dtype), jax.ShapeDtypeStruct((t, w), F32),
            jax.ShapeDtypeStruct((t, w), F32), jax.ShapeDtypeStruct((t, w), BF16),
            jax.ShapeDtypeStruct((t, w), BF16), jax.ShapeDtypeStruct((t, w), u_dtype)]
    return pl.pallas_call(
        _inproj_kernel,
        grid=(nt,),
        in_specs=[pl.BlockSpec((tm, d), row), pl.BlockSpec((1, d), fixed),
                  pl.BlockSpec((d, 4 * w), fixed),
                  pl.BlockSpec((tm, LANES), pos), pl.BlockSpec((tm, LANES), pos),
                  pl.BlockSpec((tm, LANES), pos)],
        out_specs=[pl.BlockSpec((tm, w), row)] * 6,
        out_shape=outs,
        compiler_params=_cparams(("parallel",)),
        name="inproj",
    )(x2, ln_g, w_bf, rc, rs1, rs2)


def _inproj_prompt_kernel(x_ref, g_ref, w_ref, wkt_ref, c_ref, s1_ref, s2_ref, ct_ref, st_ref,
                          q_ref, vf_ref, vb_ref, ut_ref, ktf_ref, ktb_ref):
    x = x_ref[0]
    ms = jnp.mean(x * x, axis=-1, keepdims=True)
    hn = (x * lax.rsqrt(ms + EPS) * g_ref[...]).astype(BF16)
    proj = jnp.dot(hn, w_ref[...], preferred_element_type=F32)
    w = q_ref.shape[-1]
    ktu = _nt_dot(wkt_ref[...], hn)
    kt = ktu[:w]
    ut_ref[0] = ktu[w:].astype(ut_ref.dtype)
    c, s1, s2 = c_ref[...], s1_ref[...], s2_ref[...]
    for j in range(w // LANES):
        sl = slice(j * LANES, (j + 1) * LANES)
        zq = proj[:, sl]
        rq = zq * c + pltpu.roll(zq, ROT_DIM // 2, 1) * s1 + pltpu.roll(zq, LANES - ROT_DIM // 2, 1) * s2
        q_ref[0, :, sl] = (rq * (HEAD_DIM ** -0.5)).astype(q_ref.dtype)
    v = proj[:, w:2 * w]
    vf_ref[0] = v
    vb_ref[0] = v.astype(BF16)
    ct, st = ct_ref[...], st_ref[...]
    half = ROT_DIM // 2
    tk = ktb_ref.shape[-1]
    for hc in range(w // HEAD_DIM):
        base = hc * HEAD_DIM
        x1, x2 = kt[base:base + half], kt[base + half:base + ROT_DIM]
        blk = jnp.concatenate([x1 * ct - x2 * st, x2 * ct + x1 * st, kt[base + ROT_DIM:base + HEAD_DIM]], axis=0)
        ktf_ref[0, base:base + HEAD_DIM, :] = blk
        for t in range(blk.shape[1] // tk):
            ktb_ref[0, t, base:base + HEAD_DIM, :] = blk[:, t * tk:(t + 1) * tk].astype(BF16)


def _inproj_prompt(x3, ln_g, w_qv_bf, wkut_bf, rc, rs1, rs2, rct, rst, tm, tk):
    b, s, d = x3.shape
    w = wkut_bf.shape[0] // 2
    half = ROT_DIM // 2
    tok = lambda bi, i: (bi, i, 0)
    pos = lambda bi, i: (i, 0)
    fixed = lambda bi, i: (0, 0)
    outs = [jax.ShapeDtypeStruct((b, s, w), BF16), jax.ShapeDtypeStruct((b, s, w), F32),
            jax.ShapeDtypeStruct((b, s, w), BF16), jax.ShapeDtypeStruct((b, w, s), BF16),
            jax.ShapeDtypeStruct((b, w, s), F32), jax.ShapeDtypeStruct((b, s // tk, w, tk), BF16)]
    return pl.pallas_call(
        _inproj_prompt_kernel,
        grid=(b, s // tm),
        in_specs=[pl.BlockSpec((1, tm, d), tok), pl.BlockSpec((1, d), fixed),
                  pl.BlockSpec((d, 2 * w), fixed), pl.BlockSpec((2 * w, d), fixed),
                  pl.BlockSpec((tm, LANES), pos), pl.BlockSpec((tm, LANES), pos), pl.BlockSpec((tm, LANES), pos),
                  pl.BlockSpec((half, tm), lambda bi, i: (0, i)), pl.BlockSpec((half, tm), lambda bi, i: (0, i))],
        out_specs=[pl.BlockSpec((1, tm, w), tok)] * 3
                  + [pl.BlockSpec((1, w, tm), lambda bi, i: (bi, 0, i))] * 2
                  + [
                     pl.BlockSpec((1, tm // tk, w, tk), lambda bi, i: (bi, i, 0, 0))],
        out_shape=outs,
        compiler_params=_cparams(("parallel", "parallel")),
        name="inproj_prompt",
    )(x3, ln_g, w_qv_bf, wkut_bf, rc, rs1, rs2, rct, rst)


def _rope_tables_t(pos):
    inv = ROPE_THETA ** (-jnp.arange(0, ROT_DIM, 2, dtype=F32) / ROT_DIM)
    ang = inv[:, None] * pos.astype(F32)[None, :]
    return jnp.cos(ang), jnp.sin(ang)


def _rope_tables(pos):
    half = ROT_DIM // 2
    inv = ROPE_THETA ** (-jnp.arange(0, ROT_DIM, 2, dtype=F32) / ROT_DIM)
    ang = pos.astype(F32)[:, None] * inv[None, :]
    cos, sin = jnp.cos(ang), jnp.sin(ang)
    n = pos.shape[0]
    pad = jnp.zeros((n, HEAD_DIM - ROT_DIM), F32)
    c = jnp.concatenate([cos, cos, pad + 1.0], axis=1)
    s1 = jnp.concatenate([jnp.zeros((n, half), F32), sin, pad], axis=1)
    s2 = jnp.concatenate([-sin, jnp.zeros((n, half), F32), pad], axis=1)
    rep = LANES // HEAD_DIM
    return tuple(jnp.tile(a, (1, rep)) for a in (c, s1, s2))


def _diff_lambda(lam_ref, lambda_init):
    l = lam_ref[...]
    a = jnp.sum(l[0:1] * l[1:2], axis=-1, keepdims=True)
    b = jnp.sum(l[2:3] * l[3:4], axis=-1, keepdims=True)
    return jnp.exp(a) - jnp.exp(b) + lambda_init


def _attn_kernel(q_ref, kt_ref, v_ref, lam_ref, g_ref, o_ref, s_ref, m_ref, l_ref, acc_ref, *, tq, lambda_init):
    qi = pl.program_id(2)
    q = q_ref[0]
    lane = lax.broadcasted_iota(jnp.int32, q.shape, 1)
    zero = jnp.zeros_like(q)
    qc = (jnp.where(lane < HEAD_DIM, q, zero), jnp.where(lane >= HEAD_DIM, q, zero))
    nl = tq // LANES
    m_ref[...] = jnp.full_like(m_ref, NEG_INF)
    l_ref[...] = jnp.zeros_like(l_ref)
    acc_ref[...] = jnp.zeros_like(acc_ref)

    def scores(j, masked):
        kt = kt_ref[0, j]
        for c in range(2):
            s = jnp.dot(qc[c], kt, preferred_element_type=F32)
            if masked:
                r = lax.broadcasted_iota(jnp.int32, s.shape, 0)
                cc = lax.broadcasted_iota(jnp.int32, s.shape, 1)
                s = jnp.where(cc <= r, s, NEG_INF)
            s_ref[c, j] = s
            m = m_ref[c]
            for t in range(nl):
                m = jnp.maximum(m, s[:, t * LANES:(t + 1) * LANES])
            m_ref[c] = m

    def score_body(j, carry):
        scores(j, False)
        return carry

    lax.fori_loop(0, qi, score_body, 0)
    scores(qi, True)
    for c in range(2):
        m_ref[c] = jnp.broadcast_to(jnp.max(m_ref[c], axis=-1, keepdims=True), (tq, LANES))

    def accumulate(j, carry):
        v = v_ref[0, pl.ds(pl.multiple_of(j * tq, tq), tq), :]
        for c in range(2):
            m = m_ref[c]
            ps = [jnp.exp(s_ref[c, j, :, t * LANES:(t + 1) * LANES] - m) for t in range(nl)]
            l_ref[c] += functools.reduce(lambda a, b: a + b, ps)
            p = jnp.concatenate(ps, axis=1).astype(BF16)
            acc_ref[c] += jnp.dot(p, v, preferred_element_type=F32)
        return carry

    lax.fori_loop(0, qi + 1, accumulate, 0)
    l0 = jnp.sum(l_ref[0], axis=-1, keepdims=True)
    l1 = jnp.sum(l_ref[1], axis=-1, keepdims=True)
    lam = _diff_lambda(lam_ref, lambda_init)
    o = acc_ref[0] / l0 - lam * (acc_ref[1] / l1)
    o = o * lax.rsqrt(jnp.mean(o * o, axis=-1, keepdims=True) + EPS)
    o_ref[0] = (o * g_ref[...] * (1.0 - lambda_init)).astype(o_ref.dtype)


def _attn(q, kt4, v, lam4, subln_g, n_heads, lambda_init):
    b, s, w = q.shape
    nkt, tq = kt4.shape[1], kt4.shape[3]
    qmap = lambda bi, h, i: (bi, i, h)
    fixed = lambda bi, h, i: (0, 0)
    return pl.pallas_call(
        functools.partial(_attn_kernel, tq=tq, lambda_init=lambda_init),
        grid=(b, n_heads, s // tq),
        in_specs=[pl.BlockSpec((1, tq, LANES), qmap),
                  pl.BlockSpec((1, nkt, LANES, tq), lambda bi, h, i: (bi, 0, h, 0)),
                  pl.BlockSpec((1, s, LANES), lambda bi, h, i: (bi, 0, h)),
                  pl.BlockSpec((4, HEAD_DIM), fixed), pl.BlockSpec((1, V_DIM), fixed)],
        out_specs=pl.BlockSpec((1, tq, LANES), qmap),
        out_shape=jax.ShapeDtypeStruct((b, s, w), BF16),
        scratch_shapes=[pltpu.VMEM((2, nkt, tq, tq), F32), pltpu.VMEM((2, tq, LANES), F32),
                        pltpu.VMEM((2, tq, LANES), F32), pltpu.VMEM((2, tq, V_DIM), F32)],
        compiler_params=_cparams(("parallel", "parallel", "arbitrary")),
        name="attn",
    )(q, kt4, v, lam4, subln_g)


def _split2(x):
    hi = x.astype(BF16)
    return hi, (x - hi.astype(F32)).astype(BF16)


def _head_scores(k, qcol):
    prod = k * qcol
    return jnp.sum(prod.reshape(prod.shape[0] // HEAD_DIM, HEAD_DIM, prod.shape[1]), axis=1)


def _decode_kernel(pt_ref, q_ref, kn_ref, vn_ref, lam_ref, g_ref, *refs, pp, n_heads, lambda_init):
    del pt_ref
    k_refs, v_refs = refs[:pp], refs[pp:2 * pp]
    o_ref = refs[2 * pp]
    m_ref, l_ref, acc_ref, tmp_ref = refs[2 * pp + 1:]
    j = pl.program_id(1)
    nhc = 2 * n_heads

    @pl.when(j == 0)
    def _():
        m_ref[...] = jnp.full_like(m_ref, NEG_INF)
        l_ref[...] = jnp.zeros_like(l_ref)
        acc_ref[...] = jnp.zeros_like(acc_ref)

    qcol = q_ref[0]
    s = [_head_scores(kr[...], qcol) for kr in k_refs]
    m_old = m_ref[:, 0:1]
    smax = functools.reduce(jnp.maximum, s)
    m_new = jnp.maximum(m_old, jnp.max(smax, axis=-1, keepdims=True))
    alpha = jnp.exp(m_old - m_new)
    p = [jnp.exp(si - m_new) for si in s]
    psum = functools.reduce(lambda a, b: a + b, p)
    l_ref[...] = jnp.broadcast_to(alpha * l_ref[:, 0:1] + jnp.sum(psum, axis=-1, keepdims=True), l_ref.shape)
    m_ref[...] = jnp.broadcast_to(m_new, m_ref.shape)
    row_head = lax.broadcasted_iota(jnp.int32, (nhc, V_DIM), 0) // 2
    acc = alpha * acc_ref[...]
    page = k_refs[0].shape[1]
    for pi, vr in zip(p, v_refs):
        pb = pi.astype(BF16)
        for h in range(n_heads):
            vh = vr[pl.ds(h, page, stride=n_heads), :].astype(BF16)
            r = jnp.dot(pb, vh, preferred_element_type=F32)
            acc = acc + jnp.where(row_head == h, r, 0.0)
    acc_ref[...] = acc

    @pl.when(j == pl.num_programs(1) - 1)
    def _():
        s_new = _head_scores(kn_ref[0], qcol)[:, 0:1]
        m_old = m_ref[:, 0:1]
        m_f = jnp.maximum(m_old, s_new)
        a = jnp.exp(m_old - m_f)
        pn = jnp.exp(s_new - m_f)
        l_f = a * l_ref[:, 0:1] + pn
        tmp_ref[...] = (a * acc_ref[...] + pn * vn_ref[0]) / l_f
        o0 = tmp_ref[pl.ds(0, n_heads, stride=2), :]
        o1 = tmp_ref[pl.ds(1, n_heads, stride=2), :]
        o = o0 - _diff_lambda(lam_ref, lambda_init) * o1
        o = o * lax.rsqrt(jnp.mean(o * o, axis=-1, keepdims=True) + EPS)
        o_ref[0] = o * g_ref[...] * (1.0 - lambda_init)


def _decode_attn(page_table, qcol, kn_col, vn2, lam4, subln_g, cache_kt, cache_vr, pp, lambda_init):
    bd, n_pages = page_table.shape
    _, w, page = cache_kt.shape
    n_heads = cache_vr.shape[1] // page
    nhc = 2 * n_heads
    pt_flat = page_table.reshape(-1)
    fixed2 = lambda b, j, pt: (0, 0)
    perb = lambda b, j, pt: (b, 0, 0)

    def page_spec(i, rows, cols):
        return pl.BlockSpec((None, rows, cols), lambda b, j, pt: (pt[b * n_pages + j * pp + i], 0, 0))

    grid_spec = pltpu.PrefetchScalarGridSpec(
        num_scalar_prefetch=1,
        grid=(bd, n_pages // pp),
        in_specs=[pl.BlockSpec((1, w, LANES), perb), pl.BlockSpec((1, w, LANES), perb),
                  pl.BlockSpec((1, nhc, V_DIM), perb),
                  pl.BlockSpec((4, HEAD_DIM), fixed2), pl.BlockSpec((1, V_DIM), fixed2)]
                 + [page_spec(i, w, page) for i in range(pp)]
                 + [page_spec(i, page * n_heads, V_DIM) for i in range(pp)],
        out_specs=pl.BlockSpec((1, n_heads, V_DIM), perb),
        scratch_shapes=[pltpu.VMEM((nhc, LANES), F32), pltpu.VMEM((nhc, LANES), F32),
                        pltpu.VMEM((nhc, V_DIM), F32), pltpu.VMEM((nhc, V_DIM), F32)],
    )
    return pl.pallas_call(
        functools.partial(_decode_kernel, pp=pp, n_heads=n_heads, lambda_init=lambda_init),
        grid_spec=grid_spec,
        out_shape=jax.ShapeDtypeStruct((bd, n_heads, V_DIM), F32),
        compiler_params=_cparams(("parallel", "arbitrary")),
        name="decode_attn",
    )(pt_flat, qcol, kn_col, vn2, lam4, subln_g, *([cache_kt] * pp), *([cache_vr] * pp))


def _s5prep_kernel(arow_ref, acol_ref, ldt_ref, bt_ref, ct_ref, d_ref,
                   mt_ref, w_ref, vt_ref, lam_ref):
    L, hg, p = S5_CHUNK, S5_GROUP, S5_STATE
    dt = jnp.exp(ldt_ref[0])
    ar, ai = arow_ref[0, 0:1], arow_ref[0, 1:2]
    arc, aic = acol_ref[0, :, 0:1], acol_ref[0, :, 1:2]

    def powers(a_r, a_i, j):
        mag = jnp.exp(j * (a_r * dt))
        return mag * jnp.cos(j * (a_i * dt)), mag * jnp.sin(j * (a_i * dt))

    l1r, l1i = powers(ar, ai, 1.0)
    llr, lli = powers(ar, ai, float(L))
    lam_ref[0] = jnp.concatenate([llr, lli, l1r, l1i, jnp.zeros((4, p), F32)], axis=0)
    den = ar * ar + ai * ai
    cr = ((l1r - 1.0) * ar + l1i * ai) / den
    ci = (l1i * ar - (l1r - 1.0) * ai) / den
    btr, bti = bt_ref[0, 0], bt_ref[0, 1]
    bbr = btr * cr - bti * ci
    bbi = btr * ci + bti * cr
    jrow = lax.broadcasted_iota(jnp.int32, (L, p), 0).astype(F32)
    pr, pi = powers(ar, ai, jrow)
    for s in range(L):
        qr, qi = pr[L - 1 - s:L - s], pi[L - 1 - s:L - s]
        w_ref[0, 0, s * hg:(s + 1) * hg, :] = bbr * qr - bbi * qi
        w_ref[0, 1, s * hg:(s + 1) * hg, :] = bbr * qi + bbi * qr
    jl = (lax.broadcasted_iota(jnp.int32, (p, L * hg), 1) // hg).astype(F32)
    lpr, lpi = powers(arc, aic, jl)
    ctr, cti = ct_ref[0, 0], ct_ref[0, 1]
    cjr = ctr * lpr - cti * lpi
    cji = ctr * lpi + cti * lpr
    c1r, c1i = powers(arc, aic, 1.0)
    vt_ref[0, 0] = cjr * c1r - cji * c1i
    vt_ref[0, 1] = -(cjr * c1i + cji * c1r)
    kt = (jnp.dot(bbr, cjr, preferred_element_type=F32, precision=_HI)
          - jnp.dot(bbi, cji, preferred_element_type=F32, precision=_HI))
    n = L * hg
    ri = lax.broadcasted_iota(jnp.int32, (n, n), 0)
    cidx = lax.broadcasted_iota(jnp.int32, (n, n), 1)
    dtile = d_ref[0]
    for s in range(L):
        shift = (cidx - ri == s * hg).astype(F32)
        blk = jnp.dot(kt, shift, preferred_element_type=F32, precision=_HI)
        rr = lax.broadcasted_iota(jnp.int32, (hg, n), 0) + s * hg
        cc = lax.broadcasted_iota(jnp.int32, (hg, n), 1)
        mt_ref[0, s * hg:(s + 1) * hg, :] = blk + jnp.where(rr == cc, dtile, 0.0)


def _s5prep(arow, acol, ldt, bt, ct, dt_tiled):
    g = arow.shape[0]
    L, hg, p = S5_CHUNK, S5_GROUP, S5_STATE
    n = L * hg
    i3 = lambda i: (i, 0, 0)
    i4 = lambda i: (i, 0, 0, 0)
    return pl.pallas_call(
        _s5prep_kernel,
        grid=(g,),
        in_specs=[pl.BlockSpec((1, 2, p), i3), pl.BlockSpec((1, p, 2), i3), pl.BlockSpec((1, 1, 1), i3),
                  pl.BlockSpec((1, 2, hg, p), i4), pl.BlockSpec((1, 2, p, n), i4),
                  pl.BlockSpec((1, 1, n), i3)],
        out_specs=[pl.BlockSpec((1, n, n), i3), pl.BlockSpec((1, 2, n, p), i4),
                   pl.BlockSpec((1, 2, p, n), i4), pl.BlockSpec((1, SUBLANES, p), i3)],
        out_shape=[jax.ShapeDtypeStruct((g, n, n), F32), jax.ShapeDtypeStruct((g, 2, n, p), F32),
                   jax.ShapeDtypeStruct((g, 2, p, n), F32), jax.ShapeDtypeStruct((g, SUBLANES, p), F32)],
        compiler_params=_cparams(("parallel",)),
        name="s5prep",
    )(arow, acol, ldt, bt, ct, dt_tiled)


def _s5in_kernel(ut_ref, perm_ref, o_ref):
    g, nc, n = o_ref.shape[1:]
    hg, L = S5_GROUP, S5_CHUNK
    up = jnp.dot(ut_ref[0], perm_ref[...], preferred_element_type=F32)
    for gi in range(g):
        ugt = jnp.concatenate([up[gi * hg:(gi + 1) * hg, s * nc:(s + 1) * nc] for s in range(L)], axis=0)
        o_ref[0, gi] = ugt.T.astype(o_ref.dtype)


def _s5in(ut, perm):
    b, w, s = ut.shape
    g, nc, n = w // S5_GROUP, s // S5_CHUNK, S5_CHUNK * S5_GROUP
    return pl.pallas_call(
        _s5in_kernel,
        grid=(b,),
        in_specs=[pl.BlockSpec((1, w, s), lambda i: (i, 0, 0)),
                  pl.BlockSpec((s, s), lambda i: (0, 0), pipeline_mode=pl.Buffered(1))],
        out_specs=pl.BlockSpec((1, g, None, nc, n), lambda i: (i // SUBLANES, 0, i % SUBLANES, 0, 0)),
        out_shape=jax.ShapeDtypeStruct((b // SUBLANES, g, SUBLANES, nc, n), BF16),
        compiler_params=_cparams(("parallel",)),
        name="s5in",
    )(ut, perm)


def _s5out_kernel(y_ref, permt_ref, o_ref, ypt_ref):
    g, nc, n = y_ref.shape[1:]
    hg, L = S5_GROUP, S5_CHUNK
    for gi in range(g):
        ygt = y_ref[0, gi].astype(F32).T
        for t in range(L):
            ypt_ref[gi * hg:(gi + 1) * hg, t * nc:(t + 1) * nc] = ygt[t * hg:(t + 1) * hg, :].astype(BF16)
    yt = jnp.dot(ypt_ref[...], permt_ref[...], preferred_element_type=F32)
    o_ref[0] = yt.T.astype(o_ref.dtype)


def _s5out(y5, permt):
    nbh, g, _, nc, n = y5.shape
    b, s, w = nbh * SUBLANES, nc * S5_CHUNK, g * S5_GROUP
    return pl.pallas_call(
        _s5out_kernel,
        grid=(b,),
        in_specs=[pl.BlockSpec((1, g, None, nc, n), lambda i: (i // SUBLANES, 0, i % SUBLANES, 0, 0)),
                  pl.BlockSpec((s, s), lambda i: (0, 0), pipeline_mode=pl.Buffered(1))],
        out_specs=pl.BlockSpec((1, s, w), lambda i: (i, 0, 0)),
        out_shape=jax.ShapeDtypeStruct((b, s, w), BF16),
        scratch_shapes=[pltpu.VMEM((w, s), BF16)],
        compiler_params=_cparams(("parallel",)),
        name="s5out",
    )(y5, permt)


def _s5chunk_kernel(u_ref, mt_ref, w_ref, wsw_ref, vt_ref, lam_ref, y_ref, hl_ref, t1_ref, t2_ref, hs_ref):
    u = u_ref[0, 0]
    nc = u.shape[0] // SUBLANES
    p = S5_STATE
    t1_ref[...] = jnp.dot(u, w_ref[0].astype(BF16), preferred_element_type=F32)
    t2_ref[...] = jnp.dot(u, wsw_ref[0].astype(BF16), preferred_element_type=F32)
    lane = lax.broadcasted_iota(jnp.int32, (SUBLANES, 2 * p), 1)
    la = jnp.broadcast_to(lam_ref[0, 0:1], (SUBLANES, 2 * p))
    li2 = jnp.broadcast_to(lam_ref[0, 1:2], (SUBLANES, 2 * p))
    lb = jnp.where(lane < p, -li2, li2)

    def step(c, h):
        a, b = h
        s1 = t1_ref[pl.ds(c, SUBLANES, stride=nc), :]
        s2 = t2_ref[pl.ds(c, SUBLANES, stride=nc), :]
        hs_ref[pl.ds(pl.multiple_of(c * SUBLANES, SUBLANES), SUBLANES), :] = a
        return a * la + b * lb + s1, b * la - a * lb + s2

    z = jnp.zeros((SUBLANES, 2 * p), F32)
    a, _ = lax.fori_loop(0, nc, step, (z, z), unroll=8)
    hl_ref[0, 0] = a
    mt = mt_ref[0].astype(BF16)
    vt = vt_ref[0].astype(BF16)
    for bi in range(SUBLANES):
        rows = slice(bi * nc, (bi + 1) * nc)
        hb = hs_ref[pl.ds(bi, nc, stride=SUBLANES), :].astype(BF16)
        y = jnp.dot(u[rows], mt, preferred_element_type=F32) + jnp.dot(hb, vt, preferred_element_type=F32)
        y_ref[0, 0, rows, :] = y.astype(y_ref.dtype)


def _s5chunk(u4, mt, wcat, wsw, vtcat, lam2):
    nbh, g, rows, n = u4.shape
    p2 = 2 * S5_STATE
    um = lambda b, gi: (b, gi, 0, 0)
    g3 = lambda b, gi: (gi, 0, 0)
    return pl.pallas_call(
        _s5chunk_kernel,
        grid=(nbh, g),
        in_specs=[pl.BlockSpec((1, 1, rows, n), um), pl.BlockSpec((1, n, n), g3),
                  pl.BlockSpec((1, n, p2), g3), pl.BlockSpec((1, n, p2), g3), pl.BlockSpec((1, p2, n), g3),
                  pl.BlockSpec((1, SUBLANES, p2), g3)],
        out_specs=[pl.BlockSpec((1, 1, rows, n), um), pl.BlockSpec((1, 1, SUBLANES, p2), um)],
        out_shape=[jax.ShapeDtypeStruct(u4.shape, BF16),
                   jax.ShapeDtypeStruct((nbh, g, SUBLANES, p2), F32)],
        scratch_shapes=[pltpu.VMEM((rows, p2), F32), pltpu.VMEM((rows, p2), F32), pltpu.VMEM((rows, p2), F32)],
        compiler_params=_cparams(("parallel", "parallel")),
        name="s5chunk",
    )(u4, mt, wcat, wsw, vtcat, lam2)


def _s5step_kernel(u_ref, h0_ref, bb_ref, lam_ref, c_ref, d_ref, y_ref, h_ref):
    u = u_ref[0]
    h0r, h0i = h0_ref[0, 0], h0_ref[0, 1]
    l1r, l1i = lam_ref[0, 2:3], lam_ref[0, 3:4]
    bur = jnp.dot(u, bb_ref[0, 0], preferred_element_type=F32, precision=_HI)
    bui = jnp.dot(u, bb_ref[0, 1], preferred_element_type=F32, precision=_HI)
    hr = l1r * h0r - l1i * h0i + bur
    hi = l1r * h0i + l1i * h0r + bui
    h_ref[0, 0] = hr
    h_ref[0, 1] = hi
    y = _nt_dot(hr, c_ref[0, 0], precision=_HI) - _nt_dot(hi, c_ref[0, 1], precision=_HI)
    y_ref[0] = y + d_ref[0] * u


def _s5step(u3, h0, bb, lam, c2, d3):
    g, bd, hg = u3.shape
    p = S5_STATE
    i3 = lambda i: (i, 0, 0)
    i4 = lambda i: (i, 0, 0, 0)
    return pl.pallas_call(
        _s5step_kernel,
        grid=(g,),
        in_specs=[pl.BlockSpec((1, bd, hg), i3), pl.BlockSpec((1, 2, bd, p), i4),
                  pl.BlockSpec((1, 2, hg, p), i4), pl.BlockSpec((1, SUBLANES, p), i3),
                  pl.BlockSpec((1, 2, hg, p), i4), pl.BlockSpec((1, 1, hg), i3)],
        out_specs=[pl.BlockSpec((1, bd, hg), i3), pl.BlockSpec((1, 2, bd, p), i4)],
        out_shape=[jax.ShapeDtypeStruct((g, bd, hg), F32), jax.ShapeDtypeStruct((g, 2, bd, p), F32)],
        compiler_params=_cparams(("parallel",)),
        name="s5step",
    )(u3, h0, bb, lam, c2, d3)


def _postmix_kernel(x_ref, a_ref, y_ref, wglu_ref, bglu_ref, wout_ref, g2_ref, wr_ref, wrf_ref, br_ref,
                    x1_ref, h2_ref, comb_ref, *, precise, parts):
    th = x_ref.shape[0] // parts
    for part in range(parts):
        rows = slice(part * th, (part + 1) * th)
        _postmix_rows(x_ref, a_ref, y_ref, wglu_ref, bglu_ref, wout_ref, g2_ref, wr_ref, wrf_ref, br_ref,
                      x1_ref, h2_ref, comb_ref, rows, precise)


def _postmix_rows(x_ref, a_ref, y_ref, wglu_ref, bglu_ref, wout_ref, g2_ref, wr_ref, wrf_ref, br_ref,
                  x1_ref, h2_ref, comb_ref, rows, precise):
    wa = a_ref.shape[-1]

    def mm(act, w):
        if precise:
            return jnp.dot(act, w, preferred_element_type=F32, precision=_HI)
        return jnp.dot(act.astype(BF16), w, preferred_element_type=F32)

    y = jax.nn.gelu(y_ref[rows, :].astype(F32))
    z = mm(y, wglu_ref[...]) + bglu_ref[...]
    s5o = y * jax.nn.sigmoid(z)
    mix = mm(a_ref[rows, :].astype(F32) if precise else a_ref[rows, :], wout_ref[0:wa, :])
    mix += mm(s5o, wout_ref[wa:, :])
    x1 = x_ref[rows, :] + mix
    x1_ref[rows, :] = x1
    h2 = x1 * lax.rsqrt(jnp.mean(x1 * x1, axis=-1, keepdims=True) + EPS) * g2_ref[...]
    h2_ref[rows, :] = h2.astype(h2_ref.dtype)
    if precise:
        logits = jnp.dot(h2, wrf_ref[...], preferred_element_type=F32, precision=_HI)
    else:
        hi, lo = _split2(h2)
        hh = jnp.dot(hi, wr_ref[...], preferred_element_type=F32)
        logits = hh[:, :LANES] + hh[:, LANES:] + jnp.dot(lo, wr_ref[:, :LANES], preferred_element_type=F32)
    logits = logits + br_ref[...]
    ne = N_GROUPS_MOE * EXP_PER_GROUP
    lane = lax.broadcasted_iota(jnp.int32, logits.shape, 1).astype(F32)
    big = jnp.float32(1 << 20)
    gmask = (lane >= ne) & (lane < ne + N_GROUPS_MOE)
    gl = jnp.where(gmask, logits, NEG_INF)
    gmax = jnp.max(gl, axis=-1, keepdims=True)
    gidx = jnp.min(jnp.where(gl == gmax, lane, big), axis=-1, keepdims=True) - ne
    g_w = 1.0 / jnp.sum(jnp.where(gmask, jnp.exp(logits - gmax), 0.0), axis=-1, keepdims=True)
    lo = gidx * EXP_PER_GROUP
    el = jnp.where((lane >= lo) & (lane < lo + EXP_PER_GROUP), logits, NEG_INF)
    e1 = jnp.max(el, axis=-1, keepdims=True)
    i1 = jnp.min(jnp.where(el == e1, lane, big), axis=-1, keepdims=True)
    el2 = jnp.where(lane == i1, NEG_INF, el)
    e2 = jnp.max(el2, axis=-1, keepdims=True)
    i2 = jnp.min(jnp.where(el2 == e2, lane, big), axis=-1, keepdims=True)
    r = jnp.exp(e2 - e1)
    w1 = g_w / (1.0 + r)
    w2 = g_w * r / (1.0 + r)
    comb_ref[rows, :] = (jnp.where(lane == i1, w1, 0.0) + jnp.where(lane == i2, w2, 0.0)
                         + jnp.where(lane == GID_LANE, gidx, 0.0))


def _postmix(x2, attn, ys5, wglu, bglu, wout, ln2_g, w_router, b_router, tm, precise):
    t, d = x2.shape
    wa = attn.shape[1]
    ws = ys5.shape[1]
    row = lambda i: (i, 0)
    fixed = lambda i: (0, 0)
    wr_hi, wr_lo = _split2(w_router)
    wr_hl = jnp.concatenate([wr_hi, wr_lo], axis=1)
    parts = 2 if tm % 512 == 0 else 1
    return pl.pallas_call(
        functools.partial(_postmix_kernel, precise=precise, parts=parts),
        grid=(t // tm,),
        in_specs=[pl.BlockSpec((tm, d), row), pl.BlockSpec((tm, wa), row), pl.BlockSpec((tm, ws), row),
                  pl.BlockSpec((ws, ws), fixed), pl.BlockSpec((1, ws), fixed),
                  pl.BlockSpec((wa + ws, d), fixed), pl.BlockSpec((1, d), fixed),
                  pl.BlockSpec((d, 2 * LANES), fixed), pl.BlockSpec((d, LANES), fixed),
                  pl.BlockSpec((1, LANES), fixed)],
        out_specs=[pl.BlockSpec((tm, d), row), pl.BlockSpec((tm, d), row), pl.BlockSpec((tm, LANES), row)],
        out_shape=[jax.ShapeDtypeStruct((t, d), F32), jax.ShapeDtypeStruct((t, d), BF16),
                   jax.ShapeDtypeStruct((t, LANES), F32)],
        compiler_params=_cparams(("parallel",)),
        name="postmix",
    )(x2, attn, ys5, wglu, bglu, wout, ln2_g, wr_hl, w_router, b_router)


def _moe_kernel(h_ref, x1_ref, comb_ref, wg_ref, wu_ref, wd_ref, gf_ref, o_ref, acc_ref):
    e = pl.program_id(1)

    @pl.when(e == 0)
    def _():
        acc_ref[...] = jnp.zeros_like(acc_ref)

    h = h_ref[...]
    he = jax.nn.silu(jnp.dot(h, wg_ref[0], preferred_element_type=F32))
    he = he * jnp.dot(h, wu_ref[0], preferred_element_type=F32)
    comb = comb_ref[...]
    lane = lax.broadcasted_iota(jnp.int32, comb.shape, 1)
    ce = jnp.sum(jnp.where(lane == e, comb, 0.0), axis=-1, keepdims=True)
    acc_ref[...] += jnp.dot((he * ce).astype(BF16), wd_ref[0], preferred_element_type=F32)

    @pl.when(e == pl.num_programs(1) - 1)
    def _():
        x2 = x1_ref[...] + acc_ref[...]
        o_ref[...] = x2 * lax.rsqrt(jnp.mean(x2 * x2, axis=-1, keepdims=True) + EPS) * gf_ref[...]


def _moe(h2, x1, comb, wg_bf, wu_bf, wd_bf, lnf_g, tm):
    t, d = x1.shape
    ne, _, de = wg_bf.shape
    row = lambda i, e: (i, 0)
    return pl.pallas_call(
        _moe_kernel,
        grid=(t // tm, ne),
        in_specs=[pl.BlockSpec((tm, d), row), pl.BlockSpec((tm, d), row), pl.BlockSpec((tm, LANES), row),
                  pl.BlockSpec((1, d, de), lambda i, e: (e, 0, 0)),
                  pl.BlockSpec((1, d, de), lambda i, e: (e, 0, 0)),
                  pl.BlockSpec((1, de, d), lambda i, e: (e, 0, 0)),
                  pl.BlockSpec((1, d), lambda i, e: (0, 0))],
        out_specs=pl.BlockSpec((tm, d), row),
        out_shape=jax.ShapeDtypeStruct((t, d), F32),
        scratch_shapes=[pltpu.VMEM((tm, d), F32)],
        compiler_params=_cparams(("parallel", "arbitrary")),
        name="moe",
    )(h2, x1, comb, wg_bf, wu_bf, wd_bf, lnf_g)


def _moe_grouped_kernel(h_ref, x1_ref, comb_ref, ut_ref, lt_ref, wg_ref, wu_ref, wd_ref, gf_ref, o_ref,
                        rrow_ref, rcol_ref, chi_ref, clo_ref, cnt_ref):
    g = pl.program_id(1)
    tm, d = h_ref.shape
    gf32 = g.astype(F32)

    @pl.when(g == 0)
    def _():
        comb = comb_ref[...]
        combt = comb.T
        gidc = comb[:, GID_LANE:GID_LANE + 1]
        gidr = combt[GID_LANE:GID_LANE + 1, :]
        lane = lax.broadcasted_iota(jnp.int32, comb.shape, 1).astype(F32)
        sub = lax.broadcasted_iota(jnp.int32, (SUBLANES, tm), 0).astype(F32)
        ohc = lane == gidc
        ohr = sub == gidr
        cntc = jnp.dot(lt_ref[...], jnp.where(ohc, 1.0, 0.0).astype(BF16), preferred_element_type=F32)
        cntr = jnp.dot(jnp.where(ohr, 1.0, 0.0).astype(BF16), ut_ref[...], preferred_element_type=F32)
        rankc = jnp.sum(jnp.where(ohc, cntc, 0.0), axis=-1, keepdims=True) - 1.0
        rankr = jnp.sum(jnp.where(ohr, cntr, 0.0), axis=0, keepdims=True) - 1.0
        rcol_ref[...] = jnp.broadcast_to(rankc, rcol_ref.shape)
        rrow_ref[...] = jnp.concatenate([rankr, gidr, jnp.zeros((SUBLANES - 2, tm), F32)], axis=0)
        chi, clo = _split2(combt)
        chi_ref[...] = chi
        clo_ref[...] = clo
        for r in range(N_GROUPS_MOE):
            cnt_ref[r] = jnp.max(cntr[r:r + 1, :]).astype(jnp.int32)
        o_ref[...] = jnp.zeros_like(o_ref)

    posr = jnp.where(rrow_ref[1:2, :] == gf32, rrow_ref[0:1, :], -1.0)
    posc = jnp.where(comb_ref[:, GID_LANE:GID_LANE + 1] == gf32, rcol_ref[:, 0:1], -1.0)
    n_rows = cnt_ref[g]
    big = 2 * MOE_CHUNK
    mid = big + MOE_CHUNK // 2
    use_mid = jnp.logical_and(n_rows > big, n_rows <= mid)
    n_big = jnp.where(use_mid, 0, n_rows // big)
    n_mid = jnp.where(use_mid, 1, 0)
    n_small = jnp.where(use_mid, 0, (n_rows - n_big * big + (MOE_CHUNK - 1)) // MOE_CHUNK)

    def chunk(row0, ch):
        base = row0.astype(F32)
        rid = lax.broadcasted_iota(jnp.int32, (ch, tm), 0).astype(F32) + base
        cid = lax.broadcasted_iota(jnp.int32, (tm, ch), 1).astype(F32) + base
        sel = jnp.where(posr == rid, 1.0, 0.0).astype(BF16)
        selt = jnp.where(posc == cid, 1.0, 0.0).astype(BF16)
        xg = jnp.dot(sel, h_ref[...], preferred_element_type=F32).astype(BF16)
        cg = _nt_dot(sel, chi_ref[...]) + _nt_dot(sel, clo_ref[...])
        lane = lax.broadcasted_iota(jnp.int32, cg.shape, 1).astype(F32)
        z = jnp.zeros((ch, d), F32)
        for e in range(EXP_PER_GROUP):
            he = jax.nn.silu(jnp.dot(xg, wg_ref[e], preferred_element_type=F32))
            he = he * jnp.dot(xg, wu_ref[e], preferred_element_type=F32)
            ce = jnp.sum(jnp.where(lane == gf32 * EXP_PER_GROUP + e, cg, 0.0), axis=-1, keepdims=True)
            z = z + jnp.dot((he * ce).astype(BF16), wd_ref[e], preferred_element_type=F32)
        o_ref[...] += jnp.dot(selt, z.astype(BF16), preferred_element_type=F32)

    def big_body(k, carry):
        chunk(k * big, big)
        return carry

    def mid_body(k, carry):
        chunk(k * mid, mid)
        return carry

    def small_body(k, carry):
        chunk(n_big * big + k * MOE_CHUNK, MOE_CHUNK)
        return carry

    lax.fori_loop(0, n_big, big_body, 0)
    lax.fori_loop(0, n_mid, mid_body, 0)
    lax.fori_loop(0, n_small, small_body, 0)

    @pl.when(g == pl.num_programs(1) - 1)
    def _():
        x2 = x1_ref[...] + o_ref[...]
        o_ref[...] = x2 * lax.rsqrt(jnp.mean(x2 * x2, axis=-1, keepdims=True) + EPS) * gf_ref[...]


def _moe_grouped(h2, x1, comb, wg_bf, wu_bf, wd_bf, lnf_g, tm):
    t, d = x1.shape
    ne, _, de = wg_bf.shape
    ng = ne // EXP_PER_GROUP
    tri = jnp.arange(tm)[:, None] <= jnp.arange(tm)[None, :]
    ut = tri.astype(BF16)
    lt = tri.T.astype(BF16)
    once = pl.Buffered(1)
    tile = lambda i, g: (i, 0)
    fixed = lambda i, g: (0, 0)
    wmap = lambda i, g: (g, 0, 0)
    return pl.pallas_call(
        _moe_grouped_kernel,
        grid=(t // tm, ng),
        in_specs=[pl.BlockSpec((tm, d), tile), pl.BlockSpec((tm, d), tile),
                  pl.BlockSpec((tm, LANES), tile),
                  pl.BlockSpec((tm, tm), fixed, pipeline_mode=once),
                  pl.BlockSpec((tm, tm), fixed, pipeline_mode=once),
                  pl.BlockSpec((EXP_PER_GROUP, d, de), wmap),
                  pl.BlockSpec((EXP_PER_GROUP, d, de), wmap),
                  pl.BlockSpec((EXP_PER_GROUP, de, d), wmap),
                  pl.BlockSpec((1, d), fixed)],
        out_specs=pl.BlockSpec((tm, d), tile),
        out_shape=jax.ShapeDtypeStruct((t, d), F32),
        scratch_shapes=[pltpu.VMEM((SUBLANES, tm), F32), pltpu.VMEM((tm, LANES), F32),
                        pltpu.VMEM((LANES, tm), BF16), pltpu.VMEM((LANES, tm), BF16),
                        pltpu.SMEM((N_GROUPS_MOE,), jnp.int32)],
        compiler_params=_cparams(("parallel", "arbitrary")),
        name="moe_grouped",
    )(h2, x1, comb, ut, lt, wg_bf, wu_bf, wd_bf, lnf_g)


def _tile(n, want):
    t = min(n, want)
    while n % t:
        t //= 2
    return t


def kernel(x_prompt, x_sample, cache_k, cache_v, state_s5_re, state_s5_im, page_table, ln1_g, w_in, lambda_q1, lambda_k1, lambda_q2, lambda_k2, subln_g, s5_a_re, s5_a_im, s5_log_dt, s5_b_re, s5_b_im, s5_c_re, s5_c_im, s5_d, w_glu, b_glu, w_out, ln2_g, w_router_group, b_router_group, w_router_expert, b_router_expert, w_gate, w_up, w_down, ln_f_g):
    depth = ln1_g.shape[0]
    assert depth == 1, "single-layer step"
    b, s, d = x_prompt.shape
    bd, ds, _ = x_sample.shape
    assert ds == 1
    n_pages = page_table.shape[1]
    page = cache_k.shape[2]
    past_len = n_pages * page
    wq = w_in.shape[2] // 4
    n_heads = wq // V_DIM
    g = s5_a_re.shape[1]
    p = S5_STATE
    hg = S5_GROUP
    L = S5_CHUNK
    n = L * hg
    assert b % SUBLANES == 0 and s % L == 0 and g * hg == wq
    lambda_init = 0.8 - 0.6 * math.exp(-0.3 * 0)

    w_in_bf = w_in[0].astype(BF16)
    wglu_bf = w_glu[0].astype(BF16)
    wout_bf = w_out[0].astype(BF16)
    wg_bf, wu_bf, wd_bf = (w[0].astype(BF16) for w in (w_gate, w_up, w_down))
    ne = w_gate.shape[1]
    w_router = jnp.concatenate(
        [jnp.transpose(w_router_expert[0], (1, 0, 2)).reshape(d, ne), w_router_group[0],
         jnp.zeros((d, LANES - ne - N_GROUPS_MOE), F32)], axis=1)
    b_router = jnp.concatenate([b_router_expert[0].reshape(ne), b_router_group[0],
                                jnp.zeros((LANES - ne - N_GROUPS_MOE,), F32)])[None, :]
    lam4 = jnp.concatenate([lambda_q1, lambda_k1, lambda_q2, lambda_k2], axis=0)
    ln1 = ln1_g[0][None, :]
    ln2 = ln2_g[0][None, :]
    lnf = ln_f_g[None, :]
    subg = subln_g[0][None, :]
    bglu = b_glu[0][None, :]

    arow = jnp.stack([s5_a_re[0], s5_a_im[0]], axis=1)
    acol = jnp.stack([s5_a_re[0], s5_a_im[0]], axis=2)
    ldt = s5_log_dt[0].reshape(g, 1, 1)
    bt = jnp.stack([jnp.swapaxes(s5_b_re[0], 1, 2), jnp.swapaxes(s5_b_im[0], 1, 2)], axis=1)
    ct = jnp.stack([jnp.tile(jnp.swapaxes(s5_c_re[0], 1, 2), (1, 1, L)),
                    jnp.tile(jnp.swapaxes(s5_c_im[0], 1, 2), (1, 1, L))], axis=1)
    d_g = s5_d[0].reshape(g, 1, hg)
    mt, w_s5, vt, lam_s5 = _s5prep(arow, acol, ldt, bt, ct, jnp.tile(d_g, (1, 1, L)))

    tm = _tile(b * s, 512)
    x2 = x_prompt.reshape(b * s, d)
    pos_p = jnp.arange(s, dtype=jnp.int32)
    rc, rs1, rs2 = _rope_tables(pos_p)
    rct, rst = _rope_tables_t(pos_p)
    w_qv_bf = jnp.concatenate([w_in_bf[:, :wq], w_in_bf[:, 2 * wq:3 * wq]], axis=1)
    wkut_bf = jnp.concatenate([w_in_bf[:, wq:2 * wq], w_in_bf[:, 3 * wq:]], axis=1).T
    tq = _tile(s, 512)
    q, vf, vb, ut, ktf, ktb = _inproj_prompt(x_prompt, ln1, w_qv_bf, wkut_bf, rc, rs1, rs2, rct, rst,
                                             _tile(s, tm), tq)
    attn = _attn(q, ktb, vb, lam4, subg, n_heads, lambda_init)
    kf = jnp.transpose(ktf.reshape(b, 2 * n_heads, HEAD_DIM, s), (0, 3, 1, 2))
    nbh, nc = b // SUBLANES, s // L
    tok = jnp.arange(s)
    perm = (((tok % L) * nc + tok // L)[:, None] == tok[None, :]).astype(BF16)
    u5 = _s5in(ut, perm)
    wcat = jnp.concatenate([w_s5[:, 0], w_s5[:, 1]], axis=-1)
    wsw = jnp.concatenate([w_s5[:, 1], w_s5[:, 0]], axis=-1)
    vtcat = jnp.concatenate([vt[:, 0], vt[:, 1]], axis=1)
    lam2 = jnp.concatenate([lam_s5, lam_s5], axis=-1)
    y5, hl = _s5chunk(u5.reshape(nbh, g, SUBLANES * nc, n), mt, wcat, wsw, vtcat, lam2)
    ys5 = _s5out(y5.reshape(nbh, g, SUBLANES, nc, n), perm.T).reshape(b * s, wq)
    hl = hl.transpose(0, 2, 1, 3).reshape(b, g, 2, p).transpose(2, 0, 1, 3)
    x1, h2, comb = _postmix(x2, attn.reshape(b * s, wq), ys5, wglu_bf, bglu, wout_bf, ln2,
                            w_router, b_router, tm, False)
    y_prompt = _moe_grouped(h2, x1, comb, wg_bf, wu_bf, wd_bf, lnf, _tile(b * s, 1024)).reshape(b, s, d)

    xs2 = x_sample.reshape(bd, d)
    pos_s = jnp.full((bd,), past_len, jnp.int32)
    sc, ss1, ss2 = _rope_tables(pos_s)
    qs, kfs, vfs, _, _, us = _inproj(xs2, ln1, w_in[0], sc, ss1, ss2, bd, F32, F32)
    cache_kt = jnp.transpose(cache_k[0], (0, 2, 3, 1)).reshape(-1, wq, page)
    cache_vr = cache_v[0].reshape(-1, page * n_heads, V_DIM)
    qcol = jnp.broadcast_to(qs[:, :, None], (bd, wq, LANES))
    kn_col = jnp.broadcast_to(kfs[:, :, None], (bd, wq, LANES))
    vn2 = jnp.repeat(vfs.reshape(bd, n_heads, V_DIM), 2, axis=1)
    pp = _tile(n_pages, 16)
    attn_s = _decode_attn(page_table, qcol, kn_col, vn2, lam4, subg, cache_kt, cache_vr, pp, lambda_init)
    u3 = us.reshape(bd, g, hg).transpose(1, 0, 2)
    h0 = jnp.stack([state_s5_re[0], state_s5_im[0]]).transpose(2, 0, 1, 3)
    bb = w_s5[:, :, (L - 1) * hg:, :]
    c2 = jnp.stack([s5_c_re[0], s5_c_im[0]], axis=1)
    ys3, hs = _s5step(u3, h0, bb, lam_s5, c2, d_g)
    ys_s5 = ys3.transpose(1, 0, 2).reshape(bd, wq)
    x1s, h2s, combs = _postmix(xs2, attn_s.reshape(bd, wq), ys_s5, w_glu[0], bglu, w_out[0], ln2,
                               w_router, b_router, bd, True)
    y_sample = _moe(h2s, x1s, combs, wg_bf, wu_bf, wd_bf, lnf, bd).reshape(bd, 1, d)
    hs = hs.transpose(1, 2, 0, 3)

    return (y_prompt, y_sample,
            kf.reshape(1, b, s, 2 * n_heads, HEAD_DIM), vf.reshape(1, b, s, n_heads, V_DIM),
            hl[0][None], hl[1][None],
            kfs.reshape(1, bd, 1, 2 * n_heads, HEAD_DIM), vfs.reshape(1, bd, 1, n_heads, V_DIM),
            hs[0][None], hs[1][None])
```

```python
import functools
import math

import jax
import jax.numpy as jnp
from jax import lax
from jax.experimental import pallas as pl
from jax.experimental.pallas import tpu as pltpu

F32 = jnp.float32
BF16 = jnp.bfloat16

HEAD_DIM = 64
V_DIM = 128
ROT_DIM = 16
ROPE_THETA = 500000.0
S5_GROUP = 16
S5_STATE = 64
S5_CHUNK = 16
SUBLANES = 8
N_GROUPS_MOE = 4
EXP_PER_GROUP = 8
EPS = 1e-5
NEG_INF = -1e30
LANES = 128
GID_LANE = LANES - 1
MOE_CHUNK = 128
VMEM_LIMIT = 56 * 1024 * 1024

_HI = lax.Precision.HIGHEST


def _cparams(sem):
    return pltpu.CompilerParams(dimension_semantics=sem, vmem_limit_bytes=VMEM_LIMIT)


def _nt_dot(a, b, **kw):
    return lax.dot_general(a, b, (((1,), (1,)), ((), ())), preferred_element_type=F32, **kw)


def _inproj_kernel(x_ref, g_ref, w_ref, c_ref, s1_ref, s2_ref,
                   q_ref, kf_ref, vf_ref, kb_ref, vb_ref, u_ref):
    x = x_ref[...]
    ms = jnp.mean(x * x, axis=-1, keepdims=True)
    hn = x * lax.rsqrt(ms + EPS) * g_ref[...]
    proj = jnp.dot(hn, w_ref[...], preferred_element_type=F32, precision=_HI)
    w = q_ref.shape[-1]
    c = c_ref[...]
    s1 = s1_ref[...]
    s2 = s2_ref[...]
    for j in range(w // LANES):
        sl = slice(j * LANES, (j + 1) * LANES)
        zq = proj[:, j * LANES:(j + 1) * LANES]
        zk = proj[:, w + j * LANES:w + (j + 1) * LANES]
        rq = zq * c + pltpu.roll(zq, ROT_DIM // 2, 1) * s1 + pltpu.roll(zq, LANES - ROT_DIM // 2, 1) * s2
        rk = zk * c + pltpu.roll(zk, ROT_DIM // 2, 1) * s1 + pltpu.roll(zk, LANES - ROT_DIM // 2, 1) * s2
        q_ref[:, sl] = (rq * (HEAD_DIM ** -0.5)).astype(q_ref.dtype)
        kf_ref[:, sl] = rk
        kb_ref[:, sl] = rk.astype(BF16)
    v = proj[:, 2 * w:3 * w]
    vf_ref[...] = v
    vb_ref[...] = v.astype(BF16)
    u_ref[...] = proj[:, 3 * w:4 * w].astype(u_ref.dtype)


def _inproj(x2, ln_g, w_bf, rc, rs1, rs2, tm, q_dtype, u_dtype):
    t, d = x2.shape
    w = w_bf.shape[1] // 4
    nt = t // tm
    npos = rc.shape[0] // tm
    row = lambda i: (i, 0)
    pos = lambda i: (i % npos, 0)
    fixed = lambda i: (0, 0)
    outs = [jax.ShapeDtypeStruct((t, w), q_dtype), jax.ShapeDtypeStruct((t, w), F32),
            jax.ShapeDtypeStruct((t, w), F32), jax.ShapeDtypeStruct((t, w), BF16),
            jax.ShapeDtypeStruct((t, w), BF16), jax.ShapeDtypeStruct((t, w), u_dtype)]
    return pl.pallas_call(
        _inproj_kernel,
        grid=(nt,),
        in_specs=[pl.BlockSpec((tm, d), row), pl.BlockSpec((1, d), fixed),
                  pl.BlockSpec((d, 4 * w), fixed),
                  pl.BlockSpec((tm, LANES), pos), pl.BlockSpec((tm, LANES), pos),
                  pl.BlockSpec((tm, LANES), pos)],
        out_specs=[pl.BlockSpec((tm, w), row)] * 6,
        out_shape=outs,
        compiler_params=_cparams(("parallel",)),
        name="inproj",
    )(x2, ln_g, w_bf, rc, rs1, rs2)


def _inproj_prompt_kernel(x_ref, g_ref, w_ref, wkt_ref, c_ref, s1_ref, s2_ref, ct_ref, st_ref,
                          q_ref, vf_ref, vb_ref, ut_ref, ktf_ref, ktb_ref):
    x = x_ref[0]
    ms = jnp.mean(x * x, axis=-1, keepdims=True)
    hn = (x * lax.rsqrt(ms + EPS) * g_ref[...]).astype(BF16)
    proj = jnp.dot(hn, w_ref[...], preferred_element_type=F32)
    w = q_ref.shape[-1]
    ktu = _nt_dot(wkt_ref[...], hn)
    kt = ktu[:w]
    ut_ref[0] = ktu[w:].astype(ut_ref.dtype)
    c, s1, s2 = c_ref[...], s1_ref[...], s2_ref[...]
    for j in range(w // LANES):
        sl = slice(j * LANES, (j + 1) * LANES)
        zq = proj[:, sl]
        rq = zq * c + pltpu.roll(zq, ROT_DIM // 2, 1) * s1 + pltpu.roll(zq, LANES - ROT_DIM // 2, 1) * s2
        q_ref[0, :, sl] = (rq * (HEAD_DIM ** -0.5)).astype(q_ref.dtype)
    v = proj[:, w:2 * w]
    nh = w // V_DIM
    for h in range(nh):
        vf_ref[0, pl.ds(h, v.shape[0], stride=nh), :] = v[:, h * V_DIM:(h + 1) * V_DIM]
    vb_ref[0] = v.astype(BF16)
    ct, st = ct_ref[...], st_ref[...]
    half = ROT_DIM // 2
    tk = ktb_ref.shape[-1]
    for hc in range(w // HEAD_DIM):
        base = hc * HEAD_DIM
        x1, x2 = kt[base:base + half], kt[base + half:base + ROT_DIM]
        blk = jnp.concatenate([x1 * ct - x2 * st, x2 * ct + x1 * st, kt[base + ROT_DIM:base + HEAD_DIM]], axis=0)
        ktf_ref[0, base:base + HEAD_DIM, :] = blk
        for t in range(blk.shape[1] // tk):
            ktb_ref[0, t, base:base + HEAD_DIM, :] = blk[:, t * tk:(t + 1) * tk].astype(BF16)


def _inproj_prompt(x3, ln_g, w_qv_bf, wkut_bf, rc, rs1, rs2, rct, rst, tm, tk):
    b, s, d = x3.shape
    w = wkut_bf.shape[0] // 2
    half = ROT_DIM // 2
    tok = lambda bi, i: (bi, i, 0)
    pos = lambda bi, i: (i, 0)
    fixed = lambda bi, i: (0, 0)
    nh = w // V_DIM
    outs = [jax.ShapeDtypeStruct((b, s, w), BF16), jax.ShapeDtypeStruct((b, s * nh, V_DIM), F32),
            jax.ShapeDtypeStruct((b, s, w), BF16), jax.ShapeDtypeStruct((b, w, s), BF16),
            jax.ShapeDtypeStruct((b, w, s), F32), jax.ShapeDtypeStruct((b, s // tk, w, tk), BF16)]
    return pl.pallas_call(
        _inproj_prompt_kernel,
        grid=(b, s // tm),
        in_specs=[pl.BlockSpec((1, tm, d), tok), pl.BlockSpec((1, d), fixed),
                  pl.BlockSpec((d, 2 * w), fixed), pl.BlockSpec((2 * w, d), fixed),
                  pl.BlockSpec((tm, LANES), pos), pl.BlockSpec((tm, LANES), pos), pl.BlockSpec((tm, LANES), pos),
                  pl.BlockSpec((half, tm), lambda bi, i: (0, i)), pl.BlockSpec((half, tm), lambda bi, i: (0, i))],
        out_specs=[pl.BlockSpec((1, tm, w), tok), pl.BlockSpec((1, tm * nh, V_DIM), tok),
                   pl.BlockSpec((1, tm, w), tok)]
                  + [pl.BlockSpec((1, w, tm), lambda bi, i: (bi, 0, i))] * 2
                  + [pl.BlockSpec((1, tm // tk, w, tk), lambda bi, i: (bi, i, 0, 0))],
        out_shape=outs,
        compiler_params=_cparams(("parallel", "parallel")),
        name="inproj_prompt",
    )(x3, ln_g, w_qv_bf, wkut_bf, rc, rs1, rs2, rct, rst)


def _rope_tables_t(pos):
    inv = ROPE_THETA ** (-jnp.arange(0, ROT_DIM, 2, dtype=F32) / ROT_DIM)
    ang = inv[:, None] * pos.astype(F32)[None, :]
    return jnp.cos(ang), jnp.sin(ang)


def _rope_tables(pos):
    half = ROT_DIM // 2
    inv = ROPE_THETA ** (-jnp.arange(0, ROT_DIM, 2, dtype=F32) / ROT_DIM)
    ang = pos.astype(F32)[:, None] * inv[None, :]
    cos, sin = jnp.cos(ang), jnp.sin(ang)
    n = pos.shape[0]
    pad = jnp.zeros((n, HEAD_DIM - ROT_DIM), F32)
    c = jnp.concatenate([cos, cos, pad + 1.0], axis=1)
    s1 = jnp.concatenate([jnp.zeros((n, half), F32), sin, pad], axis=1)
    s2 = jnp.concatenate([-sin, jnp.zeros((n, half), F32), pad], axis=1)
    rep = LANES // HEAD_DIM
    return tuple(jnp.tile(a, (1, rep)) for a in (c, s1, s2))


def _diff_lambda(lam_ref, lambda_init):
    l = lam_ref[...]
    a = jnp.sum(l[0:1] * l[1:2], axis=-1, keepdims=True)
    b = jnp.sum(l[2:3] * l[3:4], axis=-1, keepdims=True)
    return jnp.exp(a) - jnp.exp(b) + lambda_init


def _attn_kernel(q_ref, kt_ref, v_ref, lam_ref, g_ref, o_ref, s_ref, m_ref, acc_ref, vext_ref, *, tq, lambda_init):
    nq = q_ref.shape[1] // tq
    nl = tq // LANES
    vext_ref[:, 0:V_DIM] = v_ref[0]
    ones_col = lax.broadcasted_iota(jnp.int32, (v_ref.shape[1], LANES), 1) == 0
    vext_ref[:, V_DIM:] = jnp.where(ones_col, 1.0, 0.0).astype(BF16)
    lam = _diff_lambda(lam_ref, lambda_init)
    lane = lax.broadcasted_iota(jnp.int32, (tq, LANES), 1)
    row = lax.broadcasted_iota(jnp.int32, (tq, tq), 0)
    col = lax.broadcasted_iota(jnp.int32, (tq, tq), 1)
    for qi in range(nq):
        q = q_ref[0, qi * tq:(qi + 1) * tq, :]
        zero = jnp.zeros_like(q)
        qc = (jnp.where(lane < HEAD_DIM, q, zero), jnp.where(lane >= HEAD_DIM, q, zero))
        base = qi * (qi + 1) // 2
        for j in range(qi + 1):
            kt = kt_ref[0, j]
            for c in range(2):
                s = jnp.dot(qc[c], kt, preferred_element_type=F32)
                if j == qi:
                    s = jnp.where(col <= row, s, NEG_INF)
                s_ref[c, base + j] = s
                m = s[:, 0:LANES] if j == 0 else jnp.maximum(m_ref[qi, c], s[:, 0:LANES])
                for t in range(1, nl):
                    m = jnp.maximum(m, s[:, t * LANES:(t + 1) * LANES])
                m_ref[qi, c] = m
        for c in range(2):
            m_ref[qi, c] = jnp.broadcast_to(jnp.max(m_ref[qi, c], axis=-1, keepdims=True), (tq, LANES))
        for j in range(qi + 1):
            v = vext_ref[j * tq:(j + 1) * tq, :]
            for c in range(2):
                m = m_ref[qi, c]
                ps = [jnp.exp(s_ref[c, base + j, :, t * LANES:(t + 1) * LANES] - m) for t in range(nl)]
                pv = jnp.dot(jnp.concatenate(ps, axis=1).astype(BF16), v, preferred_element_type=F32)
                if j == 0:
                    acc_ref[qi, c] = pv
                else:
                    acc_ref[qi, c] += pv
        a0, a1 = acc_ref[qi, 0], acc_ref[qi, 1]
        o = a0[:, :V_DIM] / a0[:, V_DIM:V_DIM + 1] - lam * (a1[:, :V_DIM] / a1[:, V_DIM:V_DIM + 1])
        o = o * lax.rsqrt(jnp.mean(o * o, axis=-1, keepdims=True) + EPS)
        o_ref[0, qi * tq:(qi + 1) * tq, :] = (o * g_ref[...] * (1.0 - lambda_init)).astype(o_ref.dtype)


def _attn(q, kt4, v, lam4, subln_g, n_heads, lambda_init):
    b, s, w = q.shape
    nkt, tq = kt4.shape[1], kt4.shape[3]
    seq = lambda bi, h: (bi, 0, h)
    fixed = lambda bi, h: (0, 0)
    nq = s // tq
    return pl.pallas_call(
        functools.partial(_attn_kernel, tq=tq, lambda_init=lambda_init),
        grid=(b, n_heads),
        in_specs=[pl.BlockSpec((1, s, LANES), seq),
                  pl.BlockSpec((1, nkt, LANES, tq), lambda bi, h: (bi, 0, h, 0)),
                  pl.BlockSpec((1, s, LANES), seq),
                  pl.BlockSpec((4, HEAD_DIM), fixed), pl.BlockSpec((1, V_DIM), fixed)],
        out_specs=pl.BlockSpec((1, s, LANES), seq),
        out_shape=jax.ShapeDtypeStruct((b, s, w), BF16),
        scratch_shapes=[pltpu.VMEM((2, nq * (nq + 1) // 2, tq, tq), F32), pltpu.VMEM((nq, 2, tq, LANES), F32),
                        pltpu.VMEM((nq, 2, tq, 2 * V_DIM), F32), pltpu.VMEM((s, 2 * V_DIM), BF16)],
        compiler_params=_cparams(("parallel", "parallel")),
        name="attn",
    )(q, kt4, v, lam4, subln_g)


def _split2(x):
    hi = x.astype(BF16)
    return hi, (x - hi.astype(F32)).astype(BF16)


def _head_scores(k, qcol):
    prod = k * qcol
    return jnp.sum(prod.reshape(prod.shape[0] // HEAD_DIM, HEAD_DIM, prod.shape[1]), axis=1)


def _decode_kernel(pt_ref, q_ref, kn_ref, vn_ref, lam_ref, g_ref, *refs, pp, nb, n_heads, lambda_init):
    del pt_ref
    npg = nb * pp
    k_refs, v_refs = refs[:npg], refs[npg:2 * npg]
    o_ref = refs[2 * npg]
    m_ref, l_ref, acc_ref, tmp_ref = refs[2 * npg + 1:]
    j = pl.program_id(1)
    nhc = 2 * n_heads

    @pl.when(j == 0)
    def _():
        m_ref[...] = jnp.full_like(m_ref, NEG_INF)
        l_ref[...] = jnp.zeros_like(l_ref)
        acc_ref[...] = jnp.zeros_like(acc_ref)

    row_head = lax.broadcasted_iota(jnp.int32, (nhc, V_DIM), 0) // 2
    page = k_refs[0].shape[1]
    for r in range(nb):
        qcol = q_ref[r]
        s = [_head_scores(kr[...], qcol) for kr in k_refs[r * pp:(r + 1) * pp]]
        m_old = m_ref[r, :, 0:1]
        smax = functools.reduce(jnp.maximum, s)
        m_new = jnp.maximum(m_old, jnp.max(smax, axis=-1, keepdims=True))
        alpha = jnp.exp(m_old - m_new)
        p = [jnp.exp(si - m_new) for si in s]
        psum = functools.reduce(lambda a, b: a + b, p)
        l_ref[r] = jnp.broadcast_to(alpha * l_ref[r, :, 0:1] + jnp.sum(psum, axis=-1, keepdims=True),
                                    l_ref.shape[1:])
        m_ref[r] = jnp.broadcast_to(m_new, m_ref.shape[1:])
        acc = alpha * acc_ref[r]
        for pi, vr in zip(p, v_refs[r * pp:(r + 1) * pp]):
            pb = pi.astype(BF16)
            for h in range(n_heads):
                vh = vr[pl.ds(h, page, stride=n_heads), :].astype(BF16)
                res = jnp.dot(pb, vh, preferred_element_type=F32)
                acc = acc + jnp.where(row_head == h, res, 0.0)
        acc_ref[r] = acc

    @pl.when(j == pl.num_programs(1) - 1)
    def _():
        lam = _diff_lambda(lam_ref, lambda_init)
        for r in range(nb):
            s_new = _head_scores(kn_ref[r], q_ref[r])[:, 0:1]
            m_old = m_ref[r, :, 0:1]
            m_f = jnp.maximum(m_old, s_new)
            a = jnp.exp(m_old - m_f)
            pn = jnp.exp(s_new - m_f)
            l_f = a * l_ref[r, :, 0:1] + pn
            tmp_ref[r] = (a * acc_ref[r] + pn * vn_ref[r]) / l_f
            o0 = tmp_ref[r, pl.ds(0, n_heads, stride=2), :]
            o1 = tmp_ref[r, pl.ds(1, n_heads, stride=2), :]
            o = o0 - lam * o1
            o = o * lax.rsqrt(jnp.mean(o * o, axis=-1, keepdims=True) + EPS)
            o_ref[r] = o * g_ref[...] * (1.0 - lambda_init)


def _decode_attn(page_table, qcol, kn_col, vn2, lam4, subln_g, cache_kt, cache_vr, pp, nb, lambda_init):
    bd, n_pages = page_table.shape
    _, w, page = cache_kt.shape
    n_heads = cache_vr.shape[1] // page
    nhc = 2 * n_heads
    pt_flat = page_table.reshape(-1)
    fixed2 = lambda b, j, pt: (0, 0)
    perb = lambda b, j, pt: (b, 0, 0)

    def page_spec(r, i, rows, cols):
        return pl.BlockSpec((None, rows, cols),
                            lambda b, j, pt: (pt[(b * nb + r) * n_pages + j * pp + i], 0, 0))

    slots = [(r, i) for r in range(nb) for i in range(pp)]
    grid_spec = pltpu.PrefetchScalarGridSpec(
        num_scalar_prefetch=1,
        grid=(bd // nb, n_pages // pp),
        in_specs=[pl.BlockSpec((nb, w, LANES), perb), pl.BlockSpec((nb, w, LANES), perb),
                  pl.BlockSpec((nb, nhc, V_DIM), perb),
                  pl.BlockSpec((4, HEAD_DIM), fixed2), pl.BlockSpec((1, V_DIM), fixed2)]
                 + [page_spec(r, i, w, page) for r, i in slots]
                 + [page_spec(r, i, page * n_heads, V_DIM) for r, i in slots],
        out_specs=pl.BlockSpec((nb, n_heads, V_DIM), perb),
        scratch_shapes=[pltpu.VMEM((nb, nhc, LANES), F32), pltpu.VMEM((nb, nhc, LANES), F32),
                        pltpu.VMEM((nb, nhc, V_DIM), F32), pltpu.VMEM((nb, nhc, V_DIM), F32)],
    )
    npg = nb * pp
    return pl.pallas_call(
        functools.partial(_decode_kernel, pp=pp, nb=nb, n_heads=n_heads, lambda_init=lambda_init),
        grid_spec=grid_spec,
        out_shape=jax.ShapeDtypeStruct((bd, n_heads, V_DIM), F32),
        compiler_params=_cparams(("parallel", "arbitrary")),
        name="decode_attn",
    )(pt_flat, qcol, kn_col, vn2, lam4, subln_g, *([cache_kt] * npg), *([cache_vr] * npg))


def _s5prep_kernel(arow_ref, acol_ref, ldt_ref, bt_ref, ct_ref, d_ref,
                   mt_ref, w_ref, vt_ref, lam_ref):
    L, hg, p = S5_CHUNK, S5_GROUP, S5_STATE
    dt = jnp.exp(ldt_ref[0])
    ar, ai = arow_ref[0, 0:1], arow_ref[0, 1:2]
    arc, aic = acol_ref[0, :, 0:1], acol_ref[0, :, 1:2]

    def powers(a_r, a_i, j):
        mag = jnp.exp(j * (a_r * dt))
        return mag * jnp.cos(j * (a_i * dt)), mag * jnp.sin(j * (a_i * dt))

    l1r, l1i = powers(ar, ai, 1.0)
    llr, lli = powers(ar, ai, float(L))
    lam_ref[0] = jnp.concatenate([llr, lli, l1r, l1i, jnp.zeros((4, p), F32)], axis=0)
    den = ar * ar + ai * ai
    cr = ((l1r - 1.0) * ar + l1i * ai) / den
    ci = (l1i * ar - (l1r - 1.0) * ai) / den
    btr, bti = bt_ref[0, 0], bt_ref[0, 1]
    bbr = btr * cr - bti * ci
    bbi = btr * ci + bti * cr
    jrow = lax.broadcasted_iota(jnp.int32, (L, p), 0).astype(F32)
    pr, pi = powers(ar, ai, jrow)
    for s in range(L):
        qr, qi = pr[L - 1 - s:L - s], pi[L - 1 - s:L - s]
        w_ref[0, 0, s * hg:(s + 1) * hg, :] = bbr * qr - bbi * qi
        w_ref[0, 1, s * hg:(s + 1) * hg, :] = bbr * qi + bbi * qr
    jl = (lax.broadcasted_iota(jnp.int32, (p, L * hg), 1) // hg).astype(F32)
    lpr, lpi = powers(arc, aic, jl)
    ctr, cti = ct_ref[0, 0], ct_ref[0, 1]
    cjr = ctr * lpr - cti * lpi
    cji = ctr * lpi + cti * lpr
    c1r, c1i = powers(arc, aic, 1.0)
    vt_ref[0, 0] = cjr * c1r - cji * c1i
    vt_ref[0, 1] = -(cjr * c1i + cji * c1r)
    kt = (jnp.dot(bbr, cjr, preferred_element_type=F32, precision=_HI)
          - jnp.dot(bbi, cji, preferred_element_type=F32, precision=_HI))
    n = L * hg
    ri = lax.broadcasted_iota(jnp.int32, (n, n), 0)
    cidx = lax.broadcasted_iota(jnp.int32, (n, n), 1)
    dtile = d_ref[0]
    for s in range(L):
        shift = (cidx - ri == s * hg).astype(F32)
        blk = jnp.dot(kt, shift, preferred_element_type=F32, precision=_HI)
        rr = lax.broadcasted_iota(jnp.int32, (hg, n), 0) + s * hg
        cc = lax.broadcasted_iota(jnp.int32, (hg, n), 1)
        mt_ref[0, s * hg:(s + 1) * hg, :] = blk + jnp.where(rr == cc, dtile, 0.0)


def _s5prep(arow, acol, ldt, bt, ct, dt_tiled):
    g = arow.shape[0]
    L, hg, p = S5_CHUNK, S5_GROUP, S5_STATE
    n = L * hg
    i3 = lambda i: (i, 0, 0)
    i4 = lambda i: (i, 0, 0, 0)
    return pl.pallas_call(
        _s5prep_kernel,
        grid=(g,),
        in_specs=[pl.BlockSpec((1, 2, p), i3), pl.BlockSpec((1, p, 2), i3), pl.BlockSpec((1, 1, 1), i3),
                  pl.BlockSpec((1, 2, hg, p), i4), pl.BlockSpec((1, 2, p, n), i4),
                  pl.BlockSpec((1, 1, n), i3)],
        out_specs=[pl.BlockSpec((1, n, n), i3), pl.BlockSpec((1, 2, n, p), i4),
                   pl.BlockSpec((1, 2, p, n), i4), pl.BlockSpec((1, SUBLANES, p), i3)],
        out_shape=[jax.ShapeDtypeStruct((g, n, n), F32), jax.ShapeDtypeStruct((g, 2, n, p), F32),
                   jax.ShapeDtypeStruct((g, 2, p, n), F32), jax.ShapeDtypeStruct((g, SUBLANES, p), F32)],
        compiler_params=_cparams(("parallel",)),
        name="s5prep",
    )(arow, acol, ldt, bt, ct, dt_tiled)


def _s5in_kernel(ut_ref, perm_ref, o_ref):
    g, nc, n = o_ref.shape[1:]
    hg, L = S5_GROUP, S5_CHUNK
    up = jnp.dot(ut_ref[0], perm_ref[...], preferred_element_type=F32)
    for gi in range(g):
        ugt = jnp.concatenate([up[gi * hg:(gi + 1) * hg, s * nc:(s + 1) * nc] for s in range(L)], axis=0)
        o_ref[0, gi] = ugt.T.astype(o_ref.dtype)


def _s5in(ut, perm):
    b, w, s = ut.shape
    g, nc, n = w // S5_GROUP, s // S5_CHUNK, S5_CHUNK * S5_GROUP
    return pl.pallas_call(
        _s5in_kernel,
        grid=(b,),
        in_specs=[pl.BlockSpec((1, w, s), lambda i: (i, 0, 0)),
                  pl.BlockSpec((s, s), lambda i: (0, 0), pipeline_mode=pl.Buffered(1))],
        out_specs=pl.BlockSpec((1, g, None, nc, n), lambda i: (i // SUBLANES, 0, i % SUBLANES, 0, 0)),
        out_shape=jax.ShapeDtypeStruct((b // SUBLANES, g, SUBLANES, nc, n), BF16),
        compiler_params=_cparams(("parallel",)),
        name="s5in",
    )(ut, perm)


def _s5out_kernel(y_ref, permt_ref, o_ref, ypt_ref):
    g, nc, n = y_ref.shape[1:]
    hg, L = S5_GROUP, S5_CHUNK
    for gi in range(g):
        ygt = y_ref[0, gi].astype(F32).T
        for t in range(L):
            ypt_ref[gi * hg:(gi + 1) * hg, t * nc:(t + 1) * nc] = ygt[t * hg:(t + 1) * hg, :].astype(BF16)
    yt = jnp.dot(ypt_ref[...], permt_ref[...], preferred_element_type=F32)
    o_ref[0] = yt.T.astype(o_ref.dtype)


def _s5out(y5, permt):
    nbh, g, _, nc, n = y5.shape
    b, s, w = nbh * SUBLANES, nc * S5_CHUNK, g * S5_GROUP
    return pl.pallas_call(
        _s5out_kernel,
        grid=(b,),
        in_specs=[pl.BlockSpec((1, g, None, nc, n), lambda i: (i // SUBLANES, 0, i % SUBLANES, 0, 0)),
                  pl.BlockSpec((s, s), lambda i: (0, 0), pipeline_mode=pl.Buffered(1))],
        out_specs=pl.BlockSpec((1, s, w), lambda i: (i, 0, 0)),
        out_shape=jax.ShapeDtypeStruct((b, s, w), BF16),
        scratch_shapes=[pltpu.VMEM((w, s), BF16)],
        compiler_params=_cparams(("parallel",)),
        name="s5out",
    )(y5, permt)


def _s5chunk_kernel(u_ref, mt_ref, w_ref, wsw_ref, vt_ref, lam_ref, y_ref, hl_ref, t1_ref, t2_ref, hs_ref):
    u = u_ref[0, 0]
    nc = u.shape[0] // SUBLANES
    p = S5_STATE
    t1_ref[...] = jnp.dot(u, w_ref[0].astype(BF16), preferred_element_type=F32)
    t2_ref[...] = jnp.dot(u, wsw_ref[0].astype(BF16), preferred_element_type=F32)
    lane = lax.broadcasted_iota(jnp.int32, (SUBLANES, 2 * p), 1)
    la = jnp.broadcast_to(lam_ref[0, 0:1], (SUBLANES, 2 * p))
    li2 = jnp.broadcast_to(lam_ref[0, 1:2], (SUBLANES, 2 * p))
    lb = jnp.where(lane < p, -li2, li2)

    def step(c, h):
        a, b = h
        s1 = t1_ref[pl.ds(c, SUBLANES, stride=nc), :]
        s2 = t2_ref[pl.ds(c, SUBLANES, stride=nc), :]
        hs_ref[pl.ds(pl.multiple_of(c * SUBLANES, SUBLANES), SUBLANES), :] = a
        return a * la + b * lb + s1, b * la - a * lb + s2

    z = jnp.zeros((SUBLANES, 2 * p), F32)
    a, _ = lax.fori_loop(0, nc, step, (z, z), unroll=8)
    hl_ref[0, 0] = a
    mt = mt_ref[0].astype(BF16)
    vt = vt_ref[0].astype(BF16)
    for bi in range(SUBLANES):
        rows = slice(bi * nc, (bi + 1) * nc)
        hb = hs_ref[pl.ds(bi, nc, stride=SUBLANES), :].astype(BF16)
        y = jnp.dot(u[rows], mt, preferred_element_type=F32) + jnp.dot(hb, vt, preferred_element_type=F32)
        y_ref[0, 0, rows, :] = y.astype(y_ref.dtype)


def _s5chunk(u4, mt, wcat, wsw, vtcat, lam2):
    nbh, g, rows, n = u4.shape
    p2 = 2 * S5_STATE
    um = lambda b, gi: (b, gi, 0, 0)
    g3 = lambda b, gi: (gi, 0, 0)
    return pl.pallas_call(
        _s5chunk_kernel,
        grid=(nbh, g),
        in_specs=[pl.BlockSpec((1, 1, rows, n), um), pl.BlockSpec((1, n, n), g3),
                  pl.BlockSpec((1, n, p2), g3), pl.BlockSpec((1, n, p2), g3), pl.BlockSpec((1, p2, n), g3),
                  pl.BlockSpec((1, SUBLANES, p2), g3)],
        out_specs=[pl.BlockSpec((1, 1, rows, n), um), pl.BlockSpec((1, 1, SUBLANES, p2), um)],
        out_shape=[jax.ShapeDtypeStruct(u4.shape, BF16),
                   jax.ShapeDtypeStruct((nbh, g, SUBLANES, p2), F32)],
        scratch_shapes=[pltpu.VMEM((rows, p2), F32), pltpu.VMEM((rows, p2), F32), pltpu.VMEM((rows, p2), F32)],
        compiler_params=_cparams(("parallel", "parallel")),
        name="s5chunk",
    )(u4, mt, wcat, wsw, vtcat, lam2)


def _s5step_kernel(u_ref, h0_ref, bb_ref, lam_ref, c_ref, d_ref, y_ref, h_ref):
    u = u_ref[0]
    h0r, h0i = h0_ref[0, 0], h0_ref[0, 1]
    l1r, l1i = lam_ref[0, 2:3], lam_ref[0, 3:4]
    bur = jnp.dot(u, bb_ref[0, 0], preferred_element_type=F32, precision=_HI)
    bui = jnp.dot(u, bb_ref[0, 1], preferred_element_type=F32, precision=_HI)
    hr = l1r * h0r - l1i * h0i + bur
    hi = l1r * h0i + l1i * h0r + bui
    h_ref[0, 0] = hr
    h_ref[0, 1] = hi
    y = _nt_dot(hr, c_ref[0, 0], precision=_HI) - _nt_dot(hi, c_ref[0, 1], precision=_HI)
    y_ref[0] = y + d_ref[0] * u


def _s5step(u3, h0, bb, lam, c2, d3):
    g, bd, hg = u3.shape
    p = S5_STATE
    i3 = lambda i: (i, 0, 0)
    i4 = lambda i: (i, 0, 0, 0)
    return pl.pallas_call(
        _s5step_kernel,
        grid=(g,),
        in_specs=[pl.BlockSpec((1, bd, hg), i3), pl.BlockSpec((1, 2, bd, p), i4),
                  pl.BlockSpec((1, 2, hg, p), i4), pl.BlockSpec((1, SUBLANES, p), i3),
                  pl.BlockSpec((1, 2, hg, p), i4), pl.BlockSpec((1, 1, hg), i3)],
        out_specs=[pl.BlockSpec((1, bd, hg), i3), pl.BlockSpec((1, 2, bd, p), i4)],
        out_shape=[jax.ShapeDtypeStruct((g, bd, hg), F32), jax.ShapeDtypeStruct((g, 2, bd, p), F32)],
        compiler_params=_cparams(("parallel",)),
        name="s5step",
    )(u3, h0, bb, lam, c2, d3)


def _postmix_kernel(x_ref, a_ref, y_ref, wglu_ref, bglu_ref, wout_ref, g2_ref, wr_ref, wrf_ref, br_ref,
                    x1_ref, h2_ref, comb_ref, *, precise, parts):
    th = x_ref.shape[0] // parts
    for part in range(parts):
        rows = slice(part * th, (part + 1) * th)
        _postmix_rows(x_ref, a_ref, y_ref, wglu_ref, bglu_ref, wout_ref, g2_ref, wr_ref, wrf_ref, br_ref,
                      x1_ref, h2_ref, comb_ref, rows, precise)


def _postmix_rows(x_ref, a_ref, y_ref, wglu_ref, bglu_ref, wout_ref, g2_ref, wr_ref, wrf_ref, br_ref,
                  x1_ref, h2_ref, comb_ref, rows, precise):
    wa = a_ref.shape[-1]

    def mm(act, w):
        if precise:
            return jnp.dot(act, w, preferred_element_type=F32, precision=_HI)
        return jnp.dot(act.astype(BF16), w, preferred_element_type=F32)

    y = jax.nn.gelu(y_ref[rows, :].astype(F32))
    z = mm(y, wglu_ref[...]) + bglu_ref[...]
    s5o = y * jax.nn.sigmoid(z)
    mix = mm(a_ref[rows, :].astype(F32) if precise else a_ref[rows, :], wout_ref[0:wa, :])
    mix += mm(s5o, wout_ref[wa:, :])
    x1 = x_ref[rows, :] + mix
    x1_ref[rows, :] = x1
    h2 = x1 * lax.rsqrt(jnp.mean(x1 * x1, axis=-1, keepdims=True) + EPS) * g2_ref[...]
    h2_ref[rows, :] = h2.astype(h2_ref.dtype)
    if precise:
        logits = jnp.dot(h2, wrf_ref[...], preferred_element_type=F32, precision=_HI)
    else:
        hi, lo = _split2(h2)
        hh = jnp.dot(hi, wr_ref[...], preferred_element_type=F32)
        logits = hh[:, :LANES] + hh[:, LANES:] + jnp.dot(lo, wr_ref[:, :LANES], preferred_element_type=F32)
    logits = logits + br_ref[...]
    ne = N_GROUPS_MOE * EXP_PER_GROUP
    lane = lax.broadcasted_iota(jnp.int32, logits.shape, 1).astype(F32)
    big = jnp.float32(1 << 20)
    gmask = (lane >= ne) & (lane < ne + N_GROUPS_MOE)
    gl = jnp.where(gmask, logits, NEG_INF)
    gmax = jnp.max(gl, axis=-1, keepdims=True)
    gidx = jnp.min(jnp.where(gl == gmax, lane, big), axis=-1, keepdims=True) - ne
    g_w = 1.0 / jnp.sum(jnp.where(gmask, jnp.exp(logits - gmax), 0.0), axis=-1, keepdims=True)
    lo = gidx * EXP_PER_GROUP
    el = jnp.where((lane >= lo) & (lane < lo + EXP_PER_GROUP), logits, NEG_INF)
    e1 = jnp.max(el, axis=-1, keepdims=True)
    i1 = jnp.min(jnp.where(el == e1, lane, big), axis=-1, keepdims=True)
    el2 = jnp.where(lane == i1, NEG_INF, el)
    e2 = jnp.max(el2, axis=-1, keepdims=True)
    i2 = jnp.min(jnp.where(el2 == e2, lane, big), axis=-1, keepdims=True)
    r = jnp.exp(e2 - e1)
    w1 = g_w / (1.0 + r)
    w2 = g_w * r / (1.0 + r)
    comb_ref[rows, :] = (jnp.where(lane == i1, w1, 0.0) + jnp.where(lane == i2, w2, 0.0)
                         + jnp.where(lane == GID_LANE, gidx, 0.0))


def _postmix(x2, attn, ys5, wglu, bglu, wout, ln2_g, w_router, b_router, tm, precise):
    t, d = x2.shape
    wa = attn.shape[1]
    ws = ys5.shape[1]
    row = lambda i: (i, 0)
    fixed = lambda i: (0, 0)
    wr_hi, wr_lo = _split2(w_router)
    wr_hl = jnp.concatenate([wr_hi, wr_lo], axis=1)
    parts = 2 if tm % 512 == 0 else 1
    return pl.pallas_call(
        functools.partial(_postmix_kernel, precise=precise, parts=parts),
        grid=(t // tm,),
        in_specs=[pl.BlockSpec((tm, d), row), pl.BlockSpec((tm, wa), row), pl.BlockSpec((tm, ws), row),
                  pl.BlockSpec((ws, ws), fixed), pl.BlockSpec((1, ws), fixed),
                  pl.BlockSpec((wa + ws, d), fixed), pl.BlockSpec((1, d), fixed),
                  pl.BlockSpec((d, 2 * LANES), fixed), pl.BlockSpec((d, LANES), fixed),
                  pl.BlockSpec((1, LANES), fixed)],
        out_specs=[pl.BlockSpec((tm, d), row), pl.BlockSpec((tm, d), row), pl.BlockSpec((tm, LANES), row)],
        out_shape=[jax.ShapeDtypeStruct((t, d), F32), jax.ShapeDtypeStruct((t, d), BF16),
                   jax.ShapeDtypeStruct((t, LANES), F32)],
        compiler_params=_cparams(("parallel",)),
        name="postmix",
    )(x2, attn, ys5, wglu, bglu, wout, ln2_g, wr_hl, w_router, b_router)


def _moe_kernel(h_ref, x1_ref, comb_ref, wg_ref, wu_ref, wd_ref, gf_ref, o_ref, acc_ref):
    e = pl.program_id(1)

    @pl.when(e == 0)
    def _():
        acc_ref[...] = jnp.zeros_like(acc_ref)

    h = h_ref[...]
    he = jax.nn.silu(jnp.dot(h, wg_ref[0], preferred_element_type=F32))
    he = he * jnp.dot(h, wu_ref[0], preferred_element_type=F32)
    comb = comb_ref[...]
    lane = lax.broadcasted_iota(jnp.int32, comb.shape, 1)
    ce = jnp.sum(jnp.where(lane == e, comb, 0.0), axis=-1, keepdims=True)
    acc_ref[...] += jnp.dot((he * ce).astype(BF16), wd_ref[0], preferred_element_type=F32)

    @pl.when(e == pl.num_programs(1) - 1)
    def _():
        x2 = x1_ref[...] + acc_ref[...]
        o_ref[...] = x2 * lax.rsqrt(jnp.mean(x2 * x2, axis=-1, keepdims=True) + EPS) * gf_ref[...]


def _moe(h2, x1, comb, wg_bf, wu_bf, wd_bf, lnf_g, tm):
    t, d = x1.shape
    ne, _, de = wg_bf.shape
    row = lambda i, e: (i, 0)
    return pl.pallas_call(
        _moe_kernel,
        grid=(t // tm, ne),
        in_specs=[pl.BlockSpec((tm, d), row), pl.BlockSpec((tm, d), row), pl.BlockSpec((tm, LANES), row),
                  pl.BlockSpec((1, d, de), lambda i, e: (e, 0, 0)),
                  pl.BlockSpec((1, d, de), lambda i, e: (e, 0, 0)),
                  pl.BlockSpec((1, de, d), lambda i, e: (e, 0, 0)),
                  pl.BlockSpec((1, d), lambda i, e: (0, 0))],
        out_specs=pl.BlockSpec((tm, d), row),
        out_shape=jax.ShapeDtypeStruct((t, d), F32),
        scratch_shapes=[pltpu.VMEM((tm, d), F32)],
        compiler_params=_cparams(("parallel", "arbitrary")),
        name="moe",
    )(h2, x1, comb, wg_bf, wu_bf, wd_bf, lnf_g)


def _moe_grouped_kernel(h_ref, x1_ref, comb_ref, ut_ref, lt_ref, wg_ref, wu_ref, wd_ref, gf_ref, o_ref,
                        rrow_ref, rcol_ref, chi_ref, clo_ref, cnt_ref):
    g = pl.program_id(1)
    tm, d = h_ref.shape
    gf32 = g.astype(F32)

    @pl.when(g == 0)
    def _():
        comb = comb_ref[...]
        combt = comb.T
        gidc = comb[:, GID_LANE:GID_LANE + 1]
        gidr = combt[GID_LANE:GID_LANE + 1, :]
        lane = lax.broadcasted_iota(jnp.int32, comb.shape, 1).astype(F32)
        sub = lax.broadcasted_iota(jnp.int32, (SUBLANES, tm), 0).astype(F32)
        ohc = lane == gidc
        ohr = sub == gidr
        cntc = jnp.dot(lt_ref[...], jnp.where(ohc, 1.0, 0.0).astype(BF16), preferred_element_type=F32)
        cntr = jnp.dot(jnp.where(ohr, 1.0, 0.0).astype(BF16), ut_ref[...], preferred_element_type=F32)
        rankc = jnp.sum(jnp.where(ohc, cntc, 0.0), axis=-1, keepdims=True) - 1.0
        rankr = jnp.sum(jnp.where(ohr, cntr, 0.0), axis=0, keepdims=True) - 1.0
        rcol_ref[...] = jnp.broadcast_to(rankc, rcol_ref.shape)
        rrow_ref[...] = jnp.concatenate([rankr, gidr, jnp.zeros((SUBLANES - 2, tm), F32)], axis=0)
        chi, clo = _split2(combt)
        chi_ref[...] = chi
        clo_ref[...] = clo
        for r in range(N_GROUPS_MOE):
            cnt_ref[r] = jnp.max(cntr[r:r + 1, :]).astype(jnp.int32)
        o_ref[...] = jnp.zeros_like(o_ref)

    posr = jnp.where(rrow_ref[1:2, :] == gf32, rrow_ref[0:1, :], -1.0)
    posc = jnp.where(comb_ref[:, GID_LANE:GID_LANE + 1] == gf32, rcol_ref[:, 0:1], -1.0)
    n_rows = cnt_ref[g]
    big = 2 * MOE_CHUNK
    mid = big + MOE_CHUNK // 2
    use_mid = jnp.logical_and(n_rows > big, n_rows <= mid)
    n_big = jnp.where(use_mid, 0, n_rows // big)
    n_mid = jnp.where(use_mid, 1, 0)
    n_small = jnp.where(use_mid, 0, (n_rows - n_big * big + (MOE_CHUNK - 1)) // MOE_CHUNK)

    def chunk(row0, ch):
        base = row0.astype(F32)
        rid = lax.broadcasted_iota(jnp.int32, (ch, tm), 0).astype(F32) + base
        cid = lax.broadcasted_iota(jnp.int32, (tm, ch), 1).astype(F32) + base
        sel = jnp.where(posr == rid, 1.0, 0.0).astype(BF16)
        selt = jnp.where(posc == cid, 1.0, 0.0).astype(BF16)
        xg = jnp.dot(sel, h_ref[...], preferred_element_type=F32).astype(BF16)
        cg = _nt_dot(sel, chi_ref[...]) + _nt_dot(sel, clo_ref[...])
        lane = lax.broadcasted_iota(jnp.int32, cg.shape, 1).astype(F32)
        z = jnp.zeros((ch, d), F32)
        for e in range(EXP_PER_GROUP):
            he = jax.nn.silu(jnp.dot(xg, wg_ref[e], preferred_element_type=F32))
            he = he * jnp.dot(xg, wu_ref[e], preferred_element_type=F32)
            ce = jnp.sum(jnp.where(lane == gf32 * EXP_PER_GROUP + e, cg, 0.0), axis=-1, keepdims=True)
            z = z + jnp.dot((he * ce).astype(BF16), wd_ref[e], preferred_element_type=F32)
        o_ref[...] += jnp.dot(selt, z.astype(BF16), preferred_element_type=F32)

    def big_body(k, carry):
        chunk(k * big, big)
        return carry

    def mid_body(k, carry):
        chunk(k * mid, mid)
        return carry

    def small_body(k, carry):
        chunk(n_big * big + k * MOE_CHUNK, MOE_CHUNK)
        return carry

    lax.fori_loop(0, n_big, big_body, 0)
    lax.fori_loop(0, n_mid, mid_body, 0)
    lax.fori_loop(0, n_small, small_body, 0)

    @pl.when(g == pl.num_programs(1) - 1)
    def _():
        x2 = x1_ref[...] + o_ref[...]
        o_ref[...] = x2 * lax.rsqrt(jnp.mean(x2 * x2, axis=-1, keepdims=True) + EPS) * gf_ref[...]


def _moe_grouped(h2, x1, comb, wg_bf, wu_bf, wd_bf, lnf_g, tm):
    t, d = x1.shape
    ne, _, de = wg_bf.shape
    ng = ne // EXP_PER_GROUP
    tri = jnp.arange(tm)[:, None] <= jnp.arange(tm)[None, :]
    ut = tri.astype(BF16)
    lt = tri.T.astype(BF16)
    once = pl.Buffered(1)
    tile = lambda i, g: (i, 0)
    fixed = lambda i, g: (0, 0)
    wmap = lambda i, g: (g, 0, 0)
    return pl.pallas_call(
        _moe_grouped_kernel,
        grid=(t // tm, ng),
        in_specs=[pl.BlockSpec((tm, d), tile), pl.BlockSpec((tm, d), tile),
                  pl.BlockSpec((tm, LANES), tile),
                  pl.BlockSpec((tm, tm), fixed, pipeline_mode=once),
                  pl.BlockSpec((tm, tm), fixed, pipeline_mode=once),
                  pl.BlockSpec((EXP_PER_GROUP, d, de), wmap),
                  pl.BlockSpec((EXP_PER_GROUP, d, de), wmap),
                  pl.BlockSpec((EXP_PER_GROUP, de, d), wmap),
                  pl.BlockSpec((1, d), fixed)],
        out_specs=pl.BlockSpec((tm, d), tile),
        out_shape=jax.ShapeDtypeStruct((t, d), F32),
        scratch_shapes=[pltpu.VMEM((SUBLANES, tm), F32), pltpu.VMEM((tm, LANES), F32),
                        pltpu.VMEM((LANES, tm), BF16), pltpu.VMEM((LANES, tm), BF16),
                        pltpu.SMEM((N_GROUPS_MOE,), jnp.int32)],
        compiler_params=_cparams(("parallel", "arbitrary")),
        name="moe_grouped",
    )(h2, x1, comb, ut, lt, wg_bf, wu_bf, wd_bf, lnf_g)


def _tile(n, want):
    t = min(n, want)
    while n % t:
        t //= 2
    return t


def kernel(x_prompt, x_sample, cache_k, cache_v, state_s5_re, state_s5_im, page_table, ln1_g, w_in, lambda_q1, lambda_k1, lambda_q2, lambda_k2, subln_g, s5_a_re, s5_a_im, s5_log_dt, s5_b_re, s5_b_im, s5_c_re, s5_c_im, s5_d, w_glu, b_glu, w_out, ln2_g, w_router_group, b_router_group, w_router_expert, b_router_expert, w_gate, w_up, w_down, ln_f_g):
    depth = ln1_g.shape[0]
    assert depth == 1, "single-layer step"
    b, s, d = x_prompt.shape
    bd, ds, _ = x_sample.shape
    assert ds == 1
    n_pages = page_table.shape[1]
    page = cache_k.shape[2]
    past_len = n_pages * page
    wq = w_in.shape[2] // 4
    n_heads = wq // V_DIM
    g = s5_a_re.shape[1]
    p = S5_STATE
    hg = S5_GROUP
    L = S5_CHUNK
    n = L * hg
    assert b % SUBLANES == 0 and s % L == 0 and g * hg == wq
    lambda_init = 0.8 - 0.6 * math.exp(-0.3 * 0)

    w_in_bf = w_in[0].astype(BF16)
    wglu_bf = w_glu[0].astype(BF16)
    wout_bf = w_out[0].astype(BF16)
    wg_bf, wu_bf, wd_bf = (w[0].astype(BF16) for w in (w_gate, w_up, w_down))
    ne = w_gate.shape[1]
    w_router = jnp.concatenate(
        [jnp.transpose(w_router_expert[0], (1, 0, 2)).reshape(d, ne), w_router_group[0],
         jnp.zeros((d, LANES - ne - N_GROUPS_MOE), F32)], axis=1)
    b_router = jnp.concatenate([b_router_expert[0].reshape(ne), b_router_group[0],
                                jnp.zeros((LANES - ne - N_GROUPS_MOE,), F32)])[None, :]
    lam4 = jnp.concatenate([lambda_q1, lambda_k1, lambda_q2, lambda_k2], axis=0)
    ln1 = ln1_g[0][None, :]
    ln2 = ln2_g[0][None, :]
    lnf = ln_f_g[None, :]
    subg = subln_g[0][None, :]
    bglu = b_glu[0][None, :]

    arow = jnp.stack([s5_a_re[0], s5_a_im[0]], axis=1)
    acol = jnp.stack([s5_a_re[0], s5_a_im[0]], axis=2)
    ldt = s5_log_dt[0].reshape(g, 1, 1)
    bt = jnp.stack([jnp.swapaxes(s5_b_re[0], 1, 2), jnp.swapaxes(s5_b_im[0], 1, 2)], axis=1)
    ct = jnp.stack([jnp.tile(jnp.swapaxes(s5_c_re[0], 1, 2), (1, 1, L)),
                    jnp.tile(jnp.swapaxes(s5_c_im[0], 1, 2), (1, 1, L))], axis=1)
    d_g = s5_d[0].reshape(g, 1, hg)
    mt, w_s5, vt, lam_s5 = _s5prep(arow, acol, ldt, bt, ct, jnp.tile(d_g, (1, 1, L)))

    tm = _tile(b * s, 512)
    x2 = x_prompt.reshape(b * s, d)
    pos_p = jnp.arange(s, dtype=jnp.int32)
    rc, rs1, rs2 = _rope_tables(pos_p)
    rct, rst = _rope_tables_t(pos_p)
    w_qv_bf = jnp.concatenate([w_in_bf[:, :wq], w_in_bf[:, 2 * wq:3 * wq]], axis=1)
    wkut_bf = jnp.concatenate([w_in_bf[:, wq:2 * wq], w_in_bf[:, 3 * wq:]], axis=1).T
    tq = _tile(s, 512)
    q, vf, vb, ut, ktf, ktb = _inproj_prompt(x_prompt, ln1, w_qv_bf, wkut_bf, rc, rs1, rs2, rct, rst,
                                             _tile(s, tm), tq)
    attn = _attn(q, ktb, vb, lam4, subg, n_heads, lambda_init)
    kf = jnp.transpose(ktf.reshape(b, 2 * n_heads, HEAD_DIM, s), (0, 3, 1, 2))
    nbh, nc = b // SUBLANES, s // L
    tok = jnp.arange(s)
    perm = (((tok % L) * nc + tok // L)[:, None] == tok[None, :]).astype(BF16)
    u5 = _s5in(ut, perm)
    wcat = jnp.concatenate([w_s5[:, 0], w_s5[:, 1]], axis=-1)
    wsw = jnp.concatenate([w_s5[:, 1], w_s5[:, 0]], axis=-1)
    vtcat = jnp.concatenate([vt[:, 0], vt[:, 1]], axis=1)
    lam2 = jnp.concatenate([lam_s5, lam_s5], axis=-1)
    y5, hl = _s5chunk(u5.reshape(nbh, g, SUBLANES * nc, n), mt, wcat, wsw, vtcat, lam2)
    ys5 = _s5out(y5.reshape(nbh, g, SUBLANES, nc, n), perm.T).reshape(b * s, wq)
    hl = hl.transpose(0, 2, 1, 3).reshape(b, g, 2, p).transpose(2, 0, 1, 3)
    x1, h2, comb = _postmix(x2, attn.reshape(b * s, wq), ys5, wglu_bf, bglu, wout_bf, ln2,
                            w_router, b_router, tm, False)
    y_prompt = _moe_grouped(h2, x1, comb, wg_bf, wu_bf, wd_bf, lnf, _tile(b * s, 1024)).reshape(b, s, d)

    xs2 = x_sample.reshape(bd, d)
    pos_s = jnp.full((bd,), past_len, jnp.int32)
    sc, ss1, ss2 = _rope_tables(pos_s)
    qs, kfs, vfs, _, _, us = _inproj(xs2, ln1, w_in[0], sc, ss1, ss2, bd, F32, F32)
    cache_kt = jnp.transpose(cache_k[0], (0, 2, 3, 1)).reshape(-1, wq, page)
    cache_vr = cache_v[0].reshape(-1, page * n_heads, V_DIM)
    qcol = jnp.broadcast_to(qs[:, :, None], (bd, wq, LANES))
    kn_col = jnp.broadcast_to(kfs[:, :, None], (bd, wq, LANES))
    vn2 = jnp.repeat(vfs.reshape(bd, n_heads, V_DIM), 2, axis=1)
    pp = _tile(n_pages, 8)
    nb_dec = 2 if bd % 2 == 0 else 1
    attn_s = _decode_attn(page_table, qcol, kn_col, vn2, lam4, subg, cache_kt, cache_vr, pp, nb_dec,
                          lambda_init)
    u3 = us.reshape(bd, g, hg).transpose(1, 0, 2)
    h0 = jnp.stack([state_s5_re[0], state_s5_im[0]]).transpose(2, 0, 1, 3)
    bb = w_s5[:, :, (L - 1) * hg:, :]
    c2 = jnp.stack([s5_c_re[0], s5_c_im[0]], axis=1)
    ys3, hs = _s5step(u3, h0, bb, lam_s5, c2, d_g)
    ys_s5 = ys3.transpose(1, 0, 2).reshape(bd, wq)
    x1s, h2s, combs = _postmix(xs2, attn_s.reshape(bd, wq), ys_s5, w_glu[0], bglu, w_out[0], ln2,
                               w_router, b_router, bd, True)
    y_sample = _moe(h2s, x1s, combs, wg_bf, wu_bf, wd_bf, lnf, bd).reshape(bd, 1, d)
    hs = hs.transpose(1, 2, 0, 3)

    return (y_prompt, y_sample,
            kf.reshape(1, b, s, 2 * n_heads, HEAD_DIM), vf.reshape(1, b, s, n_heads, V_DIM),
            hl[0][None], hl[1][None],
            kfs.reshape(1, bd, 1, 2 * n_heads, HEAD_DIM), vfs.reshape(1, bd, 1, n_heads, V_DIM),
            hs[0][None], hs[1][None])
```

```python
import functools
import math

import jax
import jax.numpy as jnp
from jax import lax
from jax.experimental import pallas as pl
from jax.experimental.pallas import tpu as pltpu

F32 = jnp.float32
BF16 = jnp.bfloat16

HEAD_DIM = 64
V_DIM = 128
ROT_DIM = 16
ROPE_THETA = 500000.0
S5_GROUP = 16
S5_STATE = 64
S5_CHUNK = 16
SUBLANES = 8
N_GROUPS_MOE = 4
EXP_PER_GROUP = 8
EPS = 1e-5
NEG_INF = -1e30
LANES = 128
GID_LANE = LANES - 1
MOE_CHUNK = 128
VMEM_LIMIT = 56 * 1024 * 1024

_HI = lax.Precision.HIGHEST


def _cparams(sem):
    return pltpu.CompilerParams(dimension_semantics=sem, vmem_limit_bytes=VMEM_LIMIT)


def _nt_dot(a, b, **kw):
    return lax.dot_general(a, b, (((1,), (1,)), ((), ())), preferred_element_type=F32, **kw)


def _inproj_kernel(x_ref, g_ref, w_ref, c_ref, s1_ref, s2_ref,
                   q_ref, kf_ref, vf_ref, kb_ref, vb_ref, u_ref):
    x = x_ref[...]
    ms = jnp.mean(x * x, axis=-1, keepdims=True)
    hn = x * lax.rsqrt(ms + EPS) * g_ref[...]
    proj = jnp.dot(hn, w_ref[...], preferred_element_type=F32, precision=_HI)
    w = q_ref.shape[-1]
    c = c_ref[...]
    s1 = s1_ref[...]
    s2 = s2_ref[...]
    for j in range(w // LANES):
        sl = slice(j * LANES, (j + 1) * LANES)
        zq = proj[:, j * LANES:(j + 1) * LANES]
        zk = proj[:, w + j * LANES:w + (j + 1) * LANES]
        rq = zq * c + pltpu.roll(zq, ROT_DIM // 2, 1) * s1 + pltpu.roll(zq, LANES - ROT_DIM // 2, 1) * s2
        rk = zk * c + pltpu.roll(zk, ROT_DIM // 2, 1) * s1 + pltpu.roll(zk, LANES - ROT_DIM // 2, 1) * s2
        q_ref[:, sl] = (rq * (HEAD_DIM ** -0.5)).astype(q_ref.dtype)
        kf_ref[:, sl] = rk
        kb_ref[:, sl] = rk.astype(BF16)
    v = proj[:, 2 * w:3 * w]
    vf_ref[...] = v
    vb_ref[...] = v.astype(BF16)
    u_ref[...] = proj[:, 3 * w:4 * w].astype(u_ref.dtype)


def _inproj(x2, ln_g, w_bf, rc, rs1, rs2, tm, q_dtype, u_dtype):
    t, d = x2.shape
    w = w_bf.shape[1] // 4
    nt = t // tm
    npos = rc.shape[0] // tm
    row = lambda i: (i, 0)
    pos = lambda i: (i % npos, 0)
    fixed = lambda i: (0, 0)
    outs = [jax.ShapeDtypeStruct((t, w), q_dtype), jax.ShapeDtypeStruct((t, w), F32),
            jax.ShapeDtypeStruct((t, w), F32), jax.ShapeDtypeStruct((t, w), BF16),
            jax.ShapeDtypeStruct((t, w), BF16), jax.ShapeDtypeStruct((t, w), u_dtype)]
    return pl.pallas_call(
        _inproj_kernel,
        grid=(nt,),
        in_specs=[pl.BlockSpec((tm, d), row), pl.BlockSpec((1, d), fixed),
                  pl.BlockSpec((d, 4 * w), fixed),
                  pl.BlockSpec((tm, LANES), pos), pl.BlockSpec((tm, LANES), pos),
                  pl.BlockSpec((tm, LANES), pos)],
        out_specs=[pl.BlockSpec((tm, w), row)] * 6,
        out_shape=outs,
        compiler_params=_cparams(("parallel",)),
        name="inproj",
    )(x2, ln_g, w_bf, rc, rs1, rs2)


def _inproj_prompt_kernel(x_ref, g_ref, w_ref, wkt_ref, c_ref, s1_ref, s2_ref, ct_ref, st_ref,
                          q_ref, vf_ref, vb_ref, ut_ref, ktf_ref, ktb_ref):
    x = x_ref[0]
    ms = jnp.mean(x * x, axis=-1, keepdims=True)
    hn = (x * lax.rsqrt(ms + EPS) * g_ref[...]).astype(BF16)
    proj = jnp.dot(hn, w_ref[...], preferred_element_type=F32)
    w = q_ref.shape[-1]
    ktu = _nt_dot(wkt_ref[...], hn)
    kt = ktu[:w]
    ut_ref[0] = ktu[w:].astype(ut_ref.dtype)
    c, s1, s2 = c_ref[...], s1_ref[...], s2_ref[...]
    for j in range(w // LANES):
        sl = slice(j * LANES, (j + 1) * LANES)
        zq = proj[:, sl]
        rq = zq * c + pltpu.roll(zq, ROT_DIM // 2, 1) * s1 + pltpu.roll(zq, LANES - ROT_DIM // 2, 1) * s2
        q_ref[0, :, sl] = (rq * (HEAD_DIM ** -0.5)).astype(q_ref.dtype)
    v = proj[:, w:2 * w]
    nh = w // V_DIM
    for h in range(nh):
        vf_ref[0, pl.ds(h, v.shape[0], stride=nh), :] = v[:, h * V_DIM:(h + 1) * V_DIM]
    vb_ref[0] = v.astype(BF16)
    ct, st = ct_ref[...], st_ref[...]
    half = ROT_DIM // 2
    tk = ktb_ref.shape[-1]
    for hc in range(w // HEAD_DIM):
        base = hc * HEAD_DIM
        x1, x2 = kt[base:base + half], kt[base + half:base + ROT_DIM]
        blk = jnp.concatenate([x1 * ct - x2 * st, x2 * ct + x1 * st, kt[base + ROT_DIM:base + HEAD_DIM]], axis=0)
        ktf_ref[0, base:base + HEAD_DIM, :] = blk
        for t in range(blk.shape[1] // tk):
            ktb_ref[0, t, base:base + HEAD_DIM, :] = blk[:, t * tk:(t + 1) * tk].astype(BF16)


def _inproj_prompt(x3, ln_g, w_qv_bf, wkut_bf, rc, rs1, rs2, rct, rst, tm, tk):
    b, s, d = x3.shape
    w = wkut_bf.shape[0] // 2
    half = ROT_DIM // 2
    tok = lambda bi, i: (bi, i, 0)
    pos = lambda bi, i: (i, 0)
    fixed = lambda bi, i: (0, 0)
    nh = w // V_DIM
    outs = [jax.ShapeDtypeStruct((b, s, w), BF16), jax.ShapeDtypeStruct((b, s * nh, V_DIM), F32),
            jax.ShapeDtypeStruct((b, s, w), BF16), jax.ShapeDtypeStruct((b, w, s), BF16),
            jax.ShapeDtypeStruct((b, w, s), F32), jax.ShapeDtypeStruct((b, s // tk, w, tk), BF16)]
    return pl.pallas_call(
        _inproj_prompt_kernel,
        grid=(b, s // tm),
        in_specs=[pl.BlockSpec((1, tm, d), tok), pl.BlockSpec((1, d), fixed),
                  pl.BlockSpec((d, 2 * w), fixed), pl.BlockSpec((2 * w, d), fixed),
                  pl.BlockSpec((tm, LANES), pos), pl.BlockSpec((tm, LANES), pos), pl.BlockSpec((tm, LANES), pos),
                  pl.BlockSpec((half, tm), lambda bi, i: (0, i)), pl.BlockSpec((half, tm), lambda bi, i: (0, i))],
        out_specs=[pl.BlockSpec((1, tm, w), tok), pl.BlockSpec((1, tm * nh, V_DIM), tok),
                   pl.BlockSpec((1, tm, w), tok)]
                  + [pl.BlockSpec((1, w, tm), lambda bi, i: (bi, 0, i))] * 2
                  + [pl.BlockSpec((1, tm // tk, w, tk), lambda bi, i: (bi, i, 0, 0))],
        out_shape=outs,
        compiler_params=_cparams(("parallel", "parallel")),
        name="inproj_prompt",
    )(x3, ln_g, w_qv_bf, wkut_bf, rc, rs1, rs2, rct, rst)


def _rope_tables_t(pos):
    inv = ROPE_THETA ** (-jnp.arange(0, ROT_DIM, 2, dtype=F32) / ROT_DIM)
    ang = inv[:, None] * pos.astype(F32)[None, :]
    return jnp.cos(ang), jnp.sin(ang)


def _rope_tables(pos):
    half = ROT_DIM // 2
    inv = ROPE_THETA ** (-jnp.arange(0, ROT_DIM, 2, dtype=F32) / ROT_DIM)
    ang = pos.astype(F32)[:, None] * inv[None, :]
    cos, sin = jnp.cos(ang), jnp.sin(ang)
    n = pos.shape[0]
    pad = jnp.zeros((n, HEAD_DIM - ROT_DIM), F32)
    c = jnp.concatenate([cos, cos, pad + 1.0], axis=1)
    s1 = jnp.concatenate([jnp.zeros((n, half), F32), sin, pad], axis=1)
    s2 = jnp.concatenate([-sin, jnp.zeros((n, half), F32), pad], axis=1)
    rep = LANES // HEAD_DIM
    return tuple(jnp.tile(a, (1, rep)) for a in (c, s1, s2))


def _diff_lambda(lam_ref, lambda_init):
    l = lam_ref[...]
    a = jnp.sum(l[0:1] * l[1:2], axis=-1, keepdims=True)
    b = jnp.sum(l[2:3] * l[3:4], axis=-1, keepdims=True)
    return jnp.exp(a) - jnp.exp(b) + lambda_init


def _attn_kernel(q_ref, kt_ref, v_ref, lam_ref, g_ref, o_ref, s_ref, m_ref, acc_ref, vext_ref, *, tq, lambda_init):
    nq = q_ref.shape[1] // tq
    nl = tq // LANES
    vext_ref[:, 0:V_DIM] = v_ref[0]
    ones_col = lax.broadcasted_iota(jnp.int32, (v_ref.shape[1], LANES), 1) == 0
    vext_ref[:, V_DIM:] = jnp.where(ones_col, 1.0, 0.0).astype(BF16)
    lam = _diff_lambda(lam_ref, lambda_init)
    lane = lax.broadcasted_iota(jnp.int32, (tq, LANES), 1)
    row = lax.broadcasted_iota(jnp.int32, (tq, tq), 0)
    col = lax.broadcasted_iota(jnp.int32, (tq, tq), 1)
    for qi in range(nq):
        q = q_ref[0, qi * tq:(qi + 1) * tq, :]
        zero = jnp.zeros_like(q)
        qc = (jnp.where(lane < HEAD_DIM, q, zero), jnp.where(lane >= HEAD_DIM, q, zero))
        base = qi * (qi + 1) // 2
        for j in range(qi + 1):
            kt = kt_ref[0, j]
            for c in range(2):
                s = jnp.dot(qc[c], kt, preferred_element_type=F32)
                if j == qi:
                    s = jnp.where(col <= row, s, NEG_INF)
                s_ref[c, base + j] = s
                m = s[:, 0:LANES] if j == 0 else jnp.maximum(m_ref[qi, c], s[:, 0:LANES])
                for t in range(1, nl):
                    m = jnp.maximum(m, s[:, t * LANES:(t + 1) * LANES])
                m_ref[qi, c] = m
        for c in range(2):
            m_ref[qi, c] = jnp.broadcast_to(jnp.max(m_ref[qi, c], axis=-1, keepdims=True), (tq, LANES))
        for j in range(qi + 1):
            v = vext_ref[j * tq:(j + 1) * tq, :]
            for c in range(2):
                m = m_ref[qi, c]
                ps = [jnp.exp(s_ref[c, base + j, :, t * LANES:(t + 1) * LANES] - m) for t in range(nl)]
                pv = jnp.dot(jnp.concatenate(ps, axis=1).astype(BF16), v, preferred_element_type=F32)
                if j == 0:
                    acc_ref[qi, c] = pv
                else:
                    acc_ref[qi, c] += pv
        a0, a1 = acc_ref[qi, 0], acc_ref[qi, 1]
        o = a0[:, :V_DIM] / a0[:, V_DIM:V_DIM + 1] - lam * (a1[:, :V_DIM] / a1[:, V_DIM:V_DIM + 1])
        o = o * lax.rsqrt(jnp.mean(o * o, axis=-1, keepdims=True) + EPS)
        o_ref[0, qi * tq:(qi + 1) * tq, :] = (o * g_ref[...] * (1.0 - lambda_init)).astype(o_ref.dtype)


def _attn(q, kt4, v, lam4, subln_g, n_heads, lambda_init):
    b, s, w = q.shape
    nkt, tq = kt4.shape[1], kt4.shape[3]
    seq = lambda bi, h: (bi, 0, h)
    fixed = lambda bi, h: (0, 0)
    nq = s // tq
    return pl.pallas_call(
        functools.partial(_attn_kernel, tq=tq, lambda_init=lambda_init),
        grid=(b, n_heads),
        in_specs=[pl.BlockSpec((1, s, LANES), seq),
                  pl.BlockSpec((1, nkt, LANES, tq), lambda bi, h: (bi, 0, h, 0)),
                  pl.BlockSpec((1, s, LANES), seq),
                  pl.BlockSpec((4, HEAD_DIM), fixed), pl.BlockSpec((1, V_DIM), fixed)],
        out_specs=pl.BlockSpec((1, s, LANES), seq),
        out_shape=jax.ShapeDtypeStruct((b, s, w), BF16),
        scratch_shapes=[pltpu.VMEM((2, nq * (nq + 1) // 2, tq, tq), F32), pltpu.VMEM((nq, 2, tq, LANES), F32),
                        pltpu.VMEM((nq, 2, tq, 2 * V_DIM), F32), pltpu.VMEM((s, 2 * V_DIM), BF16)],
        compiler_params=_cparams(("parallel", "parallel")),
        name="attn",
    )(q, kt4, v, lam4, subln_g)


def _split2(x):
    hi = x.astype(BF16)
    return hi, (x - hi.astype(F32)).astype(BF16)


def _head_scores(k, qcol):
    prod = k * qcol
    return jnp.sum(prod.reshape(prod.shape[0] // HEAD_DIM, HEAD_DIM, prod.shape[1]), axis=1)


def _decode_kernel(pt_ref, q_ref, kn_ref, vn_ref, lam_ref, g_ref, *refs, pp, nb, n_heads, lambda_init):
    del pt_ref
    npg = nb * pp
    k_refs, v_refs = refs[:npg], refs[npg:2 * npg]
    o_ref = refs[2 * npg]
    m_ref, l_ref, acc_ref, tmp_ref = refs[2 * npg + 1:]
    j = pl.program_id(1)
    nhc = 2 * n_heads

    @pl.when(j == 0)
    def _():
        m_ref[...] = jnp.full_like(m_ref, NEG_INF)
        l_ref[...] = jnp.zeros_like(l_ref)
        acc_ref[...] = jnp.zeros_like(acc_ref)

    row_head = lax.broadcasted_iota(jnp.int32, (nhc, V_DIM), 0) // 2
    page = k_refs[0].shape[1]
    for r in range(nb):
        qcol = q_ref[r]
        s = [_head_scores(kr[...], qcol) for kr in k_refs[r * pp:(r + 1) * pp]]
        m_old = m_ref[r, :, 0:1]
        smax = functools.reduce(jnp.maximum, s)
        m_new = jnp.maximum(m_old, jnp.max(smax, axis=-1, keepdims=True))
        alpha = jnp.exp(m_old - m_new)
        p = [jnp.exp(si - m_new) for si in s]
        psum = functools.reduce(lambda a, b: a + b, p)
        l_ref[r] = jnp.broadcast_to(alpha * l_ref[r, :, 0:1] + jnp.sum(psum, axis=-1, keepdims=True),
                                    l_ref.shape[1:])
        m_ref[r] = jnp.broadcast_to(m_new, m_ref.shape[1:])
        acc = alpha * acc_ref[r]
        pb = [pi.astype(BF16) for pi in p]
        for h in range(n_heads):
            tot = None
            for pi, vr in zip(pb, v_refs[r * pp:(r + 1) * pp]):
                vh = vr[pl.ds(h, page, stride=n_heads), :].astype(BF16)
                res = jnp.dot(pi, vh, preferred_element_type=F32)
                tot = res if tot is None else tot + res
            acc = acc + jnp.where(row_head == h, tot, 0.0)
        acc_ref[r] = acc

    @pl.when(j == pl.num_programs(1) - 1)
    def _():
        lam = _diff_lambda(lam_ref, lambda_init)
        for r in range(nb):
            s_new = _head_scores(kn_ref[r], q_ref[r])[:, 0:1]
            m_old = m_ref[r, :, 0:1]
            m_f = jnp.maximum(m_old, s_new)
            a = jnp.exp(m_old - m_f)
            pn = jnp.exp(s_new - m_f)
            l_f = a * l_ref[r, :, 0:1] + pn
            tmp_ref[r] = (a * acc_ref[r] + pn * vn_ref[r]) / l_f
            o0 = tmp_ref[r, pl.ds(0, n_heads, stride=2), :]
            o1 = tmp_ref[r, pl.ds(1, n_heads, stride=2), :]
            o = o0 - lam * o1
            o = o * lax.rsqrt(jnp.mean(o * o, axis=-1, keepdims=True) + EPS)
            o_ref[r] = o * g_ref[...] * (1.0 - lambda_init)


def _decode_attn(page_table, qcol, kn_col, vn2, lam4, subln_g, cache_kt, cache_vr, pp, nb, lambda_init):
    bd, n_pages = page_table.shape
    _, w, page = cache_kt.shape
    n_heads = cache_vr.shape[1] // page
    nhc = 2 * n_heads
    pt_flat = page_table.reshape(-1)
    fixed2 = lambda b, j, pt: (0, 0)
    perb = lambda b, j, pt: (b, 0, 0)

    def page_spec(r, i, rows, cols):
        return pl.BlockSpec((None, rows, cols),
                            lambda b, j, pt: (pt[(b * nb + r) * n_pages + j * pp + i], 0, 0))

    slots = [(r, i) for r in range(nb) for i in range(pp)]
    grid_spec = pltpu.PrefetchScalarGridSpec(
        num_scalar_prefetch=1,
        grid=(bd // nb, n_pages // pp),
        in_specs=[pl.BlockSpec((nb, w, LANES), perb), pl.BlockSpec((nb, w, LANES), perb),
                  pl.BlockSpec((nb, nhc, V_DIM), perb),
                  pl.BlockSpec((4, HEAD_DIM), fixed2), pl.BlockSpec((1, V_DIM), fixed2)]
                 + [page_spec(r, i, w, page) for r, i in slots]
                 + [page_spec(r, i, page * n_heads, V_DIM) for r, i in slots],
        out_specs=pl.BlockSpec((nb, n_heads, V_DIM), perb),
        scratch_shapes=[pltpu.VMEM((nb, nhc, LANES), F32), pltpu.VMEM((nb, nhc, LANES), F32),
                        pltpu.VMEM((nb, nhc, V_DIM), F32), pltpu.VMEM((nb, nhc, V_DIM), F32)],
    )
    npg = nb * pp
    return pl.pallas_call(
        functools.partial(_decode_kernel, pp=pp, nb=nb, n_heads=n_heads, lambda_init=lambda_init),
        grid_spec=grid_spec,
        out_shape=jax.ShapeDtypeStruct((bd, n_heads, V_DIM), F32),
        compiler_params=_cparams(("parallel", "arbitrary")),
        name="decode_attn",
    )(pt_flat, qcol, kn_col, vn2, lam4, subln_g, *([cache_kt] * npg), *([cache_vr] * npg))


def _s5prep_kernel(arow_ref, acol_ref, ldt_ref, bt_ref, ct_ref, d_ref,
                   mt_ref, w_ref, vt_ref, lam_ref):
    for gi in range(arow_ref.shape[0]):
        _s5prep_group(gi, arow_ref, acol_ref, ldt_ref, bt_ref, ct_ref, d_ref, mt_ref, w_ref, vt_ref, lam_ref)


def _s5prep_group(gi, arow_ref, acol_ref, ldt_ref, bt_ref, ct_ref, d_ref, mt_ref, w_ref, vt_ref, lam_ref):
    L, hg, p = S5_CHUNK, S5_GROUP, S5_STATE
    dt = jnp.exp(ldt_ref[gi])
    ar, ai = arow_ref[gi, 0:1], arow_ref[gi, 1:2]
    arc, aic = acol_ref[gi, :, 0:1], acol_ref[gi, :, 1:2]

    def powers(a_r, a_i, j):
        mag = jnp.exp(j * (a_r * dt))
        return mag * jnp.cos(j * (a_i * dt)), mag * jnp.sin(j * (a_i * dt))

    l1r, l1i = powers(ar, ai, 1.0)
    llr, lli = powers(ar, ai, float(L))
    lam_ref[gi] = jnp.concatenate([llr, lli, l1r, l1i, jnp.zeros((4, p), F32)], axis=0)
    den = ar * ar + ai * ai
    cr = ((l1r - 1.0) * ar + l1i * ai) / den
    ci = (l1i * ar - (l1r - 1.0) * ai) / den
    btr, bti = bt_ref[gi, 0], bt_ref[gi, 1]
    bbr = btr * cr - bti * ci
    bbi = btr * ci + bti * cr
    jrow = lax.broadcasted_iota(jnp.int32, (L, p), 0).astype(F32)
    pr, pi = powers(ar, ai, jrow)
    for s in range(L):
        qr, qi = pr[L - 1 - s:L - s], pi[L - 1 - s:L - s]
        w_ref[gi, 0, s * hg:(s + 1) * hg, :] = bbr * qr - bbi * qi
        w_ref[gi, 1, s * hg:(s + 1) * hg, :] = bbr * qi + bbi * qr
    jl = (lax.broadcasted_iota(jnp.int32, (p, L * hg), 1) // hg).astype(F32)
    lpr, lpi = powers(arc, aic, jl)
    ctr, cti = ct_ref[gi, 0], ct_ref[gi, 1]
    cjr = ctr * lpr - cti * lpi
    cji = ctr * lpi + cti * lpr
    c1r, c1i = powers(arc, aic, 1.0)
    vt_ref[gi, 0] = cjr * c1r - cji * c1i
    vt_ref[gi, 1] = -(cjr * c1i + cji * c1r)
    kt = (jnp.dot(bbr, cjr, preferred_element_type=F32, precision=_HI)
          - jnp.dot(bbi, cji, preferred_element_type=F32, precision=_HI))
    n = L * hg
    ri = lax.broadcasted_iota(jnp.int32, (n, n), 0)
    cidx = lax.broadcasted_iota(jnp.int32, (n, n), 1)
    dtile = d_ref[gi]
    for s in range(L):
        shift = (cidx - ri == s * hg).astype(F32)
        blk = jnp.dot(kt, shift, preferred_element_type=F32, precision=_HI)
        rr = lax.broadcasted_iota(jnp.int32, (hg, n), 0) + s * hg
        cc = lax.broadcasted_iota(jnp.int32, (hg, n), 1)
        mt_ref[gi, s * hg:(s + 1) * hg, :] = blk + jnp.where(rr == cc, dtile, 0.0)


def _s5prep(arow, acol, ldt, bt, ct, dt_tiled):
    g = arow.shape[0]
    L, hg, p = S5_CHUNK, S5_GROUP, S5_STATE
    n = L * hg
    i3 = lambda i: (i, 0, 0)
    i4 = lambda i: (i, 0, 0, 0)
    gb = 4 if g % 4 == 0 else 1
    return pl.pallas_call(
        _s5prep_kernel,
        grid=(g // gb,),
        in_specs=[pl.BlockSpec((gb, 2, p), i3), pl.BlockSpec((gb, p, 2), i3), pl.BlockSpec((gb, 1, 1), i3),
                  pl.BlockSpec((gb, 2, hg, p), i4), pl.BlockSpec((gb, 2, p, n), i4),
                  pl.BlockSpec((gb, 1, n), i3)],
        out_specs=[pl.BlockSpec((gb, n, n), i3), pl.BlockSpec((gb, 2, n, p), i4),
                   pl.BlockSpec((gb, 2, p, n), i4), pl.BlockSpec((gb, SUBLANES, p), i3)],
        out_shape=[jax.ShapeDtypeStruct((g, n, n), F32), jax.ShapeDtypeStruct((g, 2, n, p), F32),
                   jax.ShapeDtypeStruct((g, 2, p, n), F32), jax.ShapeDtypeStruct((g, SUBLANES, p), F32)],
        compiler_params=_cparams(("parallel",)),
        name="s5prep",
    )(arow, acol, ldt, bt, ct, dt_tiled)


def _s5in_kernel(ut_ref, perm_ref, o_ref):
    g, nc, n = o_ref.shape[1:]
    hg, L = S5_GROUP, S5_CHUNK
    sh = perm_ref.shape[0]
    nseg, ncs = ut_ref.shape[2] // sh, sh // L
    ups = [jnp.dot(ut_ref[0, :, k * sh:(k + 1) * sh], perm_ref[...], preferred_element_type=F32)
           for k in range(nseg)]
    for gi in range(g):
        rows = slice(gi * hg, (gi + 1) * hg)
        ugt = jnp.concatenate(
            [jnp.concatenate([up[rows, s * ncs:(s + 1) * ncs] for up in ups], axis=1) for s in range(L)], axis=0)
        o_ref[0, gi] = ugt.T.astype(o_ref.dtype)


def _s5in(ut, perm):
    b, w, s = ut.shape
    g, nc, n = w // S5_GROUP, s // S5_CHUNK, S5_CHUNK * S5_GROUP
    return pl.pallas_call(
        _s5in_kernel,
        grid=(b,),
        in_specs=[pl.BlockSpec((1, w, s), lambda i: (i, 0, 0)),
                  pl.BlockSpec(perm.shape, lambda i: (0, 0), pipeline_mode=pl.Buffered(1))],
        out_specs=pl.BlockSpec((1, g, None, nc, n), lambda i: (i // SUBLANES, 0, i % SUBLANES, 0, 0)),
        out_shape=jax.ShapeDtypeStruct((b // SUBLANES, g, SUBLANES, nc, n), BF16),
        compiler_params=_cparams(("parallel",)),
        name="s5in",
    )(ut, perm)


def _s5out_kernel(y_ref, permt_ref, o_ref, ypt_ref):
    g, nc, n = y_ref.shape[1:]
    hg, L = S5_GROUP, S5_CHUNK
    nseg, sh = ypt_ref.shape[0], ypt_ref.shape[2]
    ncs = sh // L
    for gi in range(g):
        ygt = y_ref[0, gi].astype(F32).T
        for t in range(L):
            blk = ygt[t * hg:(t + 1) * hg, :].astype(BF16)
            for k in range(nseg):
                ypt_ref[k, gi * hg:(gi + 1) * hg, t * ncs:(t + 1) * ncs] = blk[:, k * ncs:(k + 1) * ncs]
    yt = jnp.concatenate([jnp.dot(ypt_ref[k], permt_ref[...], preferred_element_type=F32)
                          for k in range(nseg)], axis=1)
    o_ref[0] = yt.T.astype(o_ref.dtype)


def _s5out(y5, permt):
    nbh, g, _, nc, n = y5.shape
    b, s, w = nbh * SUBLANES, nc * S5_CHUNK, g * S5_GROUP
    return pl.pallas_call(
        _s5out_kernel,
        grid=(b,),
        in_specs=[pl.BlockSpec((1, g, None, nc, n), lambda i: (i // SUBLANES, 0, i % SUBLANES, 0, 0)),
                  pl.BlockSpec(permt.shape, lambda i: (0, 0), pipeline_mode=pl.Buffered(1))],
        out_specs=pl.BlockSpec((1, s, w), lambda i: (i, 0, 0)),
        out_shape=jax.ShapeDtypeStruct((b, s, w), BF16),
        scratch_shapes=[pltpu.VMEM((s // permt.shape[0], w, permt.shape[0]), BF16)],
        compiler_params=_cparams(("parallel",)),
        name="s5out",
    )(y5, permt)


def _s5chunk_kernel(u_ref, mt_ref, w_ref, wsw_ref, vt_ref, lam_ref, y_ref, hl_ref, t1_ref, t2_ref, hs_ref):
    u = u_ref[0, 0]
    nc = u.shape[0] // SUBLANES
    p = S5_STATE
    t1_ref[...] = jnp.dot(u, w_ref[0].astype(BF16), preferred_element_type=F32)
    t2_ref[...] = jnp.dot(u, wsw_ref[0].astype(BF16), preferred_element_type=F32)
    lane = lax.broadcasted_iota(jnp.int32, (SUBLANES, 2 * p), 1)
    la = jnp.broadcast_to(lam_ref[0, 0:1], (SUBLANES, 2 * p))
    li2 = jnp.broadcast_to(lam_ref[0, 1:2], (SUBLANES, 2 * p))
    lb = jnp.where(lane < p, -li2, li2)

    def step(c, h):
        a, b = h
        s1 = t1_ref[pl.ds(c, SUBLANES, stride=nc), :]
        s2 = t2_ref[pl.ds(c, SUBLANES, stride=nc), :]
        hs_ref[pl.ds(pl.multiple_of(c * SUBLANES, SUBLANES), SUBLANES), :] = a
        return a * la + b * lb + s1, b * la - a * lb + s2

    z = jnp.zeros((SUBLANES, 2 * p), F32)
    a, _ = lax.fori_loop(0, nc, step, (z, z), unroll=8)
    hl_ref[0, 0] = a
    mt = mt_ref[0].astype(BF16)
    vt = vt_ref[0].astype(BF16)
    for bi in range(SUBLANES):
        rows = slice(bi * nc, (bi + 1) * nc)
        hb = hs_ref[pl.ds(bi, nc, stride=SUBLANES), :].astype(BF16)
        y = jnp.dot(u[rows], mt, preferred_element_type=F32) + jnp.dot(hb, vt, preferred_element_type=F32)
        y_ref[0, 0, rows, :] = y.astype(y_ref.dtype)


def _s5chunk(u4, mt, wcat, wsw, vtcat, lam2):
    nbh, g, rows, n = u4.shape
    p2 = 2 * S5_STATE
    um = lambda b, gi: (b, gi, 0, 0)
    g3 = lambda b, gi: (gi, 0, 0)
    return pl.pallas_call(
        _s5chunk_kernel,
        grid=(nbh, g),
        in_specs=[pl.BlockSpec((1, 1, rows, n), um), pl.BlockSpec((1, n, n), g3),
                  pl.BlockSpec((1, n, p2), g3), pl.BlockSpec((1, n, p2), g3), pl.BlockSpec((1, p2, n), g3),
                  pl.BlockSpec((1, SUBLANES, p2), g3)],
        out_specs=[pl.BlockSpec((1, 1, rows, n), um), pl.BlockSpec((1, 1, SUBLANES, p2), um)],
        out_shape=[jax.ShapeDtypeStruct(u4.shape, BF16),
                   jax.ShapeDtypeStruct((nbh, g, SUBLANES, p2), F32)],
        scratch_shapes=[pltpu.VMEM((rows, p2), F32), pltpu.VMEM((rows, p2), F32), pltpu.VMEM((rows, p2), F32)],
        compiler_params=_cparams(("parallel", "parallel")),
        name="s5chunk",
    )(u4, mt, wcat, wsw, vtcat, lam2)


def _s5step_kernel(u_ref, h0_ref, bb_ref, lam_ref, c_ref, d_ref, y_ref, h_ref):
    u = u_ref[0]
    h0r, h0i = h0_ref[0, 0], h0_ref[0, 1]
    l1r, l1i = lam_ref[0, 2:3], lam_ref[0, 3:4]
    bur = jnp.dot(u, bb_ref[0, 0], preferred_element_type=F32, precision=_HI)
    bui = jnp.dot(u, bb_ref[0, 1], preferred_element_type=F32, precision=_HI)
    hr = l1r * h0r - l1i * h0i + bur
    hi = l1r * h0i + l1i * h0r + bui
    h_ref[0, 0] = hr
    h_ref[0, 1] = hi
    y = _nt_dot(hr, c_ref[0, 0], precision=_HI) - _nt_dot(hi, c_ref[0, 1], precision=_HI)
    y_ref[0] = y + d_ref[0] * u


def _s5step(u3, h0, bb, lam, c2, d3):
    g, bd, hg = u3.shape
    p = S5_STATE
    i3 = lambda i: (i, 0, 0)
    i4 = lambda i: (i, 0, 0, 0)
    return pl.pallas_call(
        _s5step_kernel,
        grid=(g,),
        in_specs=[pl.BlockSpec((1, bd, hg), i3), pl.BlockSpec((1, 2, bd, p), i4),
                  pl.BlockSpec((1, 2, hg, p), i4), pl.BlockSpec((1, SUBLANES, p), i3),
                  pl.BlockSpec((1, 2, hg, p), i4), pl.BlockSpec((1, 1, hg), i3)],
        out_specs=[pl.BlockSpec((1, bd, hg), i3), pl.BlockSpec((1, 2, bd, p), i4)],
        out_shape=[jax.ShapeDtypeStruct((g, bd, hg), F32), jax.ShapeDtypeStruct((g, 2, bd, p), F32)],
        compiler_params=_cparams(("parallel",)),
        name="s5step",
    )(u3, h0, bb, lam, c2, d3)


def _postmix_kernel(x_ref, a_ref, y_ref, wglu_ref, bglu_ref, wout_ref, g2_ref, wr_ref, wrf_ref, br_ref,
                    x1_ref, h2_ref, comb_ref, *, precise, parts):
    th = x_ref.shape[0] // parts
    for part in range(parts):
        rows = slice(part * th, (part + 1) * th)
        _postmix_rows(x_ref, a_ref, y_ref, wglu_ref, bglu_ref, wout_ref, g2_ref, wr_ref, wrf_ref, br_ref,
                      x1_ref, h2_ref, comb_ref, rows, precise)


def _postmix_rows(x_ref, a_ref, y_ref, wglu_ref, bglu_ref, wout_ref, g2_ref, wr_ref, wrf_ref, br_ref,
                  x1_ref, h2_ref, comb_ref, rows, precise):
    wa = a_ref.shape[-1]

    def mm(act, w):
        if precise:
            return jnp.dot(act, w, preferred_element_type=F32, precision=_HI)
        return jnp.dot(act.astype(BF16), w, preferred_element_type=F32)

    y = jax.nn.gelu(y_ref[rows, :].astype(F32))
    z = mm(y, wglu_ref[...]) + bglu_ref[...]
    s5o = y * jax.nn.sigmoid(z)
    mix = mm(a_ref[rows, :].astype(F32) if precise else a_ref[rows, :], wout_ref[0:wa, :])
    mix += mm(s5o, wout_ref[wa:, :])
    x1 = x_ref[rows, :] + mix
    x1_ref[rows, :] = x1
    h2 = x1 * lax.rsqrt(jnp.mean(x1 * x1, axis=-1, keepdims=True) + EPS) * g2_ref[...]
    h2_ref[rows, :] = h2.astype(h2_ref.dtype)
    if precise:
        logits = jnp.dot(h2, wrf_ref[...], preferred_element_type=F32, precision=_HI)
    else:
        hi, lo = _split2(h2)
        hh = jnp.dot(hi, wr_ref[...], preferred_element_type=F32)
        logits = hh[:, :LANES] + hh[:, LANES:] + jnp.dot(lo, wr_ref[:, :LANES], preferred_element_type=F32)
    logits = logits + br_ref[...]
    ne = N_GROUPS_MOE * EXP_PER_GROUP
    lane = lax.broadcasted_iota(jnp.int32, logits.shape, 1).astype(F32)
    big = jnp.float32(1 << 20)
    gmask = (lane >= ne) & (lane < ne + N_GROUPS_MOE)
    gl = jnp.where(gmask, logits, NEG_INF)
    gmax = jnp.max(gl, axis=-1, keepdims=True)
    gidx = jnp.min(jnp.where(gl == gmax, lane, big), axis=-1, keepdims=True) - ne
    g_w = 1.0 / jnp.sum(jnp.where(gmask, jnp.exp(logits - gmax), 0.0), axis=-1, keepdims=True)
    lo = gidx * EXP_PER_GROUP
    el = jnp.where((lane >= lo) & (lane < lo + EXP_PER_GROUP), logits, NEG_INF)
    e1 = jnp.max(el, axis=-1, keepdims=True)
    i1 = jnp.min(jnp.where(el == e1, lane, big), axis=-1, keepdims=True)
    el2 = jnp.where(lane == i1, NEG_INF, el)
    e2 = jnp.max(el2, axis=-1, keepdims=True)
    i2 = jnp.min(jnp.where(el2 == e2, lane, big), axis=-1, keepdims=True)
    r = jnp.exp(e2 - e1)
    w1 = g_w / (1.0 + r)
    w2 = g_w * r / (1.0 + r)
    comb_ref[rows, :] = (jnp.where(lane == i1, w1, 0.0) + jnp.where(lane == i2, w2, 0.0)
                         + jnp.where(lane == GID_LANE, gidx, 0.0))


def _postmix(x2, attn, ys5, wglu, bglu, wout, ln2_g, w_router, b_router, tm, precise):
    t, d = x2.shape
    wa = attn.shape[1]
    ws = ys5.shape[1]
    row = lambda i: (i, 0)
    fixed = lambda i: (0, 0)
    wr_hi, wr_lo = _split2(w_router)
    wr_hl = jnp.concatenate([wr_hi, wr_lo], axis=1)
    parts = 2 if tm % 512 == 0 else 1
    return pl.pallas_call(
        functools.partial(_postmix_kernel, precise=precise, parts=parts),
        grid=(t // tm,),
        in_specs=[pl.BlockSpec((tm, d), row), pl.BlockSpec((tm, wa), row), pl.BlockSpec((tm, ws), row),
                  pl.BlockSpec((ws, ws), fixed), pl.BlockSpec((1, ws), fixed),
                  pl.BlockSpec((wa + ws, d), fixed), pl.BlockSpec((1, d), fixed),
                  pl.BlockSpec((d, 2 * LANES), fixed), pl.BlockSpec((d, LANES), fixed),
                  pl.BlockSpec((1, LANES), fixed)],
        out_specs=[pl.BlockSpec((tm, d), row), pl.BlockSpec((tm, d), row), pl.BlockSpec((tm, LANES), row)],
        out_shape=[jax.ShapeDtypeStruct((t, d), F32), jax.ShapeDtypeStruct((t, d), BF16),
                   jax.ShapeDtypeStruct((t, LANES), F32)],
        compiler_params=_cparams(("parallel",)),
        name="postmix",
    )(x2, attn, ys5, wglu, bglu, wout, ln2_g, wr_hl, w_router, b_router)


def _moe_kernel(h_ref, x1_ref, comb_ref, wg_ref, wu_ref, wd_ref, gf_ref, o_ref, acc_ref):
    e = pl.program_id(1)

    @pl.when(e == 0)
    def _():
        acc_ref[...] = jnp.zeros_like(acc_ref)

    h = h_ref[...]
    he = jax.nn.silu(jnp.dot(h, wg_ref[0], preferred_element_type=F32))
    he = he * jnp.dot(h, wu_ref[0], preferred_element_type=F32)
    comb = comb_ref[...]
    lane = lax.broadcasted_iota(jnp.int32, comb.shape, 1)
    ce = jnp.sum(jnp.where(lane == e, comb, 0.0), axis=-1, keepdims=True)
    acc_ref[...] += jnp.dot((he * ce).astype(BF16), wd_ref[0], preferred_element_type=F32)

    @pl.when(e == pl.num_programs(1) - 1)
    def _():
        x2 = x1_ref[...] + acc_ref[...]
        o_ref[...] = x2 * lax.rsqrt(jnp.mean(x2 * x2, axis=-1, keepdims=True) + EPS) * gf_ref[...]


def _moe(h2, x1, comb, wg_bf, wu_bf, wd_bf, lnf_g, tm):
    t, d = x1.shape
    ne, _, de = wg_bf.shape
    row = lambda i, e: (i, 0)
    return pl.pallas_call(
        _moe_kernel,
        grid=(t // tm, ne),
        in_specs=[pl.BlockSpec((tm, d), row), pl.BlockSpec((tm, d), row), pl.BlockSpec((tm, LANES), row),
                  pl.BlockSpec((1, d, de), lambda i, e: (e, 0, 0)),
                  pl.BlockSpec((1, d, de), lambda i, e: (e, 0, 0)),
                  pl.BlockSpec((1, de, d), lambda i, e: (e, 0, 0)),
                  pl.BlockSpec((1, d), lambda i, e: (0, 0))],
        out_specs=pl.BlockSpec((tm, d), row),
        out_shape=jax.ShapeDtypeStruct((t, d), F32),
        scratch_shapes=[pltpu.VMEM((tm, d), F32)],
        compiler_params=_cparams(("parallel", "arbitrary")),
        name="moe",
    )(h2, x1, comb, wg_bf, wu_bf, wd_bf, lnf_g)


def _moe_grouped_kernel(h_ref, x1_ref, comb_ref, ut_ref, lt_ref, wg_ref, wu_ref, wd_ref, gf_ref, o_ref,
                        rrow_ref, rcol_ref, chi_ref, clo_ref, cnt_ref):
    g = pl.program_id(1)
    tm, d = h_ref.shape
    gf32 = g.astype(F32)

    @pl.when(g == 0)
    def _():
        comb = comb_ref[...]
        combt = comb.T
        gidc = comb[:, GID_LANE:GID_LANE + 1]
        gidr = combt[GID_LANE:GID_LANE + 1, :]
        lane = lax.broadcasted_iota(jnp.int32, comb.shape, 1).astype(F32)
        sub = lax.broadcasted_iota(jnp.int32, (SUBLANES, tm), 0).astype(F32)
        ohc = lane == gidc
        ohr = sub == gidr
        cntc = jnp.dot(lt_ref[...], jnp.where(ohc, 1.0, 0.0).astype(BF16), preferred_element_type=F32)
        cntr = jnp.dot(jnp.where(ohr, 1.0, 0.0).astype(BF16), ut_ref[...], preferred_element_type=F32)
        rankc = jnp.sum(jnp.where(ohc, cntc, 0.0), axis=-1, keepdims=True) - 1.0
        rankr = jnp.sum(jnp.where(ohr, cntr, 0.0), axis=0, keepdims=True) - 1.0
        rcol_ref[...] = jnp.broadcast_to(rankc, rcol_ref.shape)
        rrow_ref[...] = jnp.concatenate([rankr, gidr, jnp.zeros((SUBLANES - 2, tm), F32)], axis=0)
        chi, clo = _split2(combt)
        chi_ref[...] = chi
        clo_ref[...] = clo
        for r in range(N_GROUPS_MOE):
            cnt_ref[r] = jnp.max(cntr[r:r + 1, :]).astype(jnp.int32)
        o_ref[...] = jnp.zeros_like(o_ref)

    posr = jnp.where(rrow_ref[1:2, :] == gf32, rrow_ref[0:1, :], -1.0)
    posc = jnp.where(comb_ref[:, GID_LANE:GID_LANE + 1] == gf32, rcol_ref[:, 0:1], -1.0)
    n_rows = cnt_ref[g]
    big = 2 * MOE_CHUNK
    mid = big + MOE_CHUNK // 2
    use_mid = jnp.logical_and(n_rows > big, n_rows <= mid)
    n_big = jnp.where(use_mid, 0, n_rows // big)
    n_mid = jnp.where(use_mid, 1, 0)
    n_small = jnp.where(use_mid, 0, (n_rows - n_big * big + (MOE_CHUNK - 1)) // MOE_CHUNK)

    def chunk(row0, ch):
        base = row0.astype(F32)
        rid = lax.broadcasted_iota(jnp.int32, (ch, tm), 0).astype(F32) + base
        cid = lax.broadcasted_iota(jnp.int32, (tm, ch), 1).astype(F32) + base
        sel = jnp.where(posr == rid, 1.0, 0.0).astype(BF16)
        selt = jnp.where(posc == cid, 1.0, 0.0).astype(BF16)
        xg = jnp.dot(sel, h_ref[...], preferred_element_type=F32).astype(BF16)
        cg = _nt_dot(sel, chi_ref[...]) + _nt_dot(sel, clo_ref[...])
        lane = lax.broadcasted_iota(jnp.int32, cg.shape, 1).astype(F32)
        z = jnp.zeros((ch, d), F32)
        for e in range(EXP_PER_GROUP):
            he = jax.nn.silu(jnp.dot(xg, wg_ref[e], preferred_element_type=F32))
            he = he * jnp.dot(xg, wu_ref[e], preferred_element_type=F32)
            ce = jnp.sum(jnp.where(lane == gf32 * EXP_PER_GROUP + e, cg, 0.0), axis=-1, keepdims=True)
            z = z + jnp.dot((he * ce).astype(BF16), wd_ref[e], preferred_element_type=F32)
        o_ref[...] += jnp.dot(selt, z.astype(BF16), preferred_element_type=F32)

    def big_body(k, carry):
        chunk(k * big, big)
        return carry

    def mid_body(k, carry):
        chunk(k * mid, mid)
        return carry

    def small_body(k, carry):
        chunk(n_big * big + k * MOE_CHUNK, MOE_CHUNK)
        return carry

    lax.fori_loop(0, n_big, big_body, 0)
    lax.fori_loop(0, n_mid, mid_body, 0)
    lax.fori_loop(0, n_small, small_body, 0)

    @pl.when(g == pl.num_programs(1) - 1)
    def _():
        x2 = x1_ref[...] + o_ref[...]
        o_ref[...] = x2 * lax.rsqrt(jnp.mean(x2 * x2, axis=-1, keepdims=True) + EPS) * gf_ref[...]


def _moe_grouped(h2, x1, comb, wg_bf, wu_bf, wd_bf, lnf_g, tm):
    t, d = x1.shape
    ne, _, de = wg_bf.shape
    ng = ne // EXP_PER_GROUP
    tri = jnp.arange(tm)[:, None] <= jnp.arange(tm)[None, :]
    ut = tri.astype(BF16)
    lt = tri.T.astype(BF16)
    once = pl.Buffered(1)
    tile = lambda i, g: (i, 0)
    fixed = lambda i, g: (0, 0)
    wmap = lambda i, g: (g, 0, 0)
    return pl.pallas_call(
        _moe_grouped_kernel,
        grid=(t // tm, ng),
        in_specs=[pl.BlockSpec((tm, d), tile), pl.BlockSpec((tm, d), tile),
                  pl.BlockSpec((tm, LANES), tile),
                  pl.BlockSpec((tm, tm), fixed, pipeline_mode=once),
                  pl.BlockSpec((tm, tm), fixed, pipeline_mode=once),
                  pl.BlockSpec((EXP_PER_GROUP, d, de), wmap),
                  pl.BlockSpec((EXP_PER_GROUP, d, de), wmap),
                  pl.BlockSpec((EXP_PER_GROUP, de, d), wmap),
                  pl.BlockSpec((1, d), fixed)],
        out_specs=pl.BlockSpec((tm, d), tile),
        out_shape=jax.ShapeDtypeStruct((t, d), F32),
        scratch_shapes=[pltpu.VMEM((SUBLANES, tm), F32), pltpu.VMEM((tm, LANES), F32),
                        pltpu.VMEM((LANES, tm), BF16), pltpu.VMEM((LANES, tm), BF16),
                        pltpu.SMEM((N_GROUPS_MOE,), jnp.int32)],
        compiler_params=_cparams(("parallel", "arbitrary")),
        name="moe_grouped",
    )(h2, x1, comb, ut, lt, wg_bf, wu_bf, wd_bf, lnf_g)


def _tile(n, want):
    t = min(n, want)
    while n % t:
        t //= 2
    return t


def kernel(x_prompt, x_sample, cache_k, cache_v, state_s5_re, state_s5_im, page_table, ln1_g, w_in, lambda_q1, lambda_k1, lambda_q2, lambda_k2, subln_g, s5_a_re, s5_a_im, s5_log_dt, s5_b_re, s5_b_im, s5_c_re, s5_c_im, s5_d, w_glu, b_glu, w_out, ln2_g, w_router_group, b_router_group, w_router_expert, b_router_expert, w_gate, w_up, w_down, ln_f_g):
    depth = ln1_g.shape[0]
    assert depth == 1, "single-layer step"
    b, s, d = x_prompt.shape
    bd, ds, _ = x_sample.shape
    assert ds == 1
    n_pages = page_table.shape[1]
    page = cache_k.shape[2]
    past_len = n_pages * page
    wq = w_in.shape[2] // 4
    n_heads = wq // V_DIM
    g = s5_a_re.shape[1]
    p = S5_STATE
    hg = S5_GROUP
    L = S5_CHUNK
    n = L * hg
    assert b % SUBLANES == 0 and s % L == 0 and g * hg == wq
    lambda_init = 0.8 - 0.6 * math.exp(-0.3 * 0)

    w_in_bf = w_in[0].astype(BF16)
    wglu_bf = w_glu[0].astype(BF16)
    wout_bf = w_out[0].astype(BF16)
    wg_bf, wu_bf, wd_bf = (w[0].astype(BF16) for w in (w_gate, w_up, w_down))
    ne = w_gate.shape[1]
    w_router = jnp.concatenate(
        [jnp.transpose(w_router_expert[0], (1, 0, 2)).reshape(d, ne), w_router_group[0],
         jnp.zeros((d, LANES - ne - N_GROUPS_MOE), F32)], axis=1)
    b_router = jnp.concatenate([b_router_expert[0].reshape(ne), b_router_group[0],
                                jnp.zeros((LANES - ne - N_GROUPS_MOE,), F32)])[None, :]
    lam4 = jnp.concatenate([lambda_q1, lambda_k1, lambda_q2, lambda_k2], axis=0)
    ln1 = ln1_g[0][None, :]
    ln2 = ln2_g[0][None, :]
    lnf = ln_f_g[None, :]
    subg = subln_g[0][None, :]
    bglu = b_glu[0][None, :]

    arow = jnp.stack([s5_a_re[0], s5_a_im[0]], axis=1)
    acol = jnp.stack([s5_a_re[0], s5_a_im[0]], axis=2)
    ldt = s5_log_dt[0].reshape(g, 1, 1)
    bt = jnp.stack([jnp.swapaxes(s5_b_re[0], 1, 2), jnp.swapaxes(s5_b_im[0], 1, 2)], axis=1)
    ct = jnp.stack([jnp.tile(jnp.swapaxes(s5_c_re[0], 1, 2), (1, 1, L)),
                    jnp.tile(jnp.swapaxes(s5_c_im[0], 1, 2), (1, 1, L))], axis=1)
    d_g = s5_d[0].reshape(g, 1, hg)
    mt, w_s5, vt, lam_s5 = _s5prep(arow, acol, ldt, bt, ct, jnp.tile(d_g, (1, 1, L)))

    tm = _tile(b * s, 512)
    x2 = x_prompt.reshape(b * s, d)
    pos_p = jnp.arange(s, dtype=jnp.int32)
    rc, rs1, rs2 = _rope_tables(pos_p)
    rct, rst = _rope_tables_t(pos_p)
    w_qv_bf = jnp.concatenate([w_in_bf[:, :wq], w_in_bf[:, 2 * wq:3 * wq]], axis=1)
    wkut_bf = jnp.concatenate([w_in_bf[:, wq:2 * wq], w_in_bf[:, 3 * wq:]], axis=1).T
    tq = _tile(s, 256)
    q, vf, vb, ut, ktf, ktb = _inproj_prompt(x_prompt, ln1, w_qv_bf, wkut_bf, rc, rs1, rs2, rct, rst,
                                             _tile(s, tm), tq)
    attn = _attn(q, ktb, vb, lam4, subg, n_heads, lambda_init)
    kf = jnp.transpose(ktf.reshape(b, 2 * n_heads, HEAD_DIM, s), (0, 3, 1, 2))
    nbh, nc = b // SUBLANES, s // L
    seg = s // 2 if (s // 2) % (L * LANES // 2) == 0 else s
    tok = jnp.arange(seg)
    perm = (((tok % L) * (seg // L) + tok // L)[:, None] == tok[None, :]).astype(BF16)
    u5 = _s5in(ut, perm)
    wcat = jnp.concatenate([w_s5[:, 0], w_s5[:, 1]], axis=-1)
    wsw = jnp.concatenate([w_s5[:, 1], w_s5[:, 0]], axis=-1)
    vtcat = jnp.concatenate([vt[:, 0], vt[:, 1]], axis=1)
    lam2 = jnp.concatenate([lam_s5, lam_s5], axis=-1)
    y5, hl = _s5chunk(u5.reshape(nbh, g, SUBLANES * nc, n), mt, wcat, wsw, vtcat, lam2)
    ys5 = _s5out(y5.reshape(nbh, g, SUBLANES, nc, n), perm.T).reshape(b * s, wq)
    hl = hl.transpose(0, 2, 1, 3).reshape(b, g, 2, p).transpose(2, 0, 1, 3)
    x1, h2, comb = _postmix(x2, attn.reshape(b * s, wq), ys5, wglu_bf, bglu, wout_bf, ln2,
                            w_router, b_router, tm, False)
    y_prompt = _moe_grouped(h2, x1, comb, wg_bf, wu_bf, wd_bf, lnf, _tile(b * s, 1024)).reshape(b, s, d)

    xs2 = x_sample.reshape(bd, d)
    pos_s = jnp.full((bd,), past_len, jnp.int32)
    sc, ss1, ss2 = _rope_tables(pos_s)
    qs, kfs, vfs, _, _, us = _inproj(xs2, ln1, w_in[0], sc, ss1, ss2, bd, F32, F32)
    cache_kt = jnp.transpose(cache_k[0], (0, 2, 3, 1)).reshape(-1, wq, page)
    cache_vr = cache_v[0].reshape(-1, page * n_heads, V_DIM)
    qcol = jnp.broadcast_to(qs[:, :, None], (bd, wq, LANES))
    kn_col = jnp.broadcast_to(kfs[:, :, None], (bd, wq, LANES))
    vn2 = jnp.repeat(vfs.reshape(bd, n_heads, V_DIM), 2, axis=1)
    pp = _tile(n_pages, 8)
    nb_dec = 2 if bd % 2 == 0 else 1
    attn_s = _decode_attn(page_table, qcol, kn_col, vn2, lam4, subg, cache_kt, cache_vr, pp, nb_dec,
                          lambda_init)
    u3 = us.reshape(bd, g, hg).transpose(1, 0, 2)
    h0 = jnp.stack([state_s5_re[0], state_s5_im[0]]).transpose(2, 0, 1, 3)
    bb = w_s5[:, :, (L - 1) * hg:, :]
    c2 = jnp.stack([s5_c_re[0], s5_c_im[0]], axis=1)
    ys3, hs = _s5step(u3, h0, bb, lam_s5, c2, d_g)
    ys_s5 = ys3.transpose(1, 0, 2).reshape(bd, wq)
    x1s, h2s, combs = _postmix(xs2, attn_s.reshape(bd, wq), ys_s5, w_glu[0], bglu, w_out[0], ln2,
                               w_router, b_router, bd, True)
    y_sample = _moe(h2s, x1s, combs, wg_bf, wu_bf, wd_bf, lnf, bd).reshape(bd, 1, d)
    hs = hs.transpose(1, 2, 0, 3)

    return (y_prompt, y_sample,
            kf.reshape(1, b, s, 2 * n_heads, HEAD_DIM), vf.reshape(1, b, s, n_heads, V_DIM),
            hl[0][None], hl[1][None],
            kfs.reshape(1, bd, 1, 2 * n_heads, HEAD_DIM), vfs.reshape(1, bd, 1, n_heads, V_DIM),
            hs[0][None], hs[1][None])
```

```python
import functools
import math

import jax
import jax.numpy as jnp
from jax import lax
from jax.experimental import pallas as pl
from jax.experimental.pallas import tpu as pltpu

F32 = jnp.float32
BF16 = jnp.bfloat16

HEAD_DIM = 64
V_DIM = 128
ROT_DIM = 16
ROPE_THETA = 500000.0
S5_GROUP = 16
S5_STATE = 64
S5_CHUNK = 16
SUBLANES = 8
N_GROUPS_MOE = 4
EXP_PER_GROUP = 8
EPS = 1e-5
NEG_INF = -1e30
LANES = 128
GID_LANE = LANES - 1
MOE_CHUNK = 128
VMEM_LIMIT = 56 * 1024 * 1024

_HI = lax.Precision.HIGHEST


def _cparams(sem):
    return pltpu.CompilerParams(dimension_semantics=sem, vmem_limit_bytes=VMEM_LIMIT)


def _nt_dot(a, b, **kw):
    return lax.dot_general(a, b, (((1,), (1,)), ((), ())), preferred_element_type=F32, **kw)


def _inproj_kernel(x_ref, g_ref, w_ref, c_ref, s1_ref, s2_ref,
                   q_ref, kf_ref, vf_ref, kb_ref, vb_ref, u_ref):
    x = x_ref[...]
    ms = jnp.mean(x * x, axis=-1, keepdims=True)
    hn = x * lax.rsqrt(ms + EPS) * g_ref[...]
    proj = jnp.dot(hn, w_ref[...], preferred_element_type=F32, precision=_HI)
    w = q_ref.shape[-1]
    c = c_ref[...]
    s1 = s1_ref[...]
    s2 = s2_ref[...]
    for j in range(w // LANES):
        sl = slice(j * LANES, (j + 1) * LANES)
        zq = proj[:, j * LANES:(j + 1) * LANES]
        zk = proj[:, w + j * LANES:w + (j + 1) * LANES]
        rq = zq * c + pltpu.roll(zq, ROT_DIM // 2, 1) * s1 + pltpu.roll(zq, LANES - ROT_DIM // 2, 1) * s2
        rk = zk * c + pltpu.roll(zk, ROT_DIM // 2, 1) * s1 + pltpu.roll(zk, LANES - ROT_DIM // 2, 1) * s2
        q_ref[:, sl] = (rq * (HEAD_DIM ** -0.5)).astype(q_ref.dtype)
        kf_ref[:, sl] = rk
        kb_ref[:, sl] = rk.astype(BF16)
    v = proj[:, 2 * w:3 * w]
    vf_ref[...] = v
    vb_ref[...] = v.astype(BF16)
    u_ref[...] = proj[:, 3 * w:4 * w].astype(u_ref.dtype)


def _inproj(x2, ln_g, w_bf, rc, rs1, rs2, tm, q_dtype, u_dtype):
    t, d = x2.shape
    w = w_bf.shape[1] // 4
    nt = t // tm
    npos = rc.shape[0] // tm
    row = lambda i: (i, 0)
    pos = lambda i: (i % npos, 0)
    fixed = lambda i: (0, 0)
    outs = [jax.ShapeDtypeStruct((t, w), q_dtype), jax.ShapeDtypeStruct((t, w), F32),
            jax.ShapeDtypeStruct((t, w), F32), jax.ShapeDtypeStruct((t, w), BF16),
            jax.ShapeDtypeStruct((t, w), BF16), jax.ShapeDtypeStruct((t, w), u_dtype)]
    return pl.pallas_call(
        _inproj_kernel,
        grid=(nt,),
        in_specs=[pl.BlockSpec((tm, d), row), pl.BlockSpec((1, d), fixed),
                  pl.BlockSpec((d, 4 * w), fixed),
                  pl.BlockSpec((tm, LANES), pos), pl.BlockSpec((tm, LANES), pos),
                  pl.BlockSpec((tm, LANES), pos)],
        out_specs=[pl.BlockSpec((tm, w), row)] * 6,
        out_shape=outs,
        compiler_params=_cparams(("parallel",)),
        name="inproj",
    )(x2, ln_g, w_bf, rc, rs1, rs2)


def _inproj_prompt_kernel(x_ref, g_ref, w_ref, wkt_ref, c_ref, s1_ref, s2_ref, ct_ref, st_ref,
                          q_ref, vf_ref, vb_ref, ut_ref, ktf_ref, ktb_ref):
    x = x_ref[0]
    ms = jnp.mean(x * x, axis=-1, keepdims=True)
    hn = (x * lax.rsqrt(ms + EPS) * g_ref[...]).astype(BF16)
    proj = jnp.dot(hn, w_ref[...], preferred_element_type=F32)
    w = q_ref.shape[-1]
    ktu = _nt_dot(wkt_ref[...], hn)
    kt = ktu[:w]
    ut_ref[0] = ktu[w:].astype(ut_ref.dtype)
    c, s1, s2 = c_ref[...], s1_ref[...], s2_ref[...]
    for j in range(w // LANES):
        sl = slice(j * LANES, (j + 1) * LANES)
        zq = proj[:, sl]
        rq = zq * c + pltpu.roll(zq, ROT_DIM // 2, 1) * s1 + pltpu.roll(zq, LANES - ROT_DIM // 2, 1) * s2
        q_ref[0, :, sl] = (rq * (HEAD_DIM ** -0.5)).astype(q_ref.dtype)
    v = proj[:, w:2 * w]
    nh = w // V_DIM
    for h in range(nh):
        vf_ref[0, pl.ds(h, v.shape[0], stride=nh), :] = v[:, h * V_DIM:(h + 1) * V_DIM]
    vb_ref[0] = v.astype(BF16)
    ct, st = ct_ref[...], st_ref[...]
    half = ROT_DIM // 2
    tk = ktb_ref.shape[-1]
    for hc in range(w // HEAD_DIM):
        base = hc * HEAD_DIM
        x1, x2 = kt[base:base + half], kt[base + half:base + ROT_DIM]
        blk = jnp.concatenate([x1 * ct - x2 * st, x2 * ct + x1 * st, kt[base + ROT_DIM:base + HEAD_DIM]], axis=0)
        ktf_ref[0, base:base + HEAD_DIM, :] = blk
        for t in range(blk.shape[1] // tk):
            ktb_ref[0, t, base:base + HEAD_DIM, :] = blk[:, t * tk:(t + 1) * tk].astype(BF16)


def _inproj_prompt(x3, ln_g, w_qv_bf, wkut_bf, rc, rs1, rs2, rct, rst, tm, tk):
    b, s, d = x3.shape
    w = wkut_bf.shape[0] // 2
    half = ROT_DIM // 2
    tok = lambda bi, i: (bi, i, 0)
    pos = lambda bi, i: (i, 0)
    fixed = lambda bi, i: (0, 0)
    nh = w // V_DIM
    outs = [jax.ShapeDtypeStruct((b, s, w), BF16), jax.ShapeDtypeStruct((b, s * nh, V_DIM), F32),
            jax.ShapeDtypeStruct((b, s, w), BF16), jax.ShapeDtypeStruct((b, w, s), BF16),
            jax.ShapeDtypeStruct((b, w, s), F32), jax.ShapeDtypeStruct((b, s // tk, w, tk), BF16)]
    return pl.pallas_call(
        _inproj_prompt_kernel,
        grid=(b, s // tm),
        in_specs=[pl.BlockSpec((1, tm, d), tok), pl.BlockSpec((1, d), fixed),
                  pl.BlockSpec((d, 2 * w), fixed), pl.BlockSpec((2 * w, d), fixed),
                  pl.BlockSpec((tm, LANES), pos), pl.BlockSpec((tm, LANES), pos), pl.BlockSpec((tm, LANES), pos),
                  pl.BlockSpec((half, tm), lambda bi, i: (0, i)), pl.BlockSpec((half, tm), lambda bi, i: (0, i))],
        out_specs=[pl.BlockSpec((1, tm, w), tok), pl.BlockSpec((1, tm * nh, V_DIM), tok),
                   pl.BlockSpec((1, tm, w), tok)]
                  + [pl.BlockSpec((1, w, tm), lambda bi, i: (bi, 0, i))] * 2
                  + [pl.BlockSpec((1, tm // tk, w, tk), lambda bi, i: (bi, i, 0, 0))],
        out_shape=outs,
        compiler_params=_cparams(("parallel", "parallel")),
        name="inproj_prompt",
    )(x3, ln_g, w_qv_bf, wkut_bf, rc, rs1, rs2, rct, rst)


def _rope_tables_t(pos):
    inv = ROPE_THETA ** (-jnp.arange(0, ROT_DIM, 2, dtype=F32) / ROT_DIM)
    ang = inv[:, None] * pos.astype(F32)[None, :]
    return jnp.cos(ang), jnp.sin(ang)


def _rope_tables(pos):
    half = ROT_DIM // 2
    inv = ROPE_THETA ** (-jnp.arange(0, ROT_DIM, 2, dtype=F32) / ROT_DIM)
    ang = pos.astype(F32)[:, None] * inv[None, :]
    cos, sin = jnp.cos(ang), jnp.sin(ang)
    n = pos.shape[0]
    pad = jnp.zeros((n, HEAD_DIM - ROT_DIM), F32)
    c = jnp.concatenate([cos, cos, pad + 1.0], axis=1)
    s1 = jnp.concatenate([jnp.zeros((n, half), F32), sin, pad], axis=1)
    s2 = jnp.concatenate([-sin, jnp.zeros((n, half), F32), pad], axis=1)
    rep = LANES // HEAD_DIM
    return tuple(jnp.tile(a, (1, rep)) for a in (c, s1, s2))


def _diff_lambda(lam_ref, lambda_init):
    l = lam_ref[...]
    a = jnp.sum(l[0:1] * l[1:2], axis=-1, keepdims=True)
    b = jnp.sum(l[2:3] * l[3:4], axis=-1, keepdims=True)
    return jnp.exp(a) - jnp.exp(b) + lambda_init


def _attn_kernel(q_ref, kt_ref, v_ref, lam_ref, g_ref, o_ref, s_ref, m_ref, acc_ref, vext_ref, *, tq, lambda_init):
    nq = q_ref.shape[1] // tq
    nl = tq // LANES
    vext_ref[:, 0:V_DIM] = v_ref[0]
    ones_col = lax.broadcasted_iota(jnp.int32, (v_ref.shape[1], LANES), 1) == 0
    vext_ref[:, V_DIM:] = jnp.where(ones_col, 1.0, 0.0).astype(BF16)
    lam = _diff_lambda(lam_ref, lambda_init)
    lane = lax.broadcasted_iota(jnp.int32, (tq, LANES), 1)
    row = lax.broadcasted_iota(jnp.int32, (tq, tq), 0)
    col = lax.broadcasted_iota(jnp.int32, (tq, tq), 1)
    for qi in range(nq):
        q = q_ref[0, qi * tq:(qi + 1) * tq, :]
        zero = jnp.zeros_like(q)
        qc = (jnp.where(lane < HEAD_DIM, q, zero), jnp.where(lane >= HEAD_DIM, q, zero))
        base = qi * (qi + 1) // 2
        for j in range(qi + 1):
            kt = kt_ref[0, j]
            for c in range(2):
                s = jnp.dot(qc[c], kt, preferred_element_type=F32)
                if j == qi:
                    s = jnp.where(col <= row, s, NEG_INF)
                s_ref[c, base + j] = s
                m = s[:, 0:LANES] if j == 0 else jnp.maximum(m_ref[qi, c], s[:, 0:LANES])
                for t in range(1, nl):
                    m = jnp.maximum(m, s[:, t * LANES:(t + 1) * LANES])
                m_ref[qi, c] = m
        for c in range(2):
            m_ref[qi, c] = jnp.broadcast_to(jnp.max(m_ref[qi, c], axis=-1, keepdims=True), (tq, LANES))
        for j in range(qi + 1):
            v = vext_ref[j * tq:(j + 1) * tq, :]
            for c in range(2):
                m = m_ref[qi, c]
                ps = [jnp.exp(s_ref[c, base + j, :, t * LANES:(t + 1) * LANES] - m) for t in range(nl)]
                pv = jnp.dot(jnp.concatenate(ps, axis=1).astype(BF16), v, preferred_element_type=F32)
                if j == 0:
                    acc_ref[qi, c] = pv
                else:
                    acc_ref[qi, c] += pv
        a0, a1 = acc_ref[qi, 0], acc_ref[qi, 1]
        o = a0[:, :V_DIM] / a0[:, V_DIM:V_DIM + 1] - lam * (a1[:, :V_DIM] / a1[:, V_DIM:V_DIM + 1])
        o = o * lax.rsqrt(jnp.mean(o * o, axis=-1, keepdims=True) + EPS)
        o_ref[0, qi * tq:(qi + 1) * tq, :] = (o * g_ref[...] * (1.0 - lambda_init)).astype(o_ref.dtype)


def _attn(q, kt4, v, lam4, subln_g, n_heads, lambda_init):
    b, s, w = q.shape
    nkt, tq = kt4.shape[1], kt4.shape[3]
    seq = lambda bi, h: (bi, 0, h)
    fixed = lambda bi, h: (0, 0)
    nq = s // tq
    return pl.pallas_call(
        functools.partial(_attn_kernel, tq=tq, lambda_init=lambda_init),
        grid=(b, n_heads),
        in_specs=[pl.BlockSpec((1, s, LANES), seq),
                  pl.BlockSpec((1, nkt, LANES, tq), lambda bi, h: (bi, 0, h, 0)),
                  pl.BlockSpec((1, s, LANES), seq),
                  pl.BlockSpec((4, HEAD_DIM), fixed), pl.BlockSpec((1, V_DIM), fixed)],
        out_specs=pl.BlockSpec((1, s, LANES), seq),
        out_shape=jax.ShapeDtypeStruct((b, s, w), BF16),
        scratch_shapes=[pltpu.VMEM((2, nq * (nq + 1) // 2, tq, tq), F32), pltpu.VMEM((nq, 2, tq, LANES), F32),
                        pltpu.VMEM((nq, 2, tq, 2 * V_DIM), F32), pltpu.VMEM((s, 2 * V_DIM), BF16)],
        compiler_params=_cparams(("parallel", "parallel")),
        name="attn",
    )(q, kt4, v, lam4, subln_g)


def _split2(x):
    hi = x.astype(BF16)
    return hi, (x - hi.astype(F32)).astype(BF16)


def _head_scores(k, qcol):
    prod = k * qcol
    return jnp.sum(prod.reshape(prod.shape[0] // HEAD_DIM, HEAD_DIM, prod.shape[1]), axis=1)


def _decode_kernel(pt_ref, q_ref, kn_ref, vn_ref, lam_ref, g_ref, *refs, pp, nb, n_heads, lambda_init):
    del pt_ref
    npg = nb * pp
    k_refs, v_refs = refs[:npg], refs[npg:2 * npg]
    o_ref = refs[2 * npg]
    m_ref, l_ref, acc_ref, tmp_ref = refs[2 * npg + 1:]
    j = pl.program_id(1)
    nhc = 2 * n_heads

    @pl.when(j == 0)
    def _():
        m_ref[...] = jnp.full_like(m_ref, NEG_INF)
        l_ref[...] = jnp.zeros_like(l_ref)
        acc_ref[...] = jnp.zeros_like(acc_ref)

    row_head = lax.broadcasted_iota(jnp.int32, (nhc, V_DIM), 0) // 2
    page = k_refs[0].shape[1]
    for r in range(nb):
        qcol = q_ref[r]
        s = [_head_scores(kr[...], qcol) for kr in k_refs[r * pp:(r + 1) * pp]]
        m_old = m_ref[r, :, 0:1]
        smax = functools.reduce(jnp.maximum, s)
        m_new = jnp.maximum(m_old, jnp.max(smax, axis=-1, keepdims=True))
        alpha = jnp.exp(m_old - m_new)
        p = [jnp.exp(si - m_new) for si in s]
        psum = functools.reduce(lambda a, b: a + b, p)
        l_ref[r] = jnp.broadcast_to(alpha * l_ref[r, :, 0:1] + jnp.sum(psum, axis=-1, keepdims=True),
                                    l_ref.shape[1:])
        m_ref[r] = jnp.broadcast_to(m_new, m_ref.shape[1:])
        acc = alpha * acc_ref[r]
        pb = [pi.astype(BF16) for pi in p]
        for h in range(n_heads):
            tot = None
            for pi, vr in zip(pb, v_refs[r * pp:(r + 1) * pp]):
                vh = vr[pl.ds(h, page, stride=n_heads), :].astype(BF16)
                res = jnp.dot(pi, vh, preferred_element_type=F32)
                tot = res if tot is None else tot + res
            acc = acc + jnp.where(row_head == h, tot, 0.0)
        acc_ref[r] = acc

    @pl.when(j == pl.num_programs(1) - 1)
    def _():
        lam = _diff_lambda(lam_ref, lambda_init)
        for r in range(nb):
            s_new = _head_scores(kn_ref[r], q_ref[r])[:, 0:1]
            m_old = m_ref[r, :, 0:1]
            m_f = jnp.maximum(m_old, s_new)
            a = jnp.exp(m_old - m_f)
            pn = jnp.exp(s_new - m_f)
            l_f = a * l_ref[r, :, 0:1] + pn
            tmp_ref[r] = (a * acc_ref[r] + pn * vn_ref[r]) / l_f
            o0 = tmp_ref[r, pl.ds(0, n_heads, stride=2), :]
            o1 = tmp_ref[r, pl.ds(1, n_heads, stride=2), :]
            o = o0 - lam * o1
            o = o * lax.rsqrt(jnp.mean(o * o, axis=-1, keepdims=True) + EPS)
            o_ref[r] = o * g_ref[...] * (1.0 - lambda_init)


def _decode_attn(page_table, qcol, kn_col, vn2, lam4, subln_g, cache_kt, cache_vr, pp, nb, lambda_init):
    bd, n_pages = page_table.shape
    _, w, page = cache_kt.shape
    n_heads = cache_vr.shape[1] // page
    nhc = 2 * n_heads
    pt_flat = page_table.reshape(-1)
    fixed2 = lambda b, j, pt: (0, 0)
    perb = lambda b, j, pt: (b, 0, 0)

    def page_spec(r, i, rows, cols):
        return pl.BlockSpec((None, rows, cols),
                            lambda b, j, pt: (pt[(b * nb + r) * n_pages + j * pp + i], 0, 0))

    slots = [(r, i) for r in range(nb) for i in range(pp)]
    grid_spec = pltpu.PrefetchScalarGridSpec(
        num_scalar_prefetch=1,
        grid=(bd // nb, n_pages // pp),
        in_specs=[pl.BlockSpec((nb, w, LANES), perb), pl.BlockSpec((nb, w, LANES), perb),
                  pl.BlockSpec((nb, nhc, V_DIM), perb),
                  pl.BlockSpec((4, HEAD_DIM), fixed2), pl.BlockSpec((1, V_DIM), fixed2)]
                 + [page_spec(r, i, w, page) for r, i in slots]
                 + [page_spec(r, i, page * n_heads, V_DIM) for r, i in slots],
        out_specs=pl.BlockSpec((nb, n_heads, V_DIM), perb),
        scratch_shapes=[pltpu.VMEM((nb, nhc, LANES), F32), pltpu.VMEM((nb, nhc, LANES), F32),
                        pltpu.VMEM((nb, nhc, V_DIM), F32), pltpu.VMEM((nb, nhc, V_DIM), F32)],
    )
    npg = nb * pp
    return pl.pallas_call(
        functools.partial(_decode_kernel, pp=pp, nb=nb, n_heads=n_heads, lambda_init=lambda_init),
        grid_spec=grid_spec,
        out_shape=jax.ShapeDtypeStruct((bd, n_heads, V_DIM), F32),
        compiler_params=_cparams(("parallel", "arbitrary")),
        name="decode_attn",
    )(pt_flat, qcol, kn_col, vn2, lam4, subln_g, *([cache_kt] * npg), *([cache_vr] * npg))


def _s5prep_kernel(arow_ref, acol_ref, ldt_ref, bt_ref, ct_ref, d_ref,
                   mt_ref, w_ref, vt_ref, lam_ref):
    for gi in range(arow_ref.shape[0]):
        _s5prep_group(gi, arow_ref, acol_ref, ldt_ref, bt_ref, ct_ref, d_ref, mt_ref, w_ref, vt_ref, lam_ref)


def _s5prep_group(gi, arow_ref, acol_ref, ldt_ref, bt_ref, ct_ref, d_ref, mt_ref, w_ref, vt_ref, lam_ref):
    L, hg, p = S5_CHUNK, S5_GROUP, S5_STATE
    dt = jnp.exp(ldt_ref[gi])
    ar, ai = arow_ref[gi, 0:1], arow_ref[gi, 1:2]
    arc, aic = acol_ref[gi, :, 0:1], acol_ref[gi, :, 1:2]

    def powers(a_r, a_i, j):
        mag = jnp.exp(j * (a_r * dt))
        return mag * jnp.cos(j * (a_i * dt)), mag * jnp.sin(j * (a_i * dt))

    l1r, l1i = powers(ar, ai, 1.0)
    llr, lli = powers(ar, ai, float(L))
    lam_ref[gi] = jnp.concatenate([llr, lli, l1r, l1i, jnp.zeros((4, p), F32)], axis=0)
    den = ar * ar + ai * ai
    cr = ((l1r - 1.0) * ar + l1i * ai) / den
    ci = (l1i * ar - (l1r - 1.0) * ai) / den
    btr, bti = bt_ref[gi, 0], bt_ref[gi, 1]
    bbr = btr * cr - bti * ci
    bbi = btr * ci + bti * cr
    jrow = lax.broadcasted_iota(jnp.int32, (L, p), 0).astype(F32)
    pr, pi = powers(ar, ai, jrow)
    for s in range(L):
        qr, qi = pr[L - 1 - s:L - s], pi[L - 1 - s:L - s]
        w_ref[gi, 0, s * hg:(s + 1) * hg, :] = bbr * qr - bbi * qi
        w_ref[gi, 1, s * hg:(s + 1) * hg, :] = bbr * qi + bbi * qr
    jl = (lax.broadcasted_iota(jnp.int32, (p, L * hg), 1) // hg).astype(F32)
    lpr, lpi = powers(arc, aic, jl)
    ctr, cti = ct_ref[gi, 0], ct_ref[gi, 1]
    cjr = ctr * lpr - cti * lpi
    cji = ctr * lpi + cti * lpr
    c1r, c1i = powers(arc, aic, 1.0)
    vt_ref[gi, 0] = cjr * c1r - cji * c1i
    vt_ref[gi, 1] = -(cjr * c1i + cji * c1r)
    kt = (jnp.dot(bbr, cjr, preferred_element_type=F32, precision=_HI)
          - jnp.dot(bbi, cji, preferred_element_type=F32, precision=_HI))
    n = L * hg
    ri = lax.broadcasted_iota(jnp.int32, (n, n), 0)
    cidx = lax.broadcasted_iota(jnp.int32, (n, n), 1)
    dtile = d_ref[gi]
    for s in range(L):
        shift = (cidx - ri == s * hg).astype(F32)
        blk = jnp.dot(kt, shift, preferred_element_type=F32, precision=_HI)
        rr = lax.broadcasted_iota(jnp.int32, (hg, n), 0) + s * hg
        cc = lax.broadcasted_iota(jnp.int32, (hg, n), 1)
        mt_ref[gi, s * hg:(s + 1) * hg, :] = blk + jnp.where(rr == cc, dtile, 0.0)


def _s5prep(arow, acol, ldt, bt, ct, dt_tiled):
    g = arow.shape[0]
    L, hg, p = S5_CHUNK, S5_GROUP, S5_STATE
    n = L * hg
    i3 = lambda i: (i, 0, 0)
    i4 = lambda i: (i, 0, 0, 0)
    gb = 4 if g % 4 == 0 else 1
    return pl.pallas_call(
        _s5prep_kernel,
        grid=(g // gb,),
        in_specs=[pl.BlockSpec((gb, 2, p), i3), pl.BlockSpec((gb, p, 2), i3), pl.BlockSpec((gb, 1, 1), i3),
                  pl.BlockSpec((gb, 2, hg, p), i4), pl.BlockSpec((gb, 2, p, n), i4),
                  pl.BlockSpec((gb, 1, n), i3)],
        out_specs=[pl.BlockSpec((gb, n, n), i3), pl.BlockSpec((gb, 2, n, p), i4),
                   pl.BlockSpec((gb, 2, p, n), i4), pl.BlockSpec((gb, SUBLANES, p), i3)],
        out_shape=[jax.ShapeDtypeStruct((g, n, n), F32), jax.ShapeDtypeStruct((g, 2, n, p), F32),
                   jax.ShapeDtypeStruct((g, 2, p, n), F32), jax.ShapeDtypeStruct((g, SUBLANES, p), F32)],
        compiler_params=_cparams(("parallel",)),
        name="s5prep",
    )(arow, acol, ldt, bt, ct, dt_tiled)


def _s5in_kernel(ut_ref, perm_ref, o_ref):
    g, nc, n = o_ref.shape[1:]
    hg, L = S5_GROUP, S5_CHUNK
    sh = perm_ref.shape[0]
    nseg, ncs = ut_ref.shape[2] // sh, sh // L
    ups = [jnp.dot(ut_ref[0, :, k * sh:(k + 1) * sh], perm_ref[...], preferred_element_type=F32)
           for k in range(nseg)]
    for gi in range(g):
        rows = slice(gi * hg, (gi + 1) * hg)
        ugt = jnp.concatenate(
            [jnp.concatenate([up[rows, s * ncs:(s + 1) * ncs] for up in ups], axis=1) for s in range(L)], axis=0)
        o_ref[0, gi] = ugt.T.astype(o_ref.dtype)


def _s5in(ut, perm):
    b, w, s = ut.shape
    g, nc, n = w // S5_GROUP, s // S5_CHUNK, S5_CHUNK * S5_GROUP
    return pl.pallas_call(
        _s5in_kernel,
        grid=(b,),
        in_specs=[pl.BlockSpec((1, w, s), lambda i: (i, 0, 0)),
                  pl.BlockSpec(perm.shape, lambda i: (0, 0), pipeline_mode=pl.Buffered(1))],
        out_specs=pl.BlockSpec((1, g, None, nc, n), lambda i: (i // SUBLANES, 0, i % SUBLANES, 0, 0)),
        out_shape=jax.ShapeDtypeStruct((b // SUBLANES, g, SUBLANES, nc, n), BF16),
        compiler_params=_cparams(("parallel",)),
        name="s5in",
    )(ut, perm)


def _s5out_kernel(y_ref, permt_ref, o_ref, ypt_ref):
    g, nc, n = y_ref.shape[1:]
    hg, L = S5_GROUP, S5_CHUNK
    nseg, sh = ypt_ref.shape[0], ypt_ref.shape[2]
    ncs = sh // L
    for gi in range(g):
        ygt = y_ref[0, gi].astype(F32).T
        for t in range(L):
            blk = ygt[t * hg:(t + 1) * hg, :].astype(BF16)
            for k in range(nseg):
                ypt_ref[k, gi * hg:(gi + 1) * hg, t * ncs:(t + 1) * ncs] = blk[:, k * ncs:(k + 1) * ncs]
    yt = jnp.concatenate([jnp.dot(ypt_ref[k], permt_ref[...], preferred_element_type=F32)
                          for k in range(nseg)], axis=1)
    o_ref[0] = yt.T.astype(o_ref.dtype)


def _s5out(y5, permt):
    nbh, g, _, nc, n = y5.shape
    b, s, w = nbh * SUBLANES, nc * S5_CHUNK, g * S5_GROUP
    return pl.pallas_call(
        _s5out_kernel,
        grid=(b,),
        in_specs=[pl.BlockSpec((1, g, None, nc, n), lambda i: (i // SUBLANES, 0, i % SUBLANES, 0, 0)),
                  pl.BlockSpec(permt.shape, lambda i: (0, 0), pipeline_mode=pl.Buffered(1))],
        out_specs=pl.BlockSpec((1, s, w), lambda i: (i, 0, 0)),
        out_shape=jax.ShapeDtypeStruct((b, s, w), BF16),
        scratch_shapes=[pltpu.VMEM((s // permt.shape[0], w, permt.shape[0]), BF16)],
        compiler_params=_cparams(("parallel",)),
        name="s5out",
    )(y5, permt)


def _s5chunk_kernel(u_ref, mt_ref, w_ref, wsw_ref, vt_ref, lam_ref, y_ref, hl_ref, t1_ref, t2_ref, hs_ref):
    gb = u_ref.shape[1]
    nc = u_ref.shape[2] // SUBLANES
    p = S5_STATE
    lane = lax.broadcasted_iota(jnp.int32, (SUBLANES, 2 * p), 1)
    la, lb = [], []
    for gi in range(gb):
        u = u_ref[0, gi]
        t1_ref[gi] = jnp.dot(u, w_ref[gi].astype(BF16), preferred_element_type=F32)
        t2_ref[gi] = jnp.dot(u, wsw_ref[gi].astype(BF16), preferred_element_type=F32)
        la.append(jnp.broadcast_to(lam_ref[gi, 0:1], (SUBLANES, 2 * p)))
        li2 = jnp.broadcast_to(lam_ref[gi, 1:2], (SUBLANES, 2 * p))
        lb.append(jnp.where(lane < p, -li2, li2))

    def step(c, h):
        out = []
        for gi in range(gb):
            a, b = h[2 * gi], h[2 * gi + 1]
            s1 = t1_ref[gi, pl.ds(c, SUBLANES, stride=nc), :]
            s2 = t2_ref[gi, pl.ds(c, SUBLANES, stride=nc), :]
            hs_ref[gi, pl.ds(pl.multiple_of(c * SUBLANES, SUBLANES), SUBLANES), :] = a
            out += [a * la[gi] + b * lb[gi] + s1, b * la[gi] - a * lb[gi] + s2]
        return tuple(out)

    z = jnp.zeros((SUBLANES, 2 * p), F32)
    h = lax.fori_loop(0, nc, step, (z,) * (2 * gb), unroll=8)
    for gi in range(gb):
        hl_ref[0, gi] = h[2 * gi]
        mt = mt_ref[gi].astype(BF16)
        vt = vt_ref[gi].astype(BF16)
        for bi in range(SUBLANES):
            rows = slice(bi * nc, (bi + 1) * nc)
            hb = hs_ref[gi, pl.ds(bi, nc, stride=SUBLANES), :].astype(BF16)
            y = (jnp.dot(u_ref[0, gi, rows, :], mt, preferred_element_type=F32)
                 + jnp.dot(hb, vt, preferred_element_type=F32))
            y_ref[0, gi, rows, :] = y.astype(y_ref.dtype)


def _s5chunk(u4, mt, wcat, wsw, vtcat, lam2):
    nbh, g, rows, n = u4.shape
    p2 = 2 * S5_STATE
    gb = 2 if g % 2 == 0 else 1
    um = lambda b, gi: (b, gi, 0, 0)
    g3 = lambda b, gi: (gi, 0, 0)
    return pl.pallas_call(
        _s5chunk_kernel,
        grid=(nbh, g // gb),
        in_specs=[pl.BlockSpec((1, gb, rows, n), um), pl.BlockSpec((gb, n, n), g3),
                  pl.BlockSpec((gb, n, p2), g3), pl.BlockSpec((gb, n, p2), g3), pl.BlockSpec((gb, p2, n), g3),
                  pl.BlockSpec((gb, SUBLANES, p2), g3)],
        out_specs=[pl.BlockSpec((1, gb, rows, n), um), pl.BlockSpec((1, gb, SUBLANES, p2), um)],
        out_shape=[jax.ShapeDtypeStruct(u4.shape, BF16),
                   jax.ShapeDtypeStruct((nbh, g, SUBLANES, p2), F32)],
        scratch_shapes=[pltpu.VMEM((gb, rows, p2), F32), pltpu.VMEM((gb, rows, p2), F32),
                        pltpu.VMEM((gb, rows, p2), F32)],
        compiler_params=_cparams(("parallel", "parallel")),
        name="s5chunk",
    )(u4, mt, wcat, wsw, vtcat, lam2)


def _s5step_kernel(u_ref, h0_ref, bb_ref, lam_ref, c_ref, d_ref, y_ref, h_ref):
    u = u_ref[0]
    h0r, h0i = h0_ref[0, 0], h0_ref[0, 1]
    l1r, l1i = lam_ref[0, 2:3], lam_ref[0, 3:4]
    bur = jnp.dot(u, bb_ref[0, 0], preferred_element_type=F32, precision=_HI)
    bui = jnp.dot(u, bb_ref[0, 1], preferred_element_type=F32, precision=_HI)
    hr = l1r * h0r - l1i * h0i + bur
    hi = l1r * h0i + l1i * h0r + bui
    h_ref[0, 0] = hr
    h_ref[0, 1] = hi
    y = _nt_dot(hr, c_ref[0, 0], precision=_HI) - _nt_dot(hi, c_ref[0, 1], precision=_HI)
    y_ref[0] = y + d_ref[0] * u


def _s5step(u3, h0, bb, lam, c2, d3):
    g, bd, hg = u3.shape
    p = S5_STATE
    i3 = lambda i: (i, 0, 0)
    i4 = lambda i: (i, 0, 0, 0)
    return pl.pallas_call(
        _s5step_kernel,
        grid=(g,),
        in_specs=[pl.BlockSpec((1, bd, hg), i3), pl.BlockSpec((1, 2, bd, p), i4),
                  pl.BlockSpec((1, 2, hg, p), i4), pl.BlockSpec((1, SUBLANES, p), i3),
                  pl.BlockSpec((1, 2, hg, p), i4), pl.BlockSpec((1, 1, hg), i3)],
        out_specs=[pl.BlockSpec((1, bd, hg), i3), pl.BlockSpec((1, 2, bd, p), i4)],
        out_shape=[jax.ShapeDtypeStruct((g, bd, hg), F32), jax.ShapeDtypeStruct((g, 2, bd, p), F32)],
        compiler_params=_cparams(("parallel",)),
        name="s5step",
    )(u3, h0, bb, lam, c2, d3)


def _postmix_kernel(x_ref, a_ref, y_ref, wglu_ref, bglu_ref, wout_ref, g2_ref, wr_ref, wrf_ref, br_ref,
                    x1_ref, h2_ref, comb_ref, *, precise, parts):
    th = x_ref.shape[0] // parts
    for part in range(parts):
        rows = slice(part * th, (part + 1) * th)
        _postmix_rows(x_ref, a_ref, y_ref, wglu_ref, bglu_ref, wout_ref, g2_ref, wr_ref, wrf_ref, br_ref,
                      x1_ref, h2_ref, comb_ref, rows, precise)


def _postmix_rows(x_ref, a_ref, y_ref, wglu_ref, bglu_ref, wout_ref, g2_ref, wr_ref, wrf_ref, br_ref,
                  x1_ref, h2_ref, comb_ref, rows, precise):
    wa = a_ref.shape[-1]

    def mm(act, w):
        if precise:
            return jnp.dot(act, w, preferred_element_type=F32, precision=_HI)
        return jnp.dot(act.astype(BF16), w, preferred_element_type=F32)

    y = jax.nn.gelu(y_ref[rows, :].astype(F32))
    z = mm(y, wglu_ref[...]) + bglu_ref[...]
    s5o = y * jax.nn.sigmoid(z)
    mix = mm(a_ref[rows, :].astype(F32) if precise else a_ref[rows, :], wout_ref[0:wa, :])
    mix += mm(s5o, wout_ref[wa:, :])
    x1 = x_ref[rows, :] + mix
    x1_ref[rows, :] = x1
    h2 = x1 * lax.rsqrt(jnp.mean(x1 * x1, axis=-1, keepdims=True) + EPS) * g2_ref[...]
    h2_ref[rows, :] = h2.astype(h2_ref.dtype)
    if precise:
        logits = jnp.dot(h2, wrf_ref[...], preferred_element_type=F32, precision=_HI)
    else:
        hi, lo = _split2(h2)
        hh = jnp.dot(hi, wr_ref[...], preferred_element_type=F32)
        logits = hh[:, :LANES] + hh[:, LANES:] + jnp.dot(lo, wr_ref[:, :LANES], preferred_element_type=F32)
    logits = logits + br_ref[...]
    ne = N_GROUPS_MOE * EXP_PER_GROUP
    lane = lax.broadcasted_iota(jnp.int32, logits.shape, 1).astype(F32)
    big = jnp.float32(1 << 20)
    gmask = (lane >= ne) & (lane < ne + N_GROUPS_MOE)
    gl = jnp.where(gmask, logits, NEG_INF)
    gmax = jnp.max(gl, axis=-1, keepdims=True)
    gidx = jnp.min(jnp.where(gl == gmax, lane, big), axis=-1, keepdims=True) - ne
    g_w = 1.0 / jnp.sum(jnp.where(gmask, jnp.exp(logits - gmax), 0.0), axis=-1, keepdims=True)
    lo = gidx * EXP_PER_GROUP
    el = jnp.where((lane >= lo) & (lane < lo + EXP_PER_GROUP), logits, NEG_INF)
    e1 = jnp.max(el, axis=-1, keepdims=True)
    i1 = jnp.min(jnp.where(el == e1, lane, big), axis=-1, keepdims=True)
    el2 = jnp.where(lane == i1, NEG_INF, el)
    e2 = jnp.max(el2, axis=-1, keepdims=True)
    i2 = jnp.min(jnp.where(el2 == e2, lane, big), axis=-1, keepdims=True)
    r = jnp.exp(e2 - e1)
    w1 = g_w / (1.0 + r)
    w2 = g_w * r / (1.0 + r)
    comb_ref[rows, :] = (jnp.where(lane == i1, w1, 0.0) + jnp.where(lane == i2, w2, 0.0)
                         + jnp.where(lane == GID_LANE, gidx, 0.0))


def _postmix(x2, attn, ys5, wglu, bglu, wout, ln2_g, w_router, b_router, tm, precise):
    t, d = x2.shape
    wa = attn.shape[1]
    ws = ys5.shape[1]
    row = lambda i: (i, 0)
    fixed = lambda i: (0, 0)
    wr_hi, wr_lo = _split2(w_router)
    wr_hl = jnp.concatenate([wr_hi, wr_lo], axis=1)
    parts = 2 if tm % 512 == 0 else 1
    return pl.pallas_call(
        functools.partial(_postmix_kernel, precise=precise, parts=parts),
        grid=(t // tm,),
        in_specs=[pl.BlockSpec((tm, d), row), pl.BlockSpec((tm, wa), row), pl.BlockSpec((tm, ws), row),
                  pl.BlockSpec((ws, ws), fixed), pl.BlockSpec((1, ws), fixed),
                  pl.BlockSpec((wa + ws, d), fixed), pl.BlockSpec((1, d), fixed),
                  pl.BlockSpec((d, 2 * LANES), fixed), pl.BlockSpec((d, LANES), fixed),
                  pl.BlockSpec((1, LANES), fixed)],
        out_specs=[pl.BlockSpec((tm, d), row), pl.BlockSpec((tm, d), row), pl.BlockSpec((tm, LANES), row)],
        out_shape=[jax.ShapeDtypeStruct((t, d), F32), jax.ShapeDtypeStruct((t, d), BF16),
                   jax.ShapeDtypeStruct((t, LANES), F32)],
        compiler_params=_cparams(("parallel",)),
        name="postmix",
    )(x2, attn, ys5, wglu, bglu, wout, ln2_g, wr_hl, w_router, b_router)


def _moe_kernel(h_ref, x1_ref, comb_ref, wg_ref, wu_ref, wd_ref, gf_ref, o_ref, acc_ref):
    e = pl.program_id(1)

    @pl.when(e == 0)
    def _():
        acc_ref[...] = jnp.zeros_like(acc_ref)

    h = h_ref[...]
    comb = comb_ref[...]
    lane = lax.broadcasted_iota(jnp.int32, comb.shape, 1)
    eb = wg_ref.shape[0]
    acc = acc_ref[...]
    for k in range(eb):
        he = jax.nn.silu(jnp.dot(h, wg_ref[k], preferred_element_type=F32))
        he = he * jnp.dot(h, wu_ref[k], preferred_element_type=F32)
        ce = jnp.sum(jnp.where(lane == e * eb + k, comb, 0.0), axis=-1, keepdims=True)
        acc = acc + jnp.dot((he * ce).astype(BF16), wd_ref[k], preferred_element_type=F32)
    acc_ref[...] = acc

    @pl.when(e == pl.num_programs(1) - 1)
    def _():
        x2 = x1_ref[...] + acc_ref[...]
        o_ref[...] = x2 * lax.rsqrt(jnp.mean(x2 * x2, axis=-1, keepdims=True) + EPS) * gf_ref[...]


def _moe(h2, x1, comb, wg_bf, wu_bf, wd_bf, lnf_g, tm):
    t, d = x1.shape
    ne, _, de = wg_bf.shape
    row = lambda i, e: (i, 0)
    eb = 4 if ne % 4 == 0 else 1
    return pl.pallas_call(
        _moe_kernel,
        grid=(t // tm, ne // eb),
        in_specs=[pl.BlockSpec((tm, d), row), pl.BlockSpec((tm, d), row), pl.BlockSpec((tm, LANES), row),
                  pl.BlockSpec((eb, d, de), lambda i, e: (e, 0, 0)),
                  pl.BlockSpec((eb, d, de), lambda i, e: (e, 0, 0)),
                  pl.BlockSpec((eb, de, d), lambda i, e: (e, 0, 0)),
                  pl.BlockSpec((1, d), lambda i, e: (0, 0))],
        out_specs=pl.BlockSpec((tm, d), row),
        out_shape=jax.ShapeDtypeStruct((t, d), F32),
        scratch_shapes=[pltpu.VMEM((tm, d), F32)],
        compiler_params=_cparams(("parallel", "arbitrary")),
        name="moe",
    )(h2, x1, comb, wg_bf, wu_bf, wd_bf, lnf_g)


def _moe_grouped_kernel(h_ref, x1_ref, comb_ref, ut_ref, lt_ref, wg_ref, wu_ref, wd_ref, gf_ref, o_ref,
                        rrow_ref, rcol_ref, chi_ref, clo_ref, cnt_ref):
    g = pl.program_id(1)
    tm, d = h_ref.shape
    gf32 = g.astype(F32)

    @pl.when(g == 0)
    def _():
        comb = comb_ref[...]
        combt = comb.T
        gidc = comb[:, GID_LANE:GID_LANE + 1]
        gidr = combt[GID_LANE:GID_LANE + 1, :]
        lane = lax.broadcasted_iota(jnp.int32, comb.shape, 1).astype(F32)
        sub = lax.broadcasted_iota(jnp.int32, (SUBLANES, tm), 0).astype(F32)
        ohc = lane == gidc
        ohr = sub == gidr
        cntc = jnp.dot(lt_ref[...], jnp.where(ohc, 1.0, 0.0).astype(BF16), preferred_element_type=F32)
        cntr = jnp.dot(jnp.where(ohr, 1.0, 0.0).astype(BF16), ut_ref[...], preferred_element_type=F32)
        rankc = jnp.sum(jnp.where(ohc, cntc, 0.0), axis=-1, keepdims=True) - 1.0
        rankr = jnp.sum(jnp.where(ohr, cntr, 0.0), axis=0, keepdims=True) - 1.0
        rcol_ref[...] = jnp.broadcast_to(rankc, rcol_ref.shape)
        rrow_ref[...] = jnp.concatenate([rankr, gidr, jnp.zeros((SUBLANES - 2, tm), F32)], axis=0)
        chi, clo = _split2(combt)
        chi_ref[...] = chi
        clo_ref[...] = clo
        for r in range(N_GROUPS_MOE):
            cnt_ref[r] = jnp.max(cntr[r:r + 1, :]).astype(jnp.int32)
        o_ref[...] = jnp.zeros_like(o_ref)

    posr = jnp.where(rrow_ref[1:2, :] == gf32, rrow_ref[0:1, :], -1.0)
    posc = jnp.where(comb_ref[:, GID_LANE:GID_LANE + 1] == gf32, rcol_ref[:, 0:1], -1.0)
    n_rows = cnt_ref[g]
    big = 2 * MOE_CHUNK
    mid = big + MOE_CHUNK // 2
    use_mid = jnp.logical_and(n_rows > big, n_rows <= mid)
    n_big = jnp.where(use_mid, 0, n_rows // big)
    n_mid = jnp.where(use_mid, 1, 0)
    n_small = jnp.where(use_mid, 0, (n_rows - n_big * big + (MOE_CHUNK - 1)) // MOE_CHUNK)

    def chunk(row0, ch):
        base = row0.astype(F32)
        rid = lax.broadcasted_iota(jnp.int32, (ch, tm), 0).astype(F32) + base
        cid = lax.broadcasted_iota(jnp.int32, (tm, ch), 1).astype(F32) + base
        sel = jnp.where(posr == rid, 1.0, 0.0).astype(BF16)
        selt = jnp.where(posc == cid, 1.0, 0.0).astype(BF16)
        xg = jnp.dot(sel, h_ref[...], preferred_element_type=F32).astype(BF16)
        cg = _nt_dot(sel, chi_ref[...]) + _nt_dot(sel, clo_ref[...])
        lane = lax.broadcasted_iota(jnp.int32, cg.shape, 1).astype(F32)
        z = jnp.zeros((ch, d), F32)
        for e in range(EXP_PER_GROUP):
            he = jax.nn.silu(jnp.dot(xg, wg_ref[e], preferred_element_type=F32))
            he = he * jnp.dot(xg, wu_ref[e], preferred_element_type=F32)
            ce = jnp.sum(jnp.where(lane == gf32 * EXP_PER_GROUP + e, cg, 0.0), axis=-1, keepdims=True)
            z = z + jnp.dot((he * ce).astype(BF16), wd_ref[e], preferred_element_type=F32)
        o_ref[...] += jnp.dot(selt, z.astype(BF16), preferred_element_type=F32)

    def big_body(k, carry):
        chunk(k * big, big)
        return carry

    def mid_body(k, carry):
        chunk(k * mid, mid)
        return carry

    def small_body(k, carry):
        chunk(n_big * big + k * MOE_CHUNK, MOE_CHUNK)
        return carry

    lax.fori_loop(0, n_big, big_body, 0)
    lax.fori_loop(0, n_mid, mid_body, 0)
    lax.fori_loop(0, n_small, small_body, 0)

    @pl.when(g == pl.num_programs(1) - 1)
    def _():
        x2 = x1_ref[...] + o_ref[...]
        o_ref[...] = x2 * lax.rsqrt(jnp.mean(x2 * x2, axis=-1, keepdims=True) + EPS) * gf_ref[...]


def _moe_grouped(h2, x1, comb, wg_bf, wu_bf, wd_bf, lnf_g, tm):
    t, d = x1.shape
    ne, _, de = wg_bf.shape
    ng = ne // EXP_PER_GROUP
    tri = jnp.arange(tm)[:, None] <= jnp.arange(tm)[None, :]
    ut = tri.astype(BF16)
    lt = tri.T.astype(BF16)
    once = pl.Buffered(1)
    tile = lambda i, g: (i, 0)
    fixed = lambda i, g: (0, 0)
    wmap = lambda i, g: (g, 0, 0)
    return pl.pallas_call(
        _moe_grouped_kernel,
        grid=(t // tm, ng),
        in_specs=[pl.BlockSpec((tm, d), tile), pl.BlockSpec((tm, d), tile),
                  pl.BlockSpec((tm, LANES), tile),
                  pl.BlockSpec((tm, tm), fixed, pipeline_mode=once),
                  pl.BlockSpec((tm, tm), fixed, pipeline_mode=once),
                  pl.BlockSpec((EXP_PER_GROUP, d, de), wmap),
                  pl.BlockSpec((EXP_PER_GROUP, d, de), wmap),
                  pl.BlockSpec((EXP_PER_GROUP, de, d), wmap),
                  pl.BlockSpec((1, d), fixed)],
        out_specs=pl.BlockSpec((tm, d), tile),
        out_shape=jax.ShapeDtypeStruct((t, d), F32),
        scratch_shapes=[pltpu.VMEM((SUBLANES, tm), F32), pltpu.VMEM((tm, LANES), F32),
                        pltpu.VMEM((LANES, tm), BF16), pltpu.VMEM((LANES, tm), BF16),
                        pltpu.SMEM((N_GROUPS_MOE,), jnp.int32)],
        compiler_params=_cparams(("parallel", "arbitrary")),
        name="moe_grouped",
    )(h2, x1, comb, ut, lt, wg_bf, wu_bf, wd_bf, lnf_g)


def _tile(n, want):
    t = min(n, want)
    while n % t:
        t //= 2
    return t


def kernel(x_prompt, x_sample, cache_k, cache_v, state_s5_re, state_s5_im, page_table, ln1_g, w_in, lambda_q1, lambda_k1, lambda_q2, lambda_k2, subln_g, s5_a_re, s5_a_im, s5_log_dt, s5_b_re, s5_b_im, s5_c_re, s5_c_im, s5_d, w_glu, b_glu, w_out, ln2_g, w_router_group, b_router_group, w_router_expert, b_router_expert, w_gate, w_up, w_down, ln_f_g):
    depth = ln1_g.shape[0]
    assert depth == 1, "single-layer step"
    b, s, d = x_prompt.shape
    bd, ds, _ = x_sample.shape
    assert ds == 1
    n_pages = page_table.shape[1]
    page = cache_k.shape[2]
    past_len = n_pages * page
    wq = w_in.shape[2] // 4
    n_heads = wq // V_DIM
    g = s5_a_re.shape[1]
    p = S5_STATE
    hg = S5_GROUP
    L = S5_CHUNK
    n = L * hg
    assert b % SUBLANES == 0 and s % L == 0 and g * hg == wq
    lambda_init = 0.8 - 0.6 * math.exp(-0.3 * 0)

    w_in_bf = w_in[0].astype(BF16)
    wglu_bf = w_glu[0].astype(BF16)
    wout_bf = w_out[0].astype(BF16)
    wg_bf, wu_bf, wd_bf = (w[0].astype(BF16) for w in (w_gate, w_up, w_down))
    ne = w_gate.shape[1]
    w_router = jnp.concatenate(
        [jnp.transpose(w_router_expert[0], (1, 0, 2)).reshape(d, ne), w_router_group[0],
         jnp.zeros((d, LANES - ne - N_GROUPS_MOE), F32)], axis=1)
    b_router = jnp.concatenate([b_router_expert[0].reshape(ne), b_router_group[0],
                                jnp.zeros((LANES - ne - N_GROUPS_MOE,), F32)])[None, :]
    lam4 = jnp.concatenate([lambda_q1, lambda_k1, lambda_q2, lambda_k2], axis=0)
    ln1 = ln1_g[0][None, :]
    ln2 = ln2_g[0][None, :]
    lnf = ln_f_g[None, :]
    subg = subln_g[0][None, :]
    bglu = b_glu[0][None, :]

    arow = jnp.stack([s5_a_re[0], s5_a_im[0]], axis=1)
    acol = jnp.stack([s5_a_re[0], s5_a_im[0]], axis=2)
    ldt = s5_log_dt[0].reshape(g, 1, 1)
    bt = jnp.stack([jnp.swapaxes(s5_b_re[0], 1, 2), jnp.swapaxes(s5_b_im[0], 1, 2)], axis=1)
    ct = jnp.stack([jnp.tile(jnp.swapaxes(s5_c_re[0], 1, 2), (1, 1, L)),
                    jnp.tile(jnp.swapaxes(s5_c_im[0], 1, 2), (1, 1, L))], axis=1)
    d_g = s5_d[0].reshape(g, 1, hg)
    mt, w_s5, vt, lam_s5 = _s5prep(arow, acol, ldt, bt, ct, jnp.tile(d_g, (1, 1, L)))

    tm = _tile(b * s, 512)
    x2 = x_prompt.reshape(b * s, d)
    pos_p = jnp.arange(s, dtype=jnp.int32)
    rc, rs1, rs2 = _rope_tables(pos_p)
    rct, rst = _rope_tables_t(pos_p)
    w_qv_bf = jnp.concatenate([w_in_bf[:, :wq], w_in_bf[:, 2 * wq:3 * wq]], axis=1)
    wkut_bf = jnp.concatenate([w_in_bf[:, wq:2 * wq], w_in_bf[:, 3 * wq:]], axis=1).T
    tq = _tile(s, 256)
    q, vf, vb, ut, ktf, ktb = _inproj_prompt(x_prompt, ln1, w_qv_bf, wkut_bf, rc, rs1, rs2, rct, rst,
                                             _tile(s, tm), tq)
    attn = _attn(q, ktb, vb, lam4, subg, n_heads, lambda_init)
    kf = jnp.transpose(ktf.reshape(b, 2 * n_heads, HEAD_DIM, s), (0, 3, 1, 2))
    nbh, nc = b // SUBLANES, s // L
    seg = s // 2 if (s // 2) % (L * LANES // 2) == 0 else s
    tok = jnp.arange(seg)
    perm = (((tok % L) * (seg // L) + tok // L)[:, None] == tok[None, :]).astype(BF16)
    u5 = _s5in(ut, perm)
    wcat = jnp.concatenate([w_s5[:, 0], w_s5[:, 1]], axis=-1)
    wsw = jnp.concatenate([w_s5[:, 1], w_s5[:, 0]], axis=-1)
    vtcat = jnp.concatenate([vt[:, 0], vt[:, 1]], axis=1)
    lam2 = jnp.concatenate([lam_s5, lam_s5], axis=-1)
    y5, hl = _s5chunk(u5.reshape(nbh, g, SUBLANES * nc, n), mt, wcat, wsw, vtcat, lam2)
    ys5 = _s5out(y5.reshape(nbh, g, SUBLANES, nc, n), perm.T).reshape(b * s, wq)
    hl = hl.transpose(0, 2, 1, 3).reshape(b, g, 2, p).transpose(2, 0, 1, 3)
    x1, h2, comb = _postmix(x2, attn.reshape(b * s, wq), ys5, wglu_bf, bglu, wout_bf, ln2,
                            w_router, b_router, tm, False)
    y_prompt = _moe_grouped(h2, x1, comb, wg_bf, wu_bf, wd_bf, lnf, _tile(b * s, 1024)).reshape(b, s, d)

    xs2 = x_sample.reshape(bd, d)
    pos_s = jnp.full((bd,), past_len, jnp.int32)
    sc, ss1, ss2 = _rope_tables(pos_s)
    qs, kfs, vfs, _, _, us = _inproj(xs2, ln1, w_in[0], sc, ss1, ss2, bd, F32, F32)
    cache_kt = jnp.transpose(cache_k[0], (0, 2, 3, 1)).reshape(-1, wq, page)
    cache_vr = cache_v[0].reshape(-1, page * n_heads, V_DIM)
    qcol = jnp.broadcast_to(qs[:, :, None], (bd, wq, LANES))
    kn_col = jnp.broadcast_to(kfs[:, :, None], (bd, wq, LANES))
    vn2 = jnp.repeat(vfs.reshape(bd, n_heads, V_DIM), 2, axis=1)
    pp = _tile(n_pages, 16)
    nb_dec = 2 if bd % 2 == 0 else 1
    attn_s = _decode_attn(page_table, qcol, kn_col, vn2, lam4, subg, cache_kt, cache_vr, pp, nb_dec,
                          lambda_init)
    u3 = us.reshape(bd, g, hg).transpose(1, 0, 2)
    h0 = jnp.stack([state_s5_re[0], state_s5_im[0]]).transpose(2, 0, 1, 3)
    bb = w_s5[:, :, (L - 1) * hg:, :]
    c2 = jnp.stack([s5_c_re[0], s5_c_im[0]], axis=1)
    ys3, hs = _s5step(u3, h0, bb, lam_s5, c2, d_g)
    ys_s5 = ys3.transpose(1, 0, 2).reshape(bd, wq)
    x1s, h2s, combs = _postmix(xs2, attn_s.reshape(bd, wq), ys_s5, w_glu[0], bglu, w_out[0], ln2,
                               w_router, b_router, bd, True)
    y_sample = _moe(h2s, x1s, combs, wg_bf, wu_bf, wd_bf, lnf, bd).reshape(bd, 1, d)
    hs = hs.transpose(1, 2, 0, 3)

    return (y_prompt, y_sample,
            kf.reshape(1, b, s, 2 * n_heads, HEAD_DIM), vf.reshape(1, b, s, n_heads, V_DIM),
            hl[0][None], hl[1][None],
            kfs.reshape(1, bd, 1, 2 * n_heads, HEAD_DIM), vfs.reshape(1, bd, 1, n_heads, V_DIM),
            hs[0][None], hs[1][None])
```

```python
import functools
import math

import jax
import jax.numpy as jnp
from jax import lax
from jax.experimental import pallas as pl
from jax.experimental.pallas import tpu as pltpu

F32 = jnp.float32
BF16 = jnp.bfloat16

HEAD_DIM = 64
V_DIM = 128
ROT_DIM = 16
ROPE_THETA = 500000.0
S5_GROUP = 16
S5_STATE = 64
S5_CHUNK = 16
SUBLANES = 8
N_GROUPS_MOE = 4
EXP_PER_GROUP = 8
EPS = 1e-5
NEG_INF = -1e30
LANES = 128
GID_LANE = LANES - 1
MOE_CHUNK = 128
VMEM_LIMIT = 56 * 1024 * 1024
TOKEN_TILE = 512
ATTN_TILE = 256
MOE_TILE = 1024
DECODE_PAGES = 16
DECODE_ROWS = 2

_HI = lax.Precision.HIGHEST


def _cparams(sem):
    return pltpu.CompilerParams(dimension_semantics=sem, vmem_limit_bytes=VMEM_LIMIT)


def _nt_dot(a, b, **kw):
    return lax.dot_general(a, b, (((1,), (1,)), ((), ())), preferred_element_type=F32, **kw)


def _inproj_kernel(x_ref, g_ref, w_ref, c_ref, s1_ref, s2_ref, q_ref, kf_ref, vf_ref, u_ref):
    x = x_ref[...]
    ms = jnp.mean(x * x, axis=-1, keepdims=True)
    hn = x * lax.rsqrt(ms + EPS) * g_ref[...]
    proj = jnp.dot(hn, w_ref[...], preferred_element_type=F32, precision=_HI)
    w = q_ref.shape[-1]
    c = c_ref[...]
    s1 = s1_ref[...]
    s2 = s2_ref[...]
    for j in range(w // LANES):
        sl = slice(j * LANES, (j + 1) * LANES)
        zq = proj[:, j * LANES:(j + 1) * LANES]
        zk = proj[:, w + j * LANES:w + (j + 1) * LANES]
        rq = zq * c + pltpu.roll(zq, ROT_DIM // 2, 1) * s1 + pltpu.roll(zq, LANES - ROT_DIM // 2, 1) * s2
        rk = zk * c + pltpu.roll(zk, ROT_DIM // 2, 1) * s1 + pltpu.roll(zk, LANES - ROT_DIM // 2, 1) * s2
        q_ref[:, sl] = rq * (HEAD_DIM ** -0.5)
        kf_ref[:, sl] = rk
    vf_ref[...] = proj[:, 2 * w:3 * w]
    u_ref[...] = proj[:, 3 * w:4 * w]


def _inproj(x2, ln_g, w_in, rc, rs1, rs2, tm):
    t, d = x2.shape
    w = w_in.shape[1] // 4
    nt = t // tm
    npos = rc.shape[0] // tm
    row = lambda i: (i, 0)
    pos = lambda i: (i % npos, 0)
    fixed = lambda i: (0, 0)
    return pl.pallas_call(
        _inproj_kernel,
        grid=(nt,),
        in_specs=[pl.BlockSpec((tm, d), row), pl.BlockSpec((1, d), fixed),
                  pl.BlockSpec((d, 4 * w), fixed),
                  pl.BlockSpec((tm, LANES), pos), pl.BlockSpec((tm, LANES), pos),
                  pl.BlockSpec((tm, LANES), pos)],
        out_specs=[pl.BlockSpec((tm, w), row)] * 4,
        out_shape=[jax.ShapeDtypeStruct((t, w), F32)] * 4,
        compiler_params=_cparams(("parallel",)),
        name="inproj",
    )(x2, ln_g, w_in, rc, rs1, rs2)


def _inproj_prompt_kernel(x_ref, g_ref, w_ref, wkt_ref, c_ref, s1_ref, s2_ref, ct_ref, st_ref,
                          q_ref, vf_ref, vb_ref, ut_ref, ktf_ref, ktb_ref):
    x = x_ref[0]
    ms = jnp.mean(x * x, axis=-1, keepdims=True)
    hn = (x * lax.rsqrt(ms + EPS) * g_ref[...]).astype(BF16)
    proj = jnp.dot(hn, w_ref[...], preferred_element_type=F32)
    w = q_ref.shape[-1]
    ktu = _nt_dot(wkt_ref[...], hn)
    kt = ktu[:w]
    ut_ref[0] = ktu[w:].astype(ut_ref.dtype)
    c, s1, s2 = c_ref[...], s1_ref[...], s2_ref[...]
    for j in range(w // LANES):
        sl = slice(j * LANES, (j + 1) * LANES)
        zq = proj[:, sl]
        rq = zq * c + pltpu.roll(zq, ROT_DIM // 2, 1) * s1 + pltpu.roll(zq, LANES - ROT_DIM // 2, 1) * s2
        q_ref[0, :, sl] = (rq * (HEAD_DIM ** -0.5)).astype(q_ref.dtype)
    v = proj[:, w:2 * w]
    nh = w // V_DIM
    for h in range(nh):
        vf_ref[0, pl.ds(h, v.shape[0], stride=nh), :] = v[:, h * V_DIM:(h + 1) * V_DIM]
    vb_ref[0] = v.astype(BF16)
    ct, st = ct_ref[...], st_ref[...]
    half = ROT_DIM // 2
    tk = ktb_ref.shape[-1]
    for hc in range(w // HEAD_DIM):
        base = hc * HEAD_DIM
        x1, x2 = kt[base:base + half], kt[base + half:base + ROT_DIM]
        blk = jnp.concatenate([x1 * ct - x2 * st, x2 * ct + x1 * st, kt[base + ROT_DIM:base + HEAD_DIM]], axis=0)
        ktf_ref[0, base:base + HEAD_DIM, :] = blk
        for t in range(blk.shape[1] // tk):
            ktb_ref[0, t, base:base + HEAD_DIM, :] = blk[:, t * tk:(t + 1) * tk].astype(BF16)


def _inproj_prompt(x3, ln_g, w_qv_bf, wkut_bf, rc, rs1, rs2, rct, rst, tm, tk):
    b, s, d = x3.shape
    w = wkut_bf.shape[0] // 2
    half = ROT_DIM // 2
    tok = lambda bi, i: (bi, i, 0)
    pos = lambda bi, i: (i, 0)
    fixed = lambda bi, i: (0, 0)
    nh = w // V_DIM
    outs = [jax.ShapeDtypeStruct((b, s, w), BF16), jax.ShapeDtypeStruct((b, s * nh, V_DIM), F32),
            jax.ShapeDtypeStruct((b, s, w), BF16), jax.ShapeDtypeStruct((b, w, s), BF16),
            jax.ShapeDtypeStruct((b, w, s), F32), jax.ShapeDtypeStruct((b, s // tk, w, tk), BF16)]
    return pl.pallas_call(
        _inproj_prompt_kernel,
        grid=(b, s // tm),
        in_specs=[pl.BlockSpec((1, tm, d), tok), pl.BlockSpec((1, d), fixed),
                  pl.BlockSpec((d, 2 * w), fixed), pl.BlockSpec((2 * w, d), fixed),
                  pl.BlockSpec((tm, LANES), pos), pl.BlockSpec((tm, LANES), pos), pl.BlockSpec((tm, LANES), pos),
                  pl.BlockSpec((half, tm), lambda bi, i: (0, i)), pl.BlockSpec((half, tm), lambda bi, i: (0, i))],
        out_specs=[pl.BlockSpec((1, tm, w), tok), pl.BlockSpec((1, tm * nh, V_DIM), tok),
                   pl.BlockSpec((1, tm, w), tok)]
                  + [pl.BlockSpec((1, w, tm), lambda bi, i: (bi, 0, i))] * 2
                  + [pl.BlockSpec((1, tm // tk, w, tk), lambda bi, i: (bi, i, 0, 0))],
        out_shape=outs,
        compiler_params=_cparams(("parallel", "parallel")),
        name="inproj_prompt",
    )(x3, ln_g, w_qv_bf, wkut_bf, rc, rs1, rs2, rct, rst)


def _rope_tables_t(pos):
    inv = ROPE_THETA ** (-jnp.arange(0, ROT_DIM, 2, dtype=F32) / ROT_DIM)
    ang = inv[:, None] * pos.astype(F32)[None, :]
    return jnp.cos(ang), jnp.sin(ang)


def _rope_tables(pos):
    half = ROT_DIM // 2
    inv = ROPE_THETA ** (-jnp.arange(0, ROT_DIM, 2, dtype=F32) / ROT_DIM)
    ang = pos.astype(F32)[:, None] * inv[None, :]
    cos, sin = jnp.cos(ang), jnp.sin(ang)
    n = pos.shape[0]
    pad = jnp.zeros((n, HEAD_DIM - ROT_DIM), F32)
    c = jnp.concatenate([cos, cos, pad + 1.0], axis=1)
    s1 = jnp.concatenate([jnp.zeros((n, half), F32), sin, pad], axis=1)
    s2 = jnp.concatenate([-sin, jnp.zeros((n, half), F32), pad], axis=1)
    rep = LANES // HEAD_DIM
    return tuple(jnp.tile(a, (1, rep)) for a in (c, s1, s2))


def _diff_lambda(lam_ref, lambda_init):
    l = lam_ref[...]
    a = jnp.sum(l[0:1] * l[1:2], axis=-1, keepdims=True)
    b = jnp.sum(l[2:3] * l[3:4], axis=-1, keepdims=True)
    return jnp.exp(a) - jnp.exp(b) + lambda_init


def _attn_kernel(q_ref, kt_ref, v_ref, lam_ref, g_ref, o_ref, s_ref, m_ref, acc_ref, vext_ref, *, tq, lambda_init):
    nq = q_ref.shape[1] // tq
    nl = tq // LANES
    vext_ref[:, 0:V_DIM] = v_ref[0]
    ones_col = lax.broadcasted_iota(jnp.int32, (v_ref.shape[1], LANES), 1) == 0
    vext_ref[:, V_DIM:] = jnp.where(ones_col, 1.0, 0.0).astype(BF16)
    lam = _diff_lambda(lam_ref, lambda_init)
    lane = lax.broadcasted_iota(jnp.int32, (tq, LANES), 1)
    row = lax.broadcasted_iota(jnp.int32, (tq, tq), 0)
    col = lax.broadcasted_iota(jnp.int32, (tq, tq), 1)
    for qi in range(nq):
        q = q_ref[0, qi * tq:(qi + 1) * tq, :]
        zero = jnp.zeros_like(q)
        qc = (jnp.where(lane < HEAD_DIM, q, zero), jnp.where(lane >= HEAD_DIM, q, zero))
        base = qi * (qi + 1) // 2
        for j in range(qi + 1):
            kt = kt_ref[0, j]
            for c in range(2):
                s = jnp.dot(qc[c], kt, preferred_element_type=F32)
                if j == qi:
                    s = jnp.where(col <= row, s, NEG_INF)
                s_ref[c, base + j] = s
                m = s[:, 0:LANES] if j == 0 else jnp.maximum(m_ref[qi, c], s[:, 0:LANES])
                for t in range(1, nl):
                    m = jnp.maximum(m, s[:, t * LANES:(t + 1) * LANES])
                m_ref[qi, c] = m
        for c in range(2):
            m_ref[qi, c] = jnp.broadcast_to(jnp.max(m_ref[qi, c], axis=-1, keepdims=True), (tq, LANES))
        for j in range(qi + 1):
            v = vext_ref[j * tq:(j + 1) * tq, :]
            for c in range(2):
                m = m_ref[qi, c]
                ps = [jnp.exp(s_ref[c, base + j, :, t * LANES:(t + 1) * LANES] - m) for t in range(nl)]
                pv = jnp.dot(jnp.concatenate(ps, axis=1).astype(BF16), v, preferred_element_type=F32)
                if j == 0:
                    acc_ref[qi, c] = pv
                else:
                    acc_ref[qi, c] += pv
        a0, a1 = acc_ref[qi, 0], acc_ref[qi, 1]
        o = a0[:, :V_DIM] / a0[:, V_DIM:V_DIM + 1] - lam * (a1[:, :V_DIM] / a1[:, V_DIM:V_DIM + 1])
        o = o * lax.rsqrt(jnp.mean(o * o, axis=-1, keepdims=True) + EPS)
        o_ref[0, qi * tq:(qi + 1) * tq, :] = (o * g_ref[...] * (1.0 - lambda_init)).astype(o_ref.dtype)


def _attn(q, kt4, v, lam4, subln_g, n_heads, lambda_init):
    b, s, w = q.shape
    nkt, tq = kt4.shape[1], kt4.shape[3]
    seq = lambda bi, h: (bi, 0, h)
    fixed = lambda bi, h: (0, 0)
    nq = s // tq
    return pl.pallas_call(
        functools.partial(_attn_kernel, tq=tq, lambda_init=lambda_init),
        grid=(b, n_heads),
        in_specs=[pl.BlockSpec((1, s, LANES), seq),
                  pl.BlockSpec((1, nkt, LANES, tq), lambda bi, h: (bi, 0, h, 0)),
                  pl.BlockSpec((1, s, LANES), seq),
                  pl.BlockSpec((4, HEAD_DIM), fixed), pl.BlockSpec((1, V_DIM), fixed)],
        out_specs=pl.BlockSpec((1, s, LANES), seq),
        out_shape=jax.ShapeDtypeStruct((b, s, w), BF16),
        scratch_shapes=[pltpu.VMEM((2, nq * (nq + 1) // 2, tq, tq), F32), pltpu.VMEM((nq, 2, tq, LANES), F32),
                        pltpu.VMEM((nq, 2, tq, 2 * V_DIM), F32), pltpu.VMEM((s, 2 * V_DIM), BF16)],
        compiler_params=_cparams(("parallel", "parallel")),
        name="attn",
    )(q, kt4, v, lam4, subln_g)


def _split2(x):
    hi = x.astype(BF16)
    return hi, (x - hi.astype(F32)).astype(BF16)


def _head_scores(k, qcol):
    prod = k * qcol
    return jnp.sum(prod.reshape(prod.shape[0] // HEAD_DIM, HEAD_DIM, prod.shape[1]), axis=1)


def _decode_kernel(pt_ref, q_ref, kn_ref, vn_ref, lam_ref, g_ref, *refs, pp, nb, n_heads, lambda_init):
    del pt_ref
    npg = nb * pp
    k_refs, v_refs = refs[:npg], refs[npg:2 * npg]
    o_ref = refs[2 * npg]
    m_ref, l_ref, acc_ref, tmp_ref = refs[2 * npg + 1:]
    j = pl.program_id(1)
    nhc = 2 * n_heads

    @pl.when(j == 0)
    def _():
        m_ref[...] = jnp.full_like(m_ref, NEG_INF)
        l_ref[...] = jnp.zeros_like(l_ref)
        acc_ref[...] = jnp.zeros_like(acc_ref)

    row_head = lax.broadcasted_iota(jnp.int32, (nhc, V_DIM), 0) // 2
    page = k_refs[0].shape[1]
    for r in range(nb):
        qcol = q_ref[r]
        s = [_head_scores(kr[...], qcol) for kr in k_refs[r * pp:(r + 1) * pp]]
        m_old = m_ref[r, :, 0:1]
        smax = functools.reduce(jnp.maximum, s)
        m_new = jnp.maximum(m_old, jnp.max(smax, axis=-1, keepdims=True))
        alpha = jnp.exp(m_old - m_new)
        p = [jnp.exp(si - m_new) for si in s]
        psum = functools.reduce(lambda a, b: a + b, p)
        l_ref[r] = jnp.broadcast_to(alpha * l_ref[r, :, 0:1] + jnp.sum(psum, axis=-1, keepdims=True),
                                    l_ref.shape[1:])
        m_ref[r] = jnp.broadcast_to(m_new, m_ref.shape[1:])
        acc = alpha * acc_ref[r]
        pb = [pi.astype(BF16) for pi in p]
        for h in range(n_heads):
            tot = None
            for pi, vr in zip(pb, v_refs[r * pp:(r + 1) * pp]):
                vh = vr[pl.ds(h, page, stride=n_heads), :].astype(BF16)
                res = jnp.dot(pi, vh, preferred_element_type=F32)
                tot = res if tot is None else tot + res
            acc = acc + jnp.where(row_head == h, tot, 0.0)
        acc_ref[r] = acc

    @pl.when(j == pl.num_programs(1) - 1)
    def _():
        lam = _diff_lambda(lam_ref, lambda_init)
        for r in range(nb):
            s_new = _head_scores(kn_ref[r], q_ref[r])[:, 0:1]
            m_old = m_ref[r, :, 0:1]
            m_f = jnp.maximum(m_old, s_new)
            a = jnp.exp(m_old - m_f)
            pn = jnp.exp(s_new - m_f)
            l_f = a * l_ref[r, :, 0:1] + pn
            tmp_ref[r] = (a * acc_ref[r] + pn * vn_ref[r]) / l_f
            o0 = tmp_ref[r, pl.ds(0, n_heads, stride=2), :]
            o1 = tmp_ref[r, pl.ds(1, n_heads, stride=2), :]
            o = o0 - lam * o1
            o = o * lax.rsqrt(jnp.mean(o * o, axis=-1, keepdims=True) + EPS)
            o_ref[r] = o * g_ref[...] * (1.0 - lambda_init)


def _decode_attn(page_table, qcol, kn_col, vn2, lam4, subln_g, cache_kt, cache_vr, pp, nb, lambda_init):
    bd, n_pages = page_table.shape
    _, w, page = cache_kt.shape
    n_heads = cache_vr.shape[1] // page
    nhc = 2 * n_heads
    pt_flat = page_table.reshape(-1)
    fixed2 = lambda b, j, pt: (0, 0)
    perb = lambda b, j, pt: (b, 0, 0)

    def page_spec(r, i, rows, cols):
        return pl.BlockSpec((None, rows, cols),
                            lambda b, j, pt: (pt[(b * nb + r) * n_pages + j * pp + i], 0, 0))

    slots = [(r, i) for r in range(nb) for i in range(pp)]
    grid_spec = pltpu.PrefetchScalarGridSpec(
        num_scalar_prefetch=1,
        grid=(bd // nb, n_pages // pp),
        in_specs=[pl.BlockSpec((nb, w, LANES), perb), pl.BlockSpec((nb, w, LANES), perb),
                  pl.BlockSpec((nb, nhc, V_DIM), perb),
                  pl.BlockSpec((4, HEAD_DIM), fixed2), pl.BlockSpec((1, V_DIM), fixed2)]
                 + [page_spec(r, i, w, page) for r, i in slots]
                 + [page_spec(r, i, page * n_heads, V_DIM) for r, i in slots],
        out_specs=pl.BlockSpec((nb, n_heads, V_DIM), perb),
        scratch_shapes=[pltpu.VMEM((nb, nhc, LANES), F32), pltpu.VMEM((nb, nhc, LANES), F32),
                        pltpu.VMEM((nb, nhc, V_DIM), F32), pltpu.VMEM((nb, nhc, V_DIM), F32)],
    )
    npg = nb * pp
    return pl.pallas_call(
        functools.partial(_decode_kernel, pp=pp, nb=nb, n_heads=n_heads, lambda_init=lambda_init),
        grid_spec=grid_spec,
        out_shape=jax.ShapeDtypeStruct((bd, n_heads, V_DIM), F32),
        compiler_params=_cparams(("parallel", "arbitrary")),
        name="decode_attn",
    )(pt_flat, qcol, kn_col, vn2, lam4, subln_g, *([cache_kt] * npg), *([cache_vr] * npg))


def _s5prep_kernel(arow_ref, acol_ref, ldt_ref, bt_ref, ct_ref, d_ref,
                   mt_ref, w_ref, vt_ref, lam_ref):
    for gi in range(arow_ref.shape[0]):
        _s5prep_group(gi, arow_ref, acol_ref, ldt_ref, bt_ref, ct_ref, d_ref, mt_ref, w_ref, vt_ref, lam_ref)


def _s5prep_group(gi, arow_ref, acol_ref, ldt_ref, bt_ref, ct_ref, d_ref, mt_ref, w_ref, vt_ref, lam_ref):
    L, hg, p = S5_CHUNK, S5_GROUP, S5_STATE
    dt = jnp.exp(ldt_ref[gi])
    ar, ai = arow_ref[gi, 0:1], arow_ref[gi, 1:2]
    arc, aic = acol_ref[gi, :, 0:1], acol_ref[gi, :, 1:2]

    def powers(a_r, a_i, j):
        mag = jnp.exp(j * (a_r * dt))
        return mag * jnp.cos(j * (a_i * dt)), mag * jnp.sin(j * (a_i * dt))

    l1r, l1i = powers(ar, ai, 1.0)
    llr, lli = powers(ar, ai, float(L))
    lam_ref[gi] = jnp.concatenate([llr, lli, l1r, l1i, jnp.zeros((4, p), F32)], axis=0)
    den = ar * ar + ai * ai
    cr = ((l1r - 1.0) * ar + l1i * ai) / den
    ci = (l1i * ar - (l1r - 1.0) * ai) / den
    btr, bti = bt_ref[gi, 0], bt_ref[gi, 1]
    bbr = btr * cr - bti * ci
    bbi = btr * ci + bti * cr
    jrow = lax.broadcasted_iota(jnp.int32, (L, p), 0).astype(F32)
    pr, pi = powers(ar, ai, jrow)
    for s in range(L):
        qr, qi = pr[L - 1 - s:L - s], pi[L - 1 - s:L - s]
        w_ref[gi, 0, s * hg:(s + 1) * hg, :] = bbr * qr - bbi * qi
        w_ref[gi, 1, s * hg:(s + 1) * hg, :] = bbr * qi + bbi * qr
    jl = (lax.broadcasted_iota(jnp.int32, (p, L * hg), 1) // hg).astype(F32)
    lpr, lpi = powers(arc, aic, jl)
    ctr, cti = ct_ref[gi, 0], ct_ref[gi, 1]
    cjr = ctr * lpr - cti * lpi
    cji = ctr * lpi + cti * lpr
    c1r, c1i = powers(arc, aic, 1.0)
    vt_ref[gi, 0] = cjr * c1r - cji * c1i
    vt_ref[gi, 1] = -(cjr * c1i + cji * c1r)
    kt = (jnp.dot(bbr, cjr, preferred_element_type=F32, precision=_HI)
          - jnp.dot(bbi, cji, preferred_element_type=F32, precision=_HI))
    n = L * hg
    ri = lax.broadcasted_iota(jnp.int32, (n, n), 0)
    cidx = lax.broadcasted_iota(jnp.int32, (n, n), 1)
    dtile = d_ref[gi]
    for s in range(L):
        shift = (cidx - ri == s * hg).astype(F32)
        blk = jnp.dot(kt, shift, preferred_element_type=F32, precision=_HI)
        rr = lax.broadcasted_iota(jnp.int32, (hg, n), 0) + s * hg
        cc = lax.broadcasted_iota(jnp.int32, (hg, n), 1)
        mt_ref[gi, s * hg:(s + 1) * hg, :] = blk + jnp.where(rr == cc, dtile, 0.0)


def _s5prep(arow, acol, ldt, bt, ct, dt_tiled):
    g = arow.shape[0]
    L, hg, p = S5_CHUNK, S5_GROUP, S5_STATE
    n = L * hg
    i3 = lambda i: (i, 0, 0)
    i4 = lambda i: (i, 0, 0, 0)
    gb = 4 if g % 4 == 0 else 1
    return pl.pallas_call(
        _s5prep_kernel,
        grid=(g // gb,),
        in_specs=[pl.BlockSpec((gb, 2, p), i3), pl.BlockSpec((gb, p, 2), i3), pl.BlockSpec((gb, 1, 1), i3),
                  pl.BlockSpec((gb, 2, hg, p), i4), pl.BlockSpec((gb, 2, p, n), i4),
                  pl.BlockSpec((gb, 1, n), i3)],
        out_specs=[pl.BlockSpec((gb, n, n), i3), pl.BlockSpec((gb, 2, n, p), i4),
                   pl.BlockSpec((gb, 2, p, n), i4), pl.BlockSpec((gb, SUBLANES, p), i3)],
        out_shape=[jax.ShapeDtypeStruct((g, n, n), F32), jax.ShapeDtypeStruct((g, 2, n, p), F32),
                   jax.ShapeDtypeStruct((g, 2, p, n), F32), jax.ShapeDtypeStruct((g, SUBLANES, p), F32)],
        compiler_params=_cparams(("parallel",)),
        name="s5prep",
    )(arow, acol, ldt, bt, ct, dt_tiled)


def _s5in_kernel(ut_ref, perm_ref, o_ref):
    g, nc, n = o_ref.shape[1:]
    hg, L = S5_GROUP, S5_CHUNK
    sh = perm_ref.shape[0]
    nseg, ncs = ut_ref.shape[2] // sh, sh // L
    ups = [jnp.dot(ut_ref[0, :, k * sh:(k + 1) * sh], perm_ref[...], preferred_element_type=F32)
           for k in range(nseg)]
    for gi in range(g):
        rows = slice(gi * hg, (gi + 1) * hg)
        ugt = jnp.concatenate(
            [jnp.concatenate([up[rows, s * ncs:(s + 1) * ncs] for up in ups], axis=1) for s in range(L)], axis=0)
        o_ref[0, gi] = ugt.T.astype(o_ref.dtype)


def _s5in(ut, perm):
    b, w, s = ut.shape
    g, nc, n = w // S5_GROUP, s // S5_CHUNK, S5_CHUNK * S5_GROUP
    return pl.pallas_call(
        _s5in_kernel,
        grid=(b,),
        in_specs=[pl.BlockSpec((1, w, s), lambda i: (i, 0, 0)),
                  pl.BlockSpec(perm.shape, lambda i: (0, 0), pipeline_mode=pl.Buffered(1))],
        out_specs=pl.BlockSpec((1, g, None, nc, n), lambda i: (i // SUBLANES, 0, i % SUBLANES, 0, 0)),
        out_shape=jax.ShapeDtypeStruct((b // SUBLANES, g, SUBLANES, nc, n), BF16),
        compiler_params=_cparams(("parallel",)),
        name="s5in",
    )(ut, perm)


def _s5out_kernel(y_ref, permt_ref, o_ref, ypt_ref):
    g, nc, n = y_ref.shape[1:]
    hg, L = S5_GROUP, S5_CHUNK
    nseg, sh = ypt_ref.shape[0], ypt_ref.shape[2]
    ncs = sh // L
    for gi in range(g):
        ygt = y_ref[0, gi].astype(F32).T
        for t in range(L):
            blk = ygt[t * hg:(t + 1) * hg, :].astype(BF16)
            for k in range(nseg):
                ypt_ref[k, gi * hg:(gi + 1) * hg, t * ncs:(t + 1) * ncs] = blk[:, k * ncs:(k + 1) * ncs]
    yt = jnp.concatenate([jnp.dot(ypt_ref[k], permt_ref[...], preferred_element_type=F32)
                          for k in range(nseg)], axis=1)
    o_ref[0] = yt.T.astype(o_ref.dtype)


def _s5out(y5, permt):
    nbh, g, _, nc, n = y5.shape
    b, s, w = nbh * SUBLANES, nc * S5_CHUNK, g * S5_GROUP
    return pl.pallas_call(
        _s5out_kernel,
        grid=(b,),
        in_specs=[pl.BlockSpec((1, g, None, nc, n), lambda i: (i // SUBLANES, 0, i % SUBLANES, 0, 0)),
                  pl.BlockSpec(permt.shape, lambda i: (0, 0), pipeline_mode=pl.Buffered(1))],
        out_specs=pl.BlockSpec((1, s, w), lambda i: (i, 0, 0)),
        out_shape=jax.ShapeDtypeStruct((b, s, w), BF16),
        scratch_shapes=[pltpu.VMEM((s // permt.shape[0], w, permt.shape[0]), BF16)],
        compiler_params=_cparams(("parallel",)),
        name="s5out",
    )(y5, permt)


def _s5chunk_kernel(u_ref, mt_ref, w_ref, wsw_ref, vt_ref, lam_ref, y_ref, hl_ref, t1_ref, t2_ref, hs_ref):
    gb = u_ref.shape[1]
    nc = u_ref.shape[2] // SUBLANES
    p = S5_STATE
    lane = lax.broadcasted_iota(jnp.int32, (SUBLANES, 2 * p), 1)
    la, lb = [], []
    for gi in range(gb):
        u = u_ref[0, gi]
        t1_ref[gi] = jnp.dot(u, w_ref[gi].astype(BF16), preferred_element_type=F32)
        t2_ref[gi] = jnp.dot(u, wsw_ref[gi].astype(BF16), preferred_element_type=F32)
        la.append(jnp.broadcast_to(lam_ref[gi, 0:1], (SUBLANES, 2 * p)))
        li2 = jnp.broadcast_to(lam_ref[gi, 1:2], (SUBLANES, 2 * p))
        lb.append(jnp.where(lane < p, -li2, li2))

    def step(c, h):
        out = []
        for gi in range(gb):
            a, b = h[2 * gi], h[2 * gi + 1]
            s1 = t1_ref[gi, pl.ds(c, SUBLANES, stride=nc), :]
            s2 = t2_ref[gi, pl.ds(c, SUBLANES, stride=nc), :]
            hs_ref[gi, pl.ds(pl.multiple_of(c * SUBLANES, SUBLANES), SUBLANES), :] = a
            out += [a * la[gi] + b * lb[gi] + s1, b * la[gi] - a * lb[gi] + s2]
        return tuple(out)

    z = jnp.zeros((SUBLANES, 2 * p), F32)
    h = lax.fori_loop(0, nc, step, (z,) * (2 * gb), unroll=8)
    for gi in range(gb):
        hl_ref[0, gi] = h[2 * gi]
        mt = mt_ref[gi].astype(BF16)
        vt = vt_ref[gi].astype(BF16)
        for bi in range(SUBLANES):
            rows = slice(bi * nc, (bi + 1) * nc)
            hb = hs_ref[gi, pl.ds(bi, nc, stride=SUBLANES), :].astype(BF16)
            y = (jnp.dot(u_ref[0, gi, rows, :], mt, preferred_element_type=F32)
                 + jnp.dot(hb, vt, preferred_element_type=F32))
            y_ref[0, gi, rows, :] = y.astype(y_ref.dtype)


def _s5chunk(u4, mt, wcat, wsw, vtcat, lam2):
    nbh, g, rows, n = u4.shape
    p2 = 2 * S5_STATE
    gb = 2 if g % 2 == 0 else 1
    um = lambda b, gi: (b, gi, 0, 0)
    g3 = lambda b, gi: (gi, 0, 0)
    return pl.pallas_call(
        _s5chunk_kernel,
        grid=(nbh, g // gb),
        in_specs=[pl.BlockSpec((1, gb, rows, n), um), pl.BlockSpec((gb, n, n), g3),
                  pl.BlockSpec((gb, n, p2), g3), pl.BlockSpec((gb, n, p2), g3), pl.BlockSpec((gb, p2, n), g3),
                  pl.BlockSpec((gb, SUBLANES, p2), g3)],
        out_specs=[pl.BlockSpec((1, gb, rows, n), um), pl.BlockSpec((1, gb, SUBLANES, p2), um)],
        out_shape=[jax.ShapeDtypeStruct(u4.shape, BF16),
                   jax.ShapeDtypeStruct((nbh, g, SUBLANES, p2), F32)],
        scratch_shapes=[pltpu.VMEM((gb, rows, p2), F32), pltpu.VMEM((gb, rows, p2), F32),
                        pltpu.VMEM((gb, rows, p2), F32)],
        compiler_params=_cparams(("parallel", "parallel")),
        name="s5chunk",
    )(u4, mt, wcat, wsw, vtcat, lam2)


def _s5step_kernel(u_ref, h0_ref, bb_ref, lam_ref, c_ref, d_ref, y_ref, h_ref):
    u = u_ref[0]
    h0r, h0i = h0_ref[0, 0], h0_ref[0, 1]
    l1r, l1i = lam_ref[0, 2:3], lam_ref[0, 3:4]
    bur = jnp.dot(u, bb_ref[0, 0], preferred_element_type=F32, precision=_HI)
    bui = jnp.dot(u, bb_ref[0, 1], preferred_element_type=F32, precision=_HI)
    hr = l1r * h0r - l1i * h0i + bur
    hi = l1r * h0i + l1i * h0r + bui
    h_ref[0, 0] = hr
    h_ref[0, 1] = hi
    y = _nt_dot(hr, c_ref[0, 0], precision=_HI) - _nt_dot(hi, c_ref[0, 1], precision=_HI)
    y_ref[0] = y + d_ref[0] * u


def _s5step(u3, h0, bb, lam, c2, d3):
    g, bd, hg = u3.shape
    p = S5_STATE
    i3 = lambda i: (i, 0, 0)
    i4 = lambda i: (i, 0, 0, 0)
    return pl.pallas_call(
        _s5step_kernel,
        grid=(g,),
        in_specs=[pl.BlockSpec((1, bd, hg), i3), pl.BlockSpec((1, 2, bd, p), i4),
                  pl.BlockSpec((1, 2, hg, p), i4), pl.BlockSpec((1, SUBLANES, p), i3),
                  pl.BlockSpec((1, 2, hg, p), i4), pl.BlockSpec((1, 1, hg), i3)],
        out_specs=[pl.BlockSpec((1, bd, hg), i3), pl.BlockSpec((1, 2, bd, p), i4)],
        out_shape=[jax.ShapeDtypeStruct((g, bd, hg), F32), jax.ShapeDtypeStruct((g, 2, bd, p), F32)],
        compiler_params=_cparams(("parallel",)),
        name="s5step",
    )(u3, h0, bb, lam, c2, d3)


def _postmix_kernel(x_ref, a_ref, y_ref, wglu_ref, bglu_ref, wout_ref, g2_ref, wr_ref, wrf_ref, br_ref,
                    x1_ref, h2_ref, comb_ref, *, precise, parts):
    th = x_ref.shape[0] // parts
    for part in range(parts):
        rows = slice(part * th, (part + 1) * th)
        _postmix_rows(x_ref, a_ref, y_ref, wglu_ref, bglu_ref, wout_ref, g2_ref, wr_ref, wrf_ref, br_ref,
                      x1_ref, h2_ref, comb_ref, rows, precise)


def _postmix_rows(x_ref, a_ref, y_ref, wglu_ref, bglu_ref, wout_ref, g2_ref, wr_ref, wrf_ref, br_ref,
                  x1_ref, h2_ref, comb_ref, rows, precise):
    wa = a_ref.shape[-1]

    def mm(act, w):
        if precise:
            return jnp.dot(act, w, preferred_element_type=F32, precision=_HI)
        return jnp.dot(act.astype(BF16), w, preferred_element_type=F32)

    y = jax.nn.gelu(y_ref[rows, :].astype(F32))
    z = mm(y, wglu_ref[...]) + bglu_ref[...]
    s5o = y * jax.nn.sigmoid(z)
    mix = mm(a_ref[rows, :].astype(F32) if precise else a_ref[rows, :], wout_ref[0:wa, :])
    mix += mm(s5o, wout_ref[wa:, :])
    x1 = x_ref[rows, :] + mix
    x1_ref[rows, :] = x1
    h2 = x1 * lax.rsqrt(jnp.mean(x1 * x1, axis=-1, keepdims=True) + EPS) * g2_ref[...]
    h2_ref[rows, :] = h2.astype(h2_ref.dtype)
    if precise:
        logits = jnp.dot(h2, wrf_ref[...], preferred_element_type=F32, precision=_HI)
    else:
        hi, lo = _split2(h2)
        hh = jnp.dot(hi, wr_ref[...], preferred_element_type=F32)
        logits = hh[:, :LANES] + hh[:, LANES:] + jnp.dot(lo, wr_ref[:, :LANES], preferred_element_type=F32)
    logits = logits + br_ref[...]
    ne = N_GROUPS_MOE * EXP_PER_GROUP
    lane = lax.broadcasted_iota(jnp.int32, logits.shape, 1).astype(F32)
    big = jnp.float32(1 << 20)
    gmask = (lane >= ne) & (lane < ne + N_GROUPS_MOE)
    gl = jnp.where(gmask, logits, NEG_INF)
    gmax = jnp.max(gl, axis=-1, keepdims=True)
    gidx = jnp.min(jnp.where(gl == gmax, lane, big), axis=-1, keepdims=True) - ne
    g_w = 1.0 / jnp.sum(jnp.where(gmask, jnp.exp(logits - gmax), 0.0), axis=-1, keepdims=True)
    lo = gidx * EXP_PER_GROUP
    el = jnp.where((lane >= lo) & (lane < lo + EXP_PER_GROUP), logits, NEG_INF)
    e1 = jnp.max(el, axis=-1, keepdims=True)
    i1 = jnp.min(jnp.where(el == e1, lane, big), axis=-1, keepdims=True)
    el2 = jnp.where(lane == i1, NEG_INF, el)
    e2 = jnp.max(el2, axis=-1, keepdims=True)
    i2 = jnp.min(jnp.where(el2 == e2, lane, big), axis=-1, keepdims=True)
    r = jnp.exp(e2 - e1)
    w1 = g_w / (1.0 + r)
    w2 = g_w * r / (1.0 + r)
    comb_ref[rows, :] = (jnp.where(lane == i1, w1, 0.0) + jnp.where(lane == i2, w2, 0.0)
                         + jnp.where(lane == GID_LANE, gidx, 0.0))


def _postmix(x2, attn, ys5, wglu, bglu, wout, ln2_g, w_router, b_router, tm, precise):
    t, d = x2.shape
    wa = attn.shape[1]
    ws = ys5.shape[1]
    row = lambda i: (i, 0)
    fixed = lambda i: (0, 0)
    wr_hi, wr_lo = _split2(w_router)
    wr_hl = jnp.concatenate([wr_hi, wr_lo], axis=1)
    parts = 2 if tm % 512 == 0 else 1
    return pl.pallas_call(
        functools.partial(_postmix_kernel, precise=precise, parts=parts),
        grid=(t // tm,),
        in_specs=[pl.BlockSpec((tm, d), row), pl.BlockSpec((tm, wa), row), pl.BlockSpec((tm, ws), row),
                  pl.BlockSpec((ws, ws), fixed), pl.BlockSpec((1, ws), fixed),
                  pl.BlockSpec((wa + ws, d), fixed), pl.BlockSpec((1, d), fixed),
                  pl.BlockSpec((d, 2 * LANES), fixed), pl.BlockSpec((d, LANES), fixed),
                  pl.BlockSpec((1, LANES), fixed)],
        out_specs=[pl.BlockSpec((tm, d), row), pl.BlockSpec((tm, d), row), pl.BlockSpec((tm, LANES), row)],
        out_shape=[jax.ShapeDtypeStruct((t, d), F32), jax.ShapeDtypeStruct((t, d), BF16),
                   jax.ShapeDtypeStruct((t, LANES), F32)],
        compiler_params=_cparams(("parallel",)),
        name="postmix",
    )(x2, attn, ys5, wglu, bglu, wout, ln2_g, wr_hl, w_router, b_router)


def _moe_kernel(h_ref, x1_ref, comb_ref, wg_ref, wu_ref, wd_ref, gf_ref, o_ref, acc_ref):
    e = pl.program_id(1)

    @pl.when(e == 0)
    def _():
        acc_ref[...] = jnp.zeros_like(acc_ref)

    h = h_ref[...]
    comb = comb_ref[...]
    lane = lax.broadcasted_iota(jnp.int32, comb.shape, 1)
    eb = wg_ref.shape[0]
    acc = acc_ref[...]
    for k in range(eb):
        he = jax.nn.silu(jnp.dot(h, wg_ref[k], preferred_element_type=F32))
        he = he * jnp.dot(h, wu_ref[k], preferred_element_type=F32)
        ce = jnp.sum(jnp.where(lane == e * eb + k, comb, 0.0), axis=-1, keepdims=True)
        acc = acc + jnp.dot((he * ce).astype(BF16), wd_ref[k], preferred_element_type=F32)
    acc_ref[...] = acc

    @pl.when(e == pl.num_programs(1) - 1)
    def _():
        x2 = x1_ref[...] + acc_ref[...]
        o_ref[...] = x2 * lax.rsqrt(jnp.mean(x2 * x2, axis=-1, keepdims=True) + EPS) * gf_ref[...]


def _moe(h2, x1, comb, wg_bf, wu_bf, wd_bf, lnf_g, tm):
    t, d = x1.shape
    ne, _, de = wg_bf.shape
    row = lambda i, e: (i, 0)
    eb = 4 if ne % 4 == 0 else 1
    return pl.pallas_call(
        _moe_kernel,
        grid=(t // tm, ne // eb),
        in_specs=[pl.BlockSpec((tm, d), row), pl.BlockSpec((tm, d), row), pl.BlockSpec((tm, LANES), row),
                  pl.BlockSpec((eb, d, de), lambda i, e: (e, 0, 0)),
                  pl.BlockSpec((eb, d, de), lambda i, e: (e, 0, 0)),
                  pl.BlockSpec((eb, de, d), lambda i, e: (e, 0, 0)),
                  pl.BlockSpec((1, d), lambda i, e: (0, 0))],
        out_specs=pl.BlockSpec((tm, d), row),
        out_shape=jax.ShapeDtypeStruct((t, d), F32),
        scratch_shapes=[pltpu.VMEM((tm, d), F32)],
        compiler_params=_cparams(("parallel", "arbitrary")),
        name="moe",
    )(h2, x1, comb, wg_bf, wu_bf, wd_bf, lnf_g)


def _moe_grouped_kernel(h_ref, x1_ref, comb_ref, ut_ref, lt_ref, wg_ref, wu_ref, wd_ref, gf_ref, o_ref,
                        rrow_ref, rcol_ref, chi_ref, clo_ref, cnt_ref):
    g = pl.program_id(1)
    tm, d = h_ref.shape
    gf32 = g.astype(F32)

    @pl.when(g == 0)
    def _():
        comb = comb_ref[...]
        combt = comb.T
        gidc = comb[:, GID_LANE:GID_LANE + 1]
        gidr = combt[GID_LANE:GID_LANE + 1, :]
        lane = lax.broadcasted_iota(jnp.int32, comb.shape, 1).astype(F32)
        sub = lax.broadcasted_iota(jnp.int32, (SUBLANES, tm), 0).astype(F32)
        ohc = lane == gidc
        ohr = sub == gidr
        cntc = jnp.dot(lt_ref[...], jnp.where(ohc, 1.0, 0.0).astype(BF16), preferred_element_type=F32)
        cntr = jnp.dot(jnp.where(ohr, 1.0, 0.0).astype(BF16), ut_ref[...], preferred_element_type=F32)
        rankc = jnp.sum(jnp.where(ohc, cntc, 0.0), axis=-1, keepdims=True) - 1.0
        rankr = jnp.sum(jnp.where(ohr, cntr, 0.0), axis=0, keepdims=True) - 1.0
        rcol_ref[...] = jnp.broadcast_to(rankc, rcol_ref.shape)
        rrow_ref[...] = jnp.concatenate([rankr, gidr, jnp.zeros((SUBLANES - 2, tm), F32)], axis=0)
        chi, clo = _split2(combt)
        chi_ref[...] = chi
        clo_ref[...] = clo
        for r in range(N_GROUPS_MOE):
            cnt_ref[r] = jnp.max(cntr[r:r + 1, :]).astype(jnp.int32)
        o_ref[...] = jnp.zeros_like(o_ref)

    posr = jnp.where(rrow_ref[1:2, :] == gf32, rrow_ref[0:1, :], -1.0)
    posc = jnp.where(comb_ref[:, GID_LANE:GID_LANE + 1] == gf32, rcol_ref[:, 0:1], -1.0)
    n_rows = cnt_ref[g]
    big = 2 * MOE_CHUNK
    mid = big + MOE_CHUNK // 2
    use_mid = jnp.logical_and(n_rows > big, n_rows <= mid)
    n_big = jnp.where(use_mid, 0, n_rows // big)
    n_mid = jnp.where(use_mid, 1, 0)
    n_small = jnp.where(use_mid, 0, (n_rows - n_big * big + (MOE_CHUNK - 1)) // MOE_CHUNK)

    def chunk(row0, ch):
        base = row0.astype(F32)
        rid = lax.broadcasted_iota(jnp.int32, (ch, tm), 0).astype(F32) + base
        cid = lax.broadcasted_iota(jnp.int32, (tm, ch), 1).astype(F32) + base
        sel = jnp.where(posr == rid, 1.0, 0.0).astype(BF16)
        selt = jnp.where(posc == cid, 1.0, 0.0).astype(BF16)
        xg = jnp.dot(sel, h_ref[...], preferred_element_type=F32).astype(BF16)
        cg = _nt_dot(sel, chi_ref[...]) + _nt_dot(sel, clo_ref[...])
        lane = lax.broadcasted_iota(jnp.int32, cg.shape, 1).astype(F32)
        z = jnp.zeros((ch, d), F32)
        for e in range(EXP_PER_GROUP):
            he = jax.nn.silu(jnp.dot(xg, wg_ref[e], preferred_element_type=F32))
            he = he * jnp.dot(xg, wu_ref[e], preferred_element_type=F32)
            ce = jnp.sum(jnp.where(lane == gf32 * EXP_PER_GROUP + e, cg, 0.0), axis=-1, keepdims=True)
            z = z + jnp.dot((he * ce).astype(BF16), wd_ref[e], preferred_element_type=F32)
        o_ref[...] += jnp.dot(selt, z.astype(BF16), preferred_element_type=F32)

    def big_body(k, carry):
        chunk(k * big, big)
        return carry

    def mid_body(k, carry):
        chunk(k * mid, mid)
        return carry

    def small_body(k, carry):
        chunk(n_big * big + k * MOE_CHUNK, MOE_CHUNK)
        return carry

    lax.fori_loop(0, n_big, big_body, 0)
    lax.fori_loop(0, n_mid, mid_body, 0)
    lax.fori_loop(0, n_small, small_body, 0)

    @pl.when(g == pl.num_programs(1) - 1)
    def _():
        x2 = x1_ref[...] + o_ref[...]
        o_ref[...] = x2 * lax.rsqrt(jnp.mean(x2 * x2, axis=-1, keepdims=True) + EPS) * gf_ref[...]


def _moe_grouped(h2, x1, comb, wg_bf, wu_bf, wd_bf, lnf_g, tm):
    t, d = x1.shape
    ne, _, de = wg_bf.shape
    ng = ne // EXP_PER_GROUP
    tri = jnp.arange(tm)[:, None] <= jnp.arange(tm)[None, :]
    ut = tri.astype(BF16)
    lt = tri.T.astype(BF16)
    once = pl.Buffered(1)
    tile = lambda i, g: (i, 0)
    fixed = lambda i, g: (0, 0)
    wmap = lambda i, g: (g, 0, 0)
    return pl.pallas_call(
        _moe_grouped_kernel,
        grid=(t // tm, ng),
        in_specs=[pl.BlockSpec((tm, d), tile), pl.BlockSpec((tm, d), tile),
                  pl.BlockSpec((tm, LANES), tile),
                  pl.BlockSpec((tm, tm), fixed, pipeline_mode=once),
                  pl.BlockSpec((tm, tm), fixed, pipeline_mode=once),
                  pl.BlockSpec((EXP_PER_GROUP, d, de), wmap),
                  pl.BlockSpec((EXP_PER_GROUP, d, de), wmap),
                  pl.BlockSpec((EXP_PER_GROUP, de, d), wmap),
                  pl.BlockSpec((1, d), fixed)],
        out_specs=pl.BlockSpec((tm, d), tile),
        out_shape=jax.ShapeDtypeStruct((t, d), F32),
        scratch_shapes=[pltpu.VMEM((SUBLANES, tm), F32), pltpu.VMEM((tm, LANES), F32),
                        pltpu.VMEM((LANES, tm), BF16), pltpu.VMEM((LANES, tm), BF16),
                        pltpu.SMEM((N_GROUPS_MOE,), jnp.int32)],
        compiler_params=_cparams(("parallel", "arbitrary")),
        name="moe_grouped",
    )(h2, x1, comb, ut, lt, wg_bf, wu_bf, wd_bf, lnf_g)


def _tile(n, want):
    t = min(n, want)
    while n % t:
        t //= 2
    return t


def kernel(x_prompt, x_sample, cache_k, cache_v, state_s5_re, state_s5_im, page_table, ln1_g, w_in, lambda_q1, lambda_k1, lambda_q2, lambda_k2, subln_g, s5_a_re, s5_a_im, s5_log_dt, s5_b_re, s5_b_im, s5_c_re, s5_c_im, s5_d, w_glu, b_glu, w_out, ln2_g, w_router_group, b_router_group, w_router_expert, b_router_expert, w_gate, w_up, w_down, ln_f_g):
    depth = ln1_g.shape[0]
    assert depth == 1, "single-layer step"
    b, s, d = x_prompt.shape
    bd, ds, _ = x_sample.shape
    assert ds == 1
    n_pages = page_table.shape[1]
    page = cache_k.shape[2]
    past_len = n_pages * page
    wq = w_in.shape[2] // 4
    n_heads = wq // V_DIM
    g = s5_a_re.shape[1]
    p = S5_STATE
    hg = S5_GROUP
    L = S5_CHUNK
    n = L * hg
    assert b % SUBLANES == 0 and s % L == 0 and g * hg == wq
    lambda_init = 0.8 - 0.6 * math.exp(-0.3 * 0)

    w_in_bf = w_in[0].astype(BF16)
    wglu_bf = w_glu[0].astype(BF16)
    wout_bf = w_out[0].astype(BF16)
    wg_bf, wu_bf, wd_bf = (w[0].astype(BF16) for w in (w_gate, w_up, w_down))
    ne = w_gate.shape[1]
    w_router = jnp.concatenate(
        [jnp.transpose(w_router_expert[0], (1, 0, 2)).reshape(d, ne), w_router_group[0],
         jnp.zeros((d, LANES - ne - N_GROUPS_MOE), F32)], axis=1)
    b_router = jnp.concatenate([b_router_expert[0].reshape(ne), b_router_group[0],
                                jnp.zeros((LANES - ne - N_GROUPS_MOE,), F32)])[None, :]
    lam4 = jnp.concatenate([lambda_q1, lambda_k1, lambda_q2, lambda_k2], axis=0)
    ln1 = ln1_g[0][None, :]
    ln2 = ln2_g[0][None, :]
    lnf = ln_f_g[None, :]
    subg = subln_g[0][None, :]
    bglu = b_glu[0][None, :]

    arow = jnp.stack([s5_a_re[0], s5_a_im[0]], axis=1)
    acol = jnp.stack([s5_a_re[0], s5_a_im[0]], axis=2)
    ldt = s5_log_dt[0].reshape(g, 1, 1)
    bt = jnp.stack([jnp.swapaxes(s5_b_re[0], 1, 2), jnp.swapaxes(s5_b_im[0], 1, 2)], axis=1)
    ct = jnp.stack([jnp.tile(jnp.swapaxes(s5_c_re[0], 1, 2), (1, 1, L)),
                    jnp.tile(jnp.swapaxes(s5_c_im[0], 1, 2), (1, 1, L))], axis=1)
    d_g = s5_d[0].reshape(g, 1, hg)
    mt, w_s5, vt, lam_s5 = _s5prep(arow, acol, ldt, bt, ct, jnp.tile(d_g, (1, 1, L)))

    tm = _tile(b * s, TOKEN_TILE)
    x2 = x_prompt.reshape(b * s, d)
    pos_p = jnp.arange(s, dtype=jnp.int32)
    rc, rs1, rs2 = _rope_tables(pos_p)
    rct, rst = _rope_tables_t(pos_p)
    w_qv_bf = jnp.concatenate([w_in_bf[:, :wq], w_in_bf[:, 2 * wq:3 * wq]], axis=1)
    wkut_bf = jnp.concatenate([w_in_bf[:, wq:2 * wq], w_in_bf[:, 3 * wq:]], axis=1).T
    tq = _tile(s, ATTN_TILE)
    q, vf, vb, ut, ktf, ktb = _inproj_prompt(x_prompt, ln1, w_qv_bf, wkut_bf, rc, rs1, rs2, rct, rst,
                                             _tile(s, tm), tq)
    attn = _attn(q, ktb, vb, lam4, subg, n_heads, lambda_init)
    kf = jnp.transpose(ktf.reshape(b, 2 * n_heads, HEAD_DIM, s), (0, 3, 1, 2))
    nbh, nc = b // SUBLANES, s // L
    seg = s // 2 if (s // 2) % (L * LANES // 2) == 0 else s
    tok = jnp.arange(seg)
    perm = (((tok % L) * (seg // L) + tok // L)[:, None] == tok[None, :]).astype(BF16)
    u5 = _s5in(ut, perm)
    wcat = jnp.concatenate([w_s5[:, 0], w_s5[:, 1]], axis=-1)
    wsw = jnp.concatenate([w_s5[:, 1], w_s5[:, 0]], axis=-1)
    vtcat = jnp.concatenate([vt[:, 0], vt[:, 1]], axis=1)
    lam2 = jnp.concatenate([lam_s5, lam_s5], axis=-1)
    y5, hl = _s5chunk(u5.reshape(nbh, g, SUBLANES * nc, n), mt, wcat, wsw, vtcat, lam2)
    ys5 = _s5out(y5.reshape(nbh, g, SUBLANES, nc, n), perm.T).reshape(b * s, wq)
    hl = hl.transpose(0, 2, 1, 3).reshape(b, g, 2, p).transpose(2, 0, 1, 3)
    x1, h2, comb = _postmix(x2, attn.reshape(b * s, wq), ys5, wglu_bf, bglu, wout_bf, ln2,
                            w_router, b_router, tm, False)
    y_prompt = _moe_grouped(h2, x1, comb, wg_bf, wu_bf, wd_bf, lnf, _tile(b * s, MOE_TILE)).reshape(b, s, d)

    xs2 = x_sample.reshape(bd, d)
    pos_s = jnp.full((bd,), past_len, jnp.int32)
    sc, ss1, ss2 = _rope_tables(pos_s)
    qs, kfs, vfs, us = _inproj(xs2, ln1, w_in[0], sc, ss1, ss2, bd)
    cache_kt = jnp.transpose(cache_k[0], (0, 2, 3, 1)).reshape(-1, wq, page)
    cache_vr = cache_v[0].reshape(-1, page * n_heads, V_DIM)
    qcol = jnp.broadcast_to(qs[:, :, None], (bd, wq, LANES))
    kn_col = jnp.broadcast_to(kfs[:, :, None], (bd, wq, LANES))
    vn2 = jnp.repeat(vfs.reshape(bd, n_heads, V_DIM), 2, axis=1)
    pp = _tile(n_pages, DECODE_PAGES)
    nb_dec = _tile(bd, DECODE_ROWS)
    attn_s = _decode_attn(page_table, qcol, kn_col, vn2, lam4, subg, cache_kt, cache_vr, pp, nb_dec,
                          lambda_init)
    u3 = us.reshape(bd, g, hg).transpose(1, 0, 2)
    h0 = jnp.stack([state_s5_re[0], state_s5_im[0]]).transpose(2, 0, 1, 3)
    bb = w_s5[:, :, (L - 1) * hg:, :]
    c2 = jnp.stack([s5_c_re[0], s5_c_im[0]], axis=1)
    ys3, hs = _s5step(u3, h0, bb, lam_s5, c2, d_g)
    ys_s5 = ys3.transpose(1, 0, 2).reshape(bd, wq)
    x1s, h2s, combs = _postmix(xs2, attn_s.reshape(bd, wq), ys_s5, w_glu[0], bglu, w_out[0], ln2,
                               w_router, b_router, bd, True)
    y_sample = _moe(h2s, x1s, combs, wg_bf, wu_bf, wd_bf, lnf, bd).reshape(bd, 1, d)
    hs = hs.transpose(1, 2, 0, 3)

    return (y_prompt, y_sample,
            kf.reshape(1, b, s, 2 * n_heads, HEAD_DIM), vf.reshape(1, b, s, n_heads, V_DIM),
            hl[0][None], hl[1][None],
            kfs.reshape(1, bd, 1, 2 * n_heads, HEAD_DIM), vfs.reshape(1, bd, 1, n_heads, V_DIM),
            hs[0][None], hs[1][None])
```

```python
import functools
import math

import jax
import jax.numpy as jnp
from jax import lax
from jax.experimental import pallas as pl
from jax.experimental.pallas import tpu as pltpu

F32 = jnp.float32
BF16 = jnp.bfloat16

HEAD_DIM = 64
V_DIM = 128
ROT_DIM = 16
ROPE_THETA = 500000.0
S5_GROUP = 16
S5_STATE = 64
S5_CHUNK = 16
SUBLANES = 8
N_GROUPS_MOE = 4
EXP_PER_GROUP = 8
EPS = 1e-5
NEG_INF = -1e30
LANES = 128
GID_LANE = LANES - 1
MOE_CHUNK = 128
VMEM_LIMIT = 56 * 1024 * 1024
TOKEN_TILE = 1024
ATTN_TILE = 256
MOE_TILE = 1024
DECODE_PAGES = 16
DECODE_ROWS = 2

_HI = lax.Precision.HIGHEST


def _cparams(sem):
    return pltpu.CompilerParams(dimension_semantics=sem, vmem_limit_bytes=VMEM_LIMIT)


def _nt_dot(a, b, **kw):
    return lax.dot_general(a, b, (((1,), (1,)), ((), ())), preferred_element_type=F32, **kw)


def _inproj_kernel(x_ref, g_ref, w_ref, c_ref, s1_ref, s2_ref, q_ref, kf_ref, vf_ref, u_ref):
    x = x_ref[...]
    ms = jnp.mean(x * x, axis=-1, keepdims=True)
    hn = x * lax.rsqrt(ms + EPS) * g_ref[...]
    proj = jnp.dot(hn, w_ref[...], preferred_element_type=F32, precision=_HI)
    w = q_ref.shape[-1]
    c = c_ref[...]
    s1 = s1_ref[...]
    s2 = s2_ref[...]
    for j in range(w // LANES):
        sl = slice(j * LANES, (j + 1) * LANES)
        zq = proj[:, j * LANES:(j + 1) * LANES]
        zk = proj[:, w + j * LANES:w + (j + 1) * LANES]
        rq = zq * c + pltpu.roll(zq, ROT_DIM // 2, 1) * s1 + pltpu.roll(zq, LANES - ROT_DIM // 2, 1) * s2
        rk = zk * c + pltpu.roll(zk, ROT_DIM // 2, 1) * s1 + pltpu.roll(zk, LANES - ROT_DIM // 2, 1) * s2
        q_ref[:, sl] = rq * (HEAD_DIM ** -0.5)
        kf_ref[:, sl] = rk
    vf_ref[...] = proj[:, 2 * w:3 * w]
    u_ref[...] = proj[:, 3 * w:4 * w]


def _inproj(x2, ln_g, w_in, rc, rs1, rs2, tm):
    t, d = x2.shape
    w = w_in.shape[1] // 4
    nt = t // tm
    npos = rc.shape[0] // tm
    row = lambda i: (i, 0)
    pos = lambda i: (i % npos, 0)
    fixed = lambda i: (0, 0)
    return pl.pallas_call(
        _inproj_kernel,
        grid=(nt,),
        in_specs=[pl.BlockSpec((tm, d), row), pl.BlockSpec((1, d), fixed),
                  pl.BlockSpec((d, 4 * w), fixed),
                  pl.BlockSpec((tm, LANES), pos), pl.BlockSpec((tm, LANES), pos),
                  pl.BlockSpec((tm, LANES), pos)],
        out_specs=[pl.BlockSpec((tm, w), row)] * 4,
        out_shape=[jax.ShapeDtypeStruct((t, w), F32)] * 4,
        compiler_params=_cparams(("parallel",)),
        name="inproj",
    )(x2, ln_g, w_in, rc, rs1, rs2)


def _inproj_prompt_kernel(x_ref, g_ref, w_ref, wkt_ref, c_ref, s1_ref, s2_ref, ct_ref, st_ref,
                          q_ref, vf_ref, vb_ref, ut_ref, ktf_ref, ktb_ref):
    x = x_ref[0]
    ms = jnp.mean(x * x, axis=-1, keepdims=True)
    hn = (x * lax.rsqrt(ms + EPS) * g_ref[...]).astype(BF16)
    proj = jnp.dot(hn, w_ref[...], preferred_element_type=F32)
    w = q_ref.shape[-1]
    ktu = _nt_dot(wkt_ref[...], hn)
    kt = ktu[:w]
    ut_ref[0] = ktu[w:].astype(ut_ref.dtype)
    c, s1, s2 = c_ref[...], s1_ref[...], s2_ref[...]
    for j in range(w // LANES):
        sl = slice(j * LANES, (j + 1) * LANES)
        zq = proj[:, sl]
        rq = zq * c + pltpu.roll(zq, ROT_DIM // 2, 1) * s1 + pltpu.roll(zq, LANES - ROT_DIM // 2, 1) * s2
        q_ref[0, :, sl] = (rq * (HEAD_DIM ** -0.5)).astype(q_ref.dtype)
    v = proj[:, w:2 * w]
    nh = w // V_DIM
    for h in range(nh):
        vf_ref[0, pl.ds(h, v.shape[0], stride=nh), :] = v[:, h * V_DIM:(h + 1) * V_DIM]
    vb_ref[0] = v.astype(BF16)
    ct, st = ct_ref[...], st_ref[...]
    half = ROT_DIM // 2
    tk = ktb_ref.shape[-1]
    for hc in range(w // HEAD_DIM):
        base = hc * HEAD_DIM
        x1, x2 = kt[base:base + half], kt[base + half:base + ROT_DIM]
        blk = jnp.concatenate([x1 * ct - x2 * st, x2 * ct + x1 * st, kt[base + ROT_DIM:base + HEAD_DIM]], axis=0)
        ktf_ref[0, base:base + HEAD_DIM, :] = blk
        for t in range(blk.shape[1] // tk):
            ktb_ref[0, t, base:base + HEAD_DIM, :] = blk[:, t * tk:(t + 1) * tk].astype(BF16)


def _inproj_prompt(x3, ln_g, w_qv_bf, wkut_bf, rc, rs1, rs2, rct, rst, tm, tk):
    b, s, d = x3.shape
    w = wkut_bf.shape[0] // 2
    half = ROT_DIM // 2
    tok = lambda bi, i: (bi, i, 0)
    pos = lambda bi, i: (i, 0)
    fixed = lambda bi, i: (0, 0)
    nh = w // V_DIM
    outs = [jax.ShapeDtypeStruct((b, s, w), BF16), jax.ShapeDtypeStruct((b, s * nh, V_DIM), F32),
            jax.ShapeDtypeStruct((b, s, w), BF16), jax.ShapeDtypeStruct((b, w, s), BF16),
            jax.ShapeDtypeStruct((b, w, s), F32), jax.ShapeDtypeStruct((b, s // tk, w, tk), BF16)]
    return pl.pallas_call(
        _inproj_prompt_kernel,
        grid=(b, s // tm),
        in_specs=[pl.BlockSpec((1, tm, d), tok), pl.BlockSpec((1, d), fixed),
                  pl.BlockSpec((d, 2 * w), fixed), pl.BlockSpec((2 * w, d), fixed),
                  pl.BlockSpec((tm, LANES), pos), pl.BlockSpec((tm, LANES), pos), pl.BlockSpec((tm, LANES), pos),
                  pl.BlockSpec((half, tm), lambda bi, i: (0, i)), pl.BlockSpec((half, tm), lambda bi, i: (0, i))],
        out_specs=[pl.BlockSpec((1, tm, w), tok), pl.BlockSpec((1, tm * nh, V_DIM), tok),
                   pl.BlockSpec((1, tm, w), tok)]
                  + [pl.BlockSpec((1, w, tm), lambda bi, i: (bi, 0, i))] * 2
                  + [pl.BlockSpec((1, tm // tk, w, tk), lambda bi, i: (bi, i, 0, 0))],
        out_shape=outs,
        compiler_params=_cparams(("parallel", "parallel")),
        name="inproj_prompt",
    )(x3, ln_g, w_qv_bf, wkut_bf, rc, rs1, rs2, rct, rst)


def _rope_tables_t(pos):
    inv = ROPE_THETA ** (-jnp.arange(0, ROT_DIM, 2, dtype=F32) / ROT_DIM)
    ang = inv[:, None] * pos.astype(F32)[None, :]
    return jnp.cos(ang), jnp.sin(ang)


def _rope_tables(pos):
    half = ROT_DIM // 2
    inv = ROPE_THETA ** (-jnp.arange(0, ROT_DIM, 2, dtype=F32) / ROT_DIM)
    ang = pos.astype(F32)[:, None] * inv[None, :]
    cos, sin = jnp.cos(ang), jnp.sin(ang)
    n = pos.shape[0]
    pad = jnp.zeros((n, HEAD_DIM - ROT_DIM), F32)
    c = jnp.concatenate([cos, cos, pad + 1.0], axis=1)
    s1 = jnp.concatenate([jnp.zeros((n, half), F32), sin, pad], axis=1)
    s2 = jnp.concatenate([-sin, jnp.zeros((n, half), F32), pad], axis=1)
    rep = LANES // HEAD_DIM
    return tuple(jnp.tile(a, (1, rep)) for a in (c, s1, s2))


def _diff_lambda(lam_ref, lambda_init):
    l = lam_ref[...]
    a = jnp.sum(l[0:1] * l[1:2], axis=-1, keepdims=True)
    b = jnp.sum(l[2:3] * l[3:4], axis=-1, keepdims=True)
    return jnp.exp(a) - jnp.exp(b) + lambda_init


def _attn_kernel(q_ref, kt_ref, v_ref, lam_ref, g_ref, o_ref, s_ref, m_ref, acc_ref, vext_ref, *, tq, lambda_init):
    nq = q_ref.shape[1] // tq
    nl = tq // LANES
    vext_ref[:, 0:V_DIM] = v_ref[0]
    ones_col = lax.broadcasted_iota(jnp.int32, (v_ref.shape[1], LANES), 1) == 0
    vext_ref[:, V_DIM:] = jnp.where(ones_col, 1.0, 0.0).astype(BF16)
    lam = _diff_lambda(lam_ref, lambda_init)
    lane = lax.broadcasted_iota(jnp.int32, (tq, LANES), 1)
    row = lax.broadcasted_iota(jnp.int32, (tq, tq), 0)
    col = lax.broadcasted_iota(jnp.int32, (tq, tq), 1)
    for qi in range(nq):
        q = q_ref[0, qi * tq:(qi + 1) * tq, :]
        zero = jnp.zeros_like(q)
        qc = (jnp.where(lane < HEAD_DIM, q, zero), jnp.where(lane >= HEAD_DIM, q, zero))
        base = qi * (qi + 1) // 2
        for j in range(qi + 1):
            kt = kt_ref[0, j]
            for c in range(2):
                s = jnp.dot(qc[c], kt, preferred_element_type=F32)
                if j == qi:
                    s = jnp.where(col <= row, s, NEG_INF)
                s_ref[c, base + j] = s
                m = s[:, 0:LANES] if j == 0 else jnp.maximum(m_ref[qi, c], s[:, 0:LANES])
                for t in range(1, nl):
                    m = jnp.maximum(m, s[:, t * LANES:(t + 1) * LANES])
                m_ref[qi, c] = m
        for c in range(2):
            m_ref[qi, c] = jnp.broadcast_to(jnp.max(m_ref[qi, c], axis=-1, keepdims=True), (tq, LANES))
        for j in range(qi + 1):
            v = vext_ref[j * tq:(j + 1) * tq, :]
            for c in range(2):
                m = m_ref[qi, c]
                ps = [jnp.exp(s_ref[c, base + j, :, t * LANES:(t + 1) * LANES] - m) for t in range(nl)]
                pv = jnp.dot(jnp.concatenate(ps, axis=1).astype(BF16), v, preferred_element_type=F32)
                if j == 0:
                    acc_ref[qi, c] = pv
                else:
                    acc_ref[qi, c] += pv
        a0, a1 = acc_ref[qi, 0], acc_ref[qi, 1]
        o = a0[:, :V_DIM] / a0[:, V_DIM:V_DIM + 1] - lam * (a1[:, :V_DIM] / a1[:, V_DIM:V_DIM + 1])
        o = o * lax.rsqrt(jnp.mean(o * o, axis=-1, keepdims=True) + EPS)
        o_ref[0, qi * tq:(qi + 1) * tq, :] = (o * g_ref[...] * (1.0 - lambda_init)).astype(o_ref.dtype)


def _attn(q, kt4, v, lam4, subln_g, n_heads, lambda_init):
    b, s, w = q.shape
    nkt, tq = kt4.shape[1], kt4.shape[3]
    seq = lambda bi, h: (bi, 0, h)
    fixed = lambda bi, h: (0, 0)
    nq = s // tq
    return pl.pallas_call(
        functools.partial(_attn_kernel, tq=tq, lambda_init=lambda_init),
        grid=(b, n_heads),
        in_specs=[pl.BlockSpec((1, s, LANES), seq),
                  pl.BlockSpec((1, nkt, LANES, tq), lambda bi, h: (bi, 0, h, 0)),
                  pl.BlockSpec((1, s, LANES), seq),
                  pl.BlockSpec((4, HEAD_DIM), fixed), pl.BlockSpec((1, V_DIM), fixed)],
        out_specs=pl.BlockSpec((1, s, LANES), seq),
        out_shape=jax.ShapeDtypeStruct((b, s, w), BF16),
        scratch_shapes=[pltpu.VMEM((2, nq * (nq + 1) // 2, tq, tq), F32), pltpu.VMEM((nq, 2, tq, LANES), F32),
                        pltpu.VMEM((nq, 2, tq, 2 * V_DIM), F32), pltpu.VMEM((s, 2 * V_DIM), BF16)],
        compiler_params=_cparams(("parallel", "parallel")),
        name="attn",
    )(q, kt4, v, lam4, subln_g)


def _split2(x):
    hi = x.astype(BF16)
    return hi, (x - hi.astype(F32)).astype(BF16)


def _head_scores(k, qcol):
    prod = k * qcol
    return jnp.sum(prod.reshape(prod.shape[0] // HEAD_DIM, HEAD_DIM, prod.shape[1]), axis=1)


def _decode_kernel(pt_ref, q_ref, kn_ref, vn_ref, lam_ref, g_ref, *refs, pp, nb, n_heads, lambda_init):
    del pt_ref
    npg = nb * pp
    k_refs, v_refs = refs[:npg], refs[npg:2 * npg]
    o_ref = refs[2 * npg]
    m_ref, l_ref, acc_ref, tmp_ref = refs[2 * npg + 1:]
    j = pl.program_id(1)
    nhc = 2 * n_heads

    @pl.when(j == 0)
    def _():
        m_ref[...] = jnp.full_like(m_ref, NEG_INF)
        l_ref[...] = jnp.zeros_like(l_ref)
        acc_ref[...] = jnp.zeros_like(acc_ref)

    row_head = lax.broadcasted_iota(jnp.int32, (nhc, V_DIM), 0) // 2
    page = k_refs[0].shape[1]
    for r in range(nb):
        qcol = q_ref[r]
        s = [_head_scores(kr[...], qcol) for kr in k_refs[r * pp:(r + 1) * pp]]
        m_old = m_ref[r, :, 0:1]
        smax = functools.reduce(jnp.maximum, s)
        m_new = jnp.maximum(m_old, jnp.max(smax, axis=-1, keepdims=True))
        alpha = jnp.exp(m_old - m_new)
        p = [jnp.exp(si - m_new) for si in s]
        psum = functools.reduce(lambda a, b: a + b, p)
        l_ref[r] = jnp.broadcast_to(alpha * l_ref[r, :, 0:1] + jnp.sum(psum, axis=-1, keepdims=True),
                                    l_ref.shape[1:])
        m_ref[r] = jnp.broadcast_to(m_new, m_ref.shape[1:])
        acc = alpha * acc_ref[r]
        pb = [pi.astype(BF16) for pi in p]
        for h in range(n_heads):
            tot = None
            for pi, vr in zip(pb, v_refs[r * pp:(r + 1) * pp]):
                vh = vr[pl.ds(h, page, stride=n_heads), :].astype(BF16)
                res = jnp.dot(pi, vh, preferred_element_type=F32)
                tot = res if tot is None else tot + res
            acc = acc + jnp.where(row_head == h, tot, 0.0)
        acc_ref[r] = acc

    @pl.when(j == pl.num_programs(1) - 1)
    def _():
        lam = _diff_lambda(lam_ref, lambda_init)
        for r in range(nb):
            s_new = _head_scores(kn_ref[r], q_ref[r])[:, 0:1]
            m_old = m_ref[r, :, 0:1]
            m_f = jnp.maximum(m_old, s_new)
            a = jnp.exp(m_old - m_f)
            pn = jnp.exp(s_new - m_f)
            l_f = a * l_ref[r, :, 0:1] + pn
            tmp_ref[r] = (a * acc_ref[r] + pn * vn_ref[r]) / l_f
            o0 = tmp_ref[r, pl.ds(0, n_heads, stride=2), :]
            o1 = tmp_ref[r, pl.ds(1, n_heads, stride=2), :]
            o = o0 - lam * o1
            o = o * lax.rsqrt(jnp.mean(o * o, axis=-1, keepdims=True) + EPS)
            o_ref[r] = o * g_ref[...] * (1.0 - lambda_init)


def _decode_attn(page_table, qcol, kn_col, vn2, lam4, subln_g, cache_kt, cache_vr, pp, nb, lambda_init):
    bd, n_pages = page_table.shape
    _, w, page = cache_kt.shape
    n_heads = cache_vr.shape[1] // page
    nhc = 2 * n_heads
    pt_flat = page_table.reshape(-1)
    fixed2 = lambda b, j, pt: (0, 0)
    perb = lambda b, j, pt: (b, 0, 0)

    def page_spec(r, i, rows, cols):
        return pl.BlockSpec((None, rows, cols),
                            lambda b, j, pt: (pt[(b * nb + r) * n_pages + j * pp + i], 0, 0))

    slots = [(r, i) for r in range(nb) for i in range(pp)]
    grid_spec = pltpu.PrefetchScalarGridSpec(
        num_scalar_prefetch=1,
        grid=(bd // nb, n_pages // pp),
        in_specs=[pl.BlockSpec((nb, w, LANES), perb), pl.BlockSpec((nb, w, LANES), perb),
                  pl.BlockSpec((nb, nhc, V_DIM), perb),
                  pl.BlockSpec((4, HEAD_DIM), fixed2), pl.BlockSpec((1, V_DIM), fixed2)]
                 + [page_spec(r, i, w, page) for r, i in slots]
                 + [page_spec(r, i, page * n_heads, V_DIM) for r, i in slots],
        out_specs=pl.BlockSpec((nb, n_heads, V_DIM), perb),
        scratch_shapes=[pltpu.VMEM((nb, nhc, LANES), F32), pltpu.VMEM((nb, nhc, LANES), F32),
                        pltpu.VMEM((nb, nhc, V_DIM), F32), pltpu.VMEM((nb, nhc, V_DIM), F32)],
    )
    npg = nb * pp
    return pl.pallas_call(
        functools.partial(_decode_kernel, pp=pp, nb=nb, n_heads=n_heads, lambda_init=lambda_init),
        grid_spec=grid_spec,
        out_shape=jax.ShapeDtypeStruct((bd, n_heads, V_DIM), F32),
        compiler_params=_cparams(("parallel", "arbitrary")),
        name="decode_attn",
    )(pt_flat, qcol, kn_col, vn2, lam4, subln_g, *([cache_kt] * npg), *([cache_vr] * npg))


def _s5prep_kernel(arow_ref, acol_ref, ldt_ref, bt_ref, ct_ref, d_ref,
                   mt_ref, w_ref, vt_ref, lam_ref):
    for gi in range(arow_ref.shape[0]):
        _s5prep_group(gi, arow_ref, acol_ref, ldt_ref, bt_ref, ct_ref, d_ref, mt_ref, w_ref, vt_ref, lam_ref)


def _s5prep_group(gi, arow_ref, acol_ref, ldt_ref, bt_ref, ct_ref, d_ref, mt_ref, w_ref, vt_ref, lam_ref):
    L, hg, p = S5_CHUNK, S5_GROUP, S5_STATE
    dt = jnp.exp(ldt_ref[gi])
    ar, ai = arow_ref[gi, 0:1], arow_ref[gi, 1:2]
    arc, aic = acol_ref[gi, :, 0:1], acol_ref[gi, :, 1:2]

    def powers(a_r, a_i, j):
        mag = jnp.exp(j * (a_r * dt))
        return mag * jnp.cos(j * (a_i * dt)), mag * jnp.sin(j * (a_i * dt))

    l1r, l1i = powers(ar, ai, 1.0)
    llr, lli = powers(ar, ai, float(L))
    lam_ref[gi] = jnp.concatenate([llr, lli, l1r, l1i, jnp.zeros((4, p), F32)], axis=0)
    den = ar * ar + ai * ai
    cr = ((l1r - 1.0) * ar + l1i * ai) / den
    ci = (l1i * ar - (l1r - 1.0) * ai) / den
    btr, bti = bt_ref[gi, 0], bt_ref[gi, 1]
    bbr = btr * cr - bti * ci
    bbi = btr * ci + bti * cr
    jrow = lax.broadcasted_iota(jnp.int32, (L, p), 0).astype(F32)
    pr, pi = powers(ar, ai, jrow)
    for s in range(L):
        qr, qi = pr[L - 1 - s:L - s], pi[L - 1 - s:L - s]
        w_ref[gi, 0, s * hg:(s + 1) * hg, :] = bbr * qr - bbi * qi
        w_ref[gi, 1, s * hg:(s + 1) * hg, :] = bbr * qi + bbi * qr
    jl = (lax.broadcasted_iota(jnp.int32, (p, L * hg), 1) // hg).astype(F32)
    lpr, lpi = powers(arc, aic, jl)
    ctr, cti = ct_ref[gi, 0], ct_ref[gi, 1]
    cjr = ctr * lpr - cti * lpi
    cji = ctr * lpi + cti * lpr
    c1r, c1i = powers(arc, aic, 1.0)
    vt_ref[gi, 0] = cjr * c1r - cji * c1i
    vt_ref[gi, 1] = -(cjr * c1i + cji * c1r)
    kt = (jnp.dot(bbr, cjr, preferred_element_type=F32, precision=_HI)
          - jnp.dot(bbi, cji, preferred_element_type=F32, precision=_HI))
    n = L * hg
    ri = lax.broadcasted_iota(jnp.int32, (n, n), 0)
    cidx = lax.broadcasted_iota(jnp.int32, (n, n), 1)
    dtile = d_ref[gi]
    for s in range(L):
        shift = (cidx - ri == s * hg).astype(F32)
        blk = jnp.dot(kt, shift, preferred_element_type=F32, precision=_HI)
        rr = lax.broadcasted_iota(jnp.int32, (hg, n), 0) + s * hg
        cc = lax.broadcasted_iota(jnp.int32, (hg, n), 1)
        mt_ref[gi, s * hg:(s + 1) * hg, :] = blk + jnp.where(rr == cc, dtile, 0.0)


def _s5prep(arow, acol, ldt, bt, ct, dt_tiled):
    g = arow.shape[0]
    L, hg, p = S5_CHUNK, S5_GROUP, S5_STATE
    n = L * hg
    i3 = lambda i: (i, 0, 0)
    i4 = lambda i: (i, 0, 0, 0)
    gb = 4 if g % 4 == 0 else 1
    return pl.pallas_call(
        _s5prep_kernel,
        grid=(g // gb,),
        in_specs=[pl.BlockSpec((gb, 2, p), i3), pl.BlockSpec((gb, p, 2), i3), pl.BlockSpec((gb, 1, 1), i3),
                  pl.BlockSpec((gb, 2, hg, p), i4), pl.BlockSpec((gb, 2, p, n), i4),
                  pl.BlockSpec((gb, 1, n), i3)],
        out_specs=[pl.BlockSpec((gb, n, n), i3), pl.BlockSpec((gb, 2, n, p), i4),
                   pl.BlockSpec((gb, 2, p, n), i4), pl.BlockSpec((gb, SUBLANES, p), i3)],
        out_shape=[jax.ShapeDtypeStruct((g, n, n), F32), jax.ShapeDtypeStruct((g, 2, n, p), F32),
                   jax.ShapeDtypeStruct((g, 2, p, n), F32), jax.ShapeDtypeStruct((g, SUBLANES, p), F32)],
        compiler_params=_cparams(("parallel",)),
        name="s5prep",
    )(arow, acol, ldt, bt, ct, dt_tiled)


def _s5in_kernel(ut_ref, perm_ref, o_ref):
    g, nc, n = o_ref.shape[1:]
    hg, L = S5_GROUP, S5_CHUNK
    sh = perm_ref.shape[0]
    nseg, ncs = ut_ref.shape[2] // sh, sh // L
    ups = [jnp.dot(ut_ref[0, :, k * sh:(k + 1) * sh], perm_ref[...], preferred_element_type=F32)
           for k in range(nseg)]
    for gi in range(g):
        rows = slice(gi * hg, (gi + 1) * hg)
        ugt = jnp.concatenate(
            [jnp.concatenate([up[rows, s * ncs:(s + 1) * ncs] for up in ups], axis=1) for s in range(L)], axis=0)
        o_ref[0, gi] = ugt.T.astype(o_ref.dtype)


def _s5in(ut, perm):
    b, w, s = ut.shape
    g, nc, n = w // S5_GROUP, s // S5_CHUNK, S5_CHUNK * S5_GROUP
    return pl.pallas_call(
        _s5in_kernel,
        grid=(b,),
        in_specs=[pl.BlockSpec((1, w, s), lambda i: (i, 0, 0)),
                  pl.BlockSpec(perm.shape, lambda i: (0, 0), pipeline_mode=pl.Buffered(1))],
        out_specs=pl.BlockSpec((1, g, None, nc, n), lambda i: (i // SUBLANES, 0, i % SUBLANES, 0, 0)),
        out_shape=jax.ShapeDtypeStruct((b // SUBLANES, g, SUBLANES, nc, n), BF16),
        compiler_params=_cparams(("parallel",)),
        name="s5in",
    )(ut, perm)


def _s5out_kernel(y_ref, permt_ref, o_ref, ypt_ref):
    g, nc, n = y_ref.shape[1:]
    hg, L = S5_GROUP, S5_CHUNK
    nseg, sh = ypt_ref.shape[0], ypt_ref.shape[2]
    ncs = sh // L
    for gi in range(g):
        ygt = y_ref[0, gi].astype(F32).T
        for t in range(L):
            blk = ygt[t * hg:(t + 1) * hg, :].astype(BF16)
            for k in range(nseg):
                ypt_ref[k, gi * hg:(gi + 1) * hg, t * ncs:(t + 1) * ncs] = blk[:, k * ncs:(k + 1) * ncs]
    yt = jnp.concatenate([jnp.dot(ypt_ref[k], permt_ref[...], preferred_element_type=F32)
                          for k in range(nseg)], axis=1)
    o_ref[0] = yt.T.astype(o_ref.dtype)


def _s5out(y5, permt):
    nbh, g, _, nc, n = y5.shape
    b, s, w = nbh * SUBLANES, nc * S5_CHUNK, g * S5_GROUP
    return pl.pallas_call(
        _s5out_kernel,
        grid=(b,),
        in_specs=[pl.BlockSpec((1, g, None, nc, n), lambda i: (i // SUBLANES, 0, i % SUBLANES, 0, 0)),
                  pl.BlockSpec(permt.shape, lambda i: (0, 0), pipeline_mode=pl.Buffered(1))],
        out_specs=pl.BlockSpec((1, s, w), lambda i: (i, 0, 0)),
        out_shape=jax.ShapeDtypeStruct((b, s, w), BF16),
        scratch_shapes=[pltpu.VMEM((s // permt.shape[0], w, permt.shape[0]), BF16)],
        compiler_params=_cparams(("parallel",)),
        name="s5out",
    )(y5, permt)


def _s5chunk_kernel(u_ref, mt_ref, w_ref, wsw_ref, vt_ref, lam_ref, y_ref, hl_ref, t1_ref, t2_ref, hs_ref):
    gb = u_ref.shape[1]
    nc = u_ref.shape[2] // SUBLANES
    p = S5_STATE
    lane = lax.broadcasted_iota(jnp.int32, (SUBLANES, 2 * p), 1)
    la, lb = [], []
    for gi in range(gb):
        u = u_ref[0, gi]
        t1_ref[gi] = jnp.dot(u, w_ref[gi].astype(BF16), preferred_element_type=F32)
        t2_ref[gi] = jnp.dot(u, wsw_ref[gi].astype(BF16), preferred_element_type=F32)
        la.append(jnp.broadcast_to(lam_ref[gi, 0:1], (SUBLANES, 2 * p)))
        li2 = jnp.broadcast_to(lam_ref[gi, 1:2], (SUBLANES, 2 * p))
        lb.append(jnp.where(lane < p, -li2, li2))

    def step(c, h):
        out = []
        for gi in range(gb):
            a, b = h[2 * gi], h[2 * gi + 1]
            s1 = t1_ref[gi, pl.ds(c, SUBLANES, stride=nc), :]
            s2 = t2_ref[gi, pl.ds(c, SUBLANES, stride=nc), :]
            hs_ref[gi, pl.ds(pl.multiple_of(c * SUBLANES, SUBLANES), SUBLANES), :] = a
            out += [a * la[gi] + b * lb[gi] + s1, b * la[gi] - a * lb[gi] + s2]
        return tuple(out)

    z = jnp.zeros((SUBLANES, 2 * p), F32)
    h = lax.fori_loop(0, nc, step, (z,) * (2 * gb), unroll=8)
    for gi in range(gb):
        hl_ref[0, gi] = h[2 * gi]
        mt = mt_ref[gi].astype(BF16)
        vt = vt_ref[gi].astype(BF16)
        for bi in range(SUBLANES):
            rows = slice(bi * nc, (bi + 1) * nc)
            hb = hs_ref[gi, pl.ds(bi, nc, stride=SUBLANES), :].astype(BF16)
            y = (jnp.dot(u_ref[0, gi, rows, :], mt, preferred_element_type=F32)
                 + jnp.dot(hb, vt, preferred_element_type=F32))
            y_ref[0, gi, rows, :] = y.astype(y_ref.dtype)


def _s5chunk(u4, mt, wcat, wsw, vtcat, lam2):
    nbh, g, rows, n = u4.shape
    p2 = 2 * S5_STATE
    gb = 2 if g % 2 == 0 else 1
    um = lambda b, gi: (b, gi, 0, 0)
    g3 = lambda b, gi: (gi, 0, 0)
    return pl.pallas_call(
        _s5chunk_kernel,
        grid=(nbh, g // gb),
        in_specs=[pl.BlockSpec((1, gb, rows, n), um), pl.BlockSpec((gb, n, n), g3),
                  pl.BlockSpec((gb, n, p2), g3), pl.BlockSpec((gb, n, p2), g3), pl.BlockSpec((gb, p2, n), g3),
                  pl.BlockSpec((gb, SUBLANES, p2), g3)],
        out_specs=[pl.BlockSpec((1, gb, rows, n), um), pl.BlockSpec((1, gb, SUBLANES, p2), um)],
        out_shape=[jax.ShapeDtypeStruct(u4.shape, BF16),
                   jax.ShapeDtypeStruct((nbh, g, SUBLANES, p2), F32)],
        scratch_shapes=[pltpu.VMEM((gb, rows, p2), F32), pltpu.VMEM((gb, rows, p2), F32),
                        pltpu.VMEM((gb, rows, p2), F32)],
        compiler_params=_cparams(("parallel", "parallel")),
        name="s5chunk",
    )(u4, mt, wcat, wsw, vtcat, lam2)


def _s5step_kernel(u_ref, h0_ref, bb_ref, lam_ref, c_ref, d_ref, y_ref, h_ref):
    u = u_ref[0]
    h0r, h0i = h0_ref[0, 0], h0_ref[0, 1]
    l1r, l1i = lam_ref[0, 2:3], lam_ref[0, 3:4]
    bur = jnp.dot(u, bb_ref[0, 0], preferred_element_type=F32, precision=_HI)
    bui = jnp.dot(u, bb_ref[0, 1], preferred_element_type=F32, precision=_HI)
    hr = l1r * h0r - l1i * h0i + bur
    hi = l1r * h0i + l1i * h0r + bui
    h_ref[0, 0] = hr
    h_ref[0, 1] = hi
    y = _nt_dot(hr, c_ref[0, 0], precision=_HI) - _nt_dot(hi, c_ref[0, 1], precision=_HI)
    y_ref[0] = y + d_ref[0] * u


def _s5step(u3, h0, bb, lam, c2, d3):
    g, bd, hg = u3.shape
    p = S5_STATE
    i3 = lambda i: (i, 0, 0)
    i4 = lambda i: (i, 0, 0, 0)
    return pl.pallas_call(
        _s5step_kernel,
        grid=(g,),
        in_specs=[pl.BlockSpec((1, bd, hg), i3), pl.BlockSpec((1, 2, bd, p), i4),
                  pl.BlockSpec((1, 2, hg, p), i4), pl.BlockSpec((1, SUBLANES, p), i3),
                  pl.BlockSpec((1, 2, hg, p), i4), pl.BlockSpec((1, 1, hg), i3)],
        out_specs=[pl.BlockSpec((1, bd, hg), i3), pl.BlockSpec((1, 2, bd, p), i4)],
        out_shape=[jax.ShapeDtypeStruct((g, bd, hg), F32), jax.ShapeDtypeStruct((g, 2, bd, p), F32)],
        compiler_params=_cparams(("parallel",)),
        name="s5step",
    )(u3, h0, bb, lam, c2, d3)


def _postmix_kernel(x_ref, a_ref, y_ref, wglu_ref, bglu_ref, wout_ref, g2_ref, wr_ref, wrf_ref, br_ref,
                    x1_ref, h2_ref, comb_ref, *, precise, parts):
    th = x_ref.shape[0] // parts
    for part in range(parts):
        rows = slice(part * th, (part + 1) * th)
        _postmix_rows(x_ref, a_ref, y_ref, wglu_ref, bglu_ref, wout_ref, g2_ref, wr_ref, wrf_ref, br_ref,
                      x1_ref, h2_ref, comb_ref, rows, precise)


def _postmix_rows(x_ref, a_ref, y_ref, wglu_ref, bglu_ref, wout_ref, g2_ref, wr_ref, wrf_ref, br_ref,
                  x1_ref, h2_ref, comb_ref, rows, precise):
    wa = a_ref.shape[-1]

    def mm(act, w):
        if precise:
            return jnp.dot(act, w, preferred_element_type=F32, precision=_HI)
        return jnp.dot(act.astype(BF16), w, preferred_element_type=F32)

    y = jax.nn.gelu(y_ref[rows, :].astype(F32))
    z = mm(y, wglu_ref[...]) + bglu_ref[...]
    s5o = y * jax.nn.sigmoid(z)
    mix = mm(a_ref[rows, :].astype(F32) if precise else a_ref[rows, :], wout_ref[0:wa, :])
    mix += mm(s5o, wout_ref[wa:, :])
    x1 = x_ref[rows, :] + mix
    x1_ref[rows, :] = x1
    h2 = x1 * lax.rsqrt(jnp.mean(x1 * x1, axis=-1, keepdims=True) + EPS) * g2_ref[...]
    h2_ref[rows, :] = h2.astype(h2_ref.dtype)
    if precise:
        logits = jnp.dot(h2, wrf_ref[...], preferred_element_type=F32, precision=_HI)
    else:
        hi, lo = _split2(h2)
        hh = jnp.dot(hi, wr_ref[...], preferred_element_type=F32)
        logits = hh[:, :LANES] + hh[:, LANES:] + jnp.dot(lo, wr_ref[:, :LANES], preferred_element_type=F32)
    logits = logits + br_ref[...]
    ne = N_GROUPS_MOE * EXP_PER_GROUP
    lane = lax.broadcasted_iota(jnp.int32, logits.shape, 1).astype(F32)
    big = jnp.float32(1 << 20)
    gmask = (lane >= ne) & (lane < ne + N_GROUPS_MOE)
    gl = jnp.where(gmask, logits, NEG_INF)
    gmax = jnp.max(gl, axis=-1, keepdims=True)
    gidx = jnp.min(jnp.where(gl == gmax, lane, big), axis=-1, keepdims=True) - ne
    g_w = 1.0 / jnp.sum(jnp.where(gmask, jnp.exp(logits - gmax), 0.0), axis=-1, keepdims=True)
    lo = gidx * EXP_PER_GROUP
    el = jnp.where((lane >= lo) & (lane < lo + EXP_PER_GROUP), logits, NEG_INF)
    e1 = jnp.max(el, axis=-1, keepdims=True)
    i1 = jnp.min(jnp.where(el == e1, lane, big), axis=-1, keepdims=True)
    el2 = jnp.where(lane == i1, NEG_INF, el)
    e2 = jnp.max(el2, axis=-1, keepdims=True)
    i2 = jnp.min(jnp.where(el2 == e2, lane, big), axis=-1, keepdims=True)
    r = jnp.exp(e2 - e1)
    w1 = g_w / (1.0 + r)
    w2 = g_w * r / (1.0 + r)
    comb_ref[rows, :] = (jnp.where(lane == i1, w1, 0.0) + jnp.where(lane == i2, w2, 0.0)
                         + jnp.where(lane == GID_LANE, gidx, 0.0))


def _postmix(x2, attn, ys5, wglu, bglu, wout, ln2_g, w_router, b_router, tm, precise):
    t, d = x2.shape
    wa = attn.shape[1]
    ws = ys5.shape[1]
    row = lambda i: (i, 0)
    fixed = lambda i: (0, 0)
    wr_hi, wr_lo = _split2(w_router)
    wr_hl = jnp.concatenate([wr_hi, wr_lo], axis=1)
    parts = 2 if tm % 512 == 0 else 1
    return pl.pallas_call(
        functools.partial(_postmix_kernel, precise=precise, parts=parts),
        grid=(t // tm,),
        in_specs=[pl.BlockSpec((tm, d), row), pl.BlockSpec((tm, wa), row), pl.BlockSpec((tm, ws), row),
                  pl.BlockSpec((ws, ws), fixed), pl.BlockSpec((1, ws), fixed),
                  pl.BlockSpec((wa + ws, d), fixed), pl.BlockSpec((1, d), fixed),
                  pl.BlockSpec((d, 2 * LANES), fixed), pl.BlockSpec((d, LANES), fixed),
                  pl.BlockSpec((1, LANES), fixed)],
        out_specs=[pl.BlockSpec((tm, d), row), pl.BlockSpec((tm, d), row), pl.BlockSpec((tm, LANES), row)],
        out_shape=[jax.ShapeDtypeStruct((t, d), F32), jax.ShapeDtypeStruct((t, d), BF16),
                   jax.ShapeDtypeStruct((t, LANES), F32)],
        compiler_params=_cparams(("parallel",)),
        name="postmix",
    )(x2, attn, ys5, wglu, bglu, wout, ln2_g, wr_hl, w_router, b_router)


def _moe_kernel(h_ref, x1_ref, comb_ref, wg_ref, wu_ref, wd_ref, gf_ref, o_ref, acc_ref):
    e = pl.program_id(1)

    @pl.when(e == 0)
    def _():
        acc_ref[...] = jnp.zeros_like(acc_ref)

    h = h_ref[...]
    comb = comb_ref[...]
    lane = lax.broadcasted_iota(jnp.int32, comb.shape, 1)
    eb = wg_ref.shape[0]
    acc = acc_ref[...]
    for k in range(eb):
        he = jax.nn.silu(jnp.dot(h, wg_ref[k], preferred_element_type=F32))
        he = he * jnp.dot(h, wu_ref[k], preferred_element_type=F32)
        ce = jnp.sum(jnp.where(lane == e * eb + k, comb, 0.0), axis=-1, keepdims=True)
        acc = acc + jnp.dot((he * ce).astype(BF16), wd_ref[k], preferred_element_type=F32)
    acc_ref[...] = acc

    @pl.when(e == pl.num_programs(1) - 1)
    def _():
        x2 = x1_ref[...] + acc_ref[...]
        o_ref[...] = x2 * lax.rsqrt(jnp.mean(x2 * x2, axis=-1, keepdims=True) + EPS) * gf_ref[...]


def _moe(h2, x1, comb, wg_bf, wu_bf, wd_bf, lnf_g, tm):
    t, d = x1.shape
    ne, _, de = wg_bf.shape
    row = lambda i, e: (i, 0)
    eb = 4 if ne % 4 == 0 else 1
    return pl.pallas_call(
        _moe_kernel,
        grid=(t // tm, ne // eb),
        in_specs=[pl.BlockSpec((tm, d), row), pl.BlockSpec((tm, d), row), pl.BlockSpec((tm, LANES), row),
                  pl.BlockSpec((eb, d, de), lambda i, e: (e, 0, 0)),
                  pl.BlockSpec((eb, d, de), lambda i, e: (e, 0, 0)),
                  pl.BlockSpec((eb, de, d), lambda i, e: (e, 0, 0)),
                  pl.BlockSpec((1, d), lambda i, e: (0, 0))],
        out_specs=pl.BlockSpec((tm, d), row),
        out_shape=jax.ShapeDtypeStruct((t, d), F32),
        scratch_shapes=[pltpu.VMEM((tm, d), F32)],
        compiler_params=_cparams(("parallel", "arbitrary")),
        name="moe",
    )(h2, x1, comb, wg_bf, wu_bf, wd_bf, lnf_g)


def _moe_grouped_kernel(h_ref, x1_ref, comb_ref, ut_ref, lt_ref, wg_ref, wu_ref, wd_ref, gf_ref, o_ref,
                        rrow_ref, rcol_ref, chi_ref, clo_ref, cnt_ref):
    g = pl.program_id(1)
    tm, d = h_ref.shape
    gf32 = g.astype(F32)

    @pl.when(g == 0)
    def _():
        comb = comb_ref[...]
        combt = comb.T
        gidc = comb[:, GID_LANE:GID_LANE + 1]
        gidr = combt[GID_LANE:GID_LANE + 1, :]
        lane = lax.broadcasted_iota(jnp.int32, comb.shape, 1).astype(F32)
        sub = lax.broadcasted_iota(jnp.int32, (SUBLANES, tm), 0).astype(F32)
        ohc = lane == gidc
        ohr = sub == gidr
        cntc = jnp.dot(lt_ref[...], jnp.where(ohc, 1.0, 0.0).astype(BF16), preferred_element_type=F32)
        cntr = jnp.dot(jnp.where(ohr, 1.0, 0.0).astype(BF16), ut_ref[...], preferred_element_type=F32)
        rankc = jnp.sum(jnp.where(ohc, cntc, 0.0), axis=-1, keepdims=True) - 1.0
        rankr = jnp.sum(jnp.where(ohr, cntr, 0.0), axis=0, keepdims=True) - 1.0
        rcol_ref[...] = jnp.broadcast_to(rankc, rcol_ref.shape)
        rrow_ref[...] = jnp.concatenate([rankr, gidr, jnp.zeros((SUBLANES - 2, tm), F32)], axis=0)
        chi, clo = _split2(combt)
        chi_ref[...] = chi
        clo_ref[...] = clo
        for r in range(N_GROUPS_MOE):
            cnt_ref[r] = jnp.max(cntr[r:r + 1, :]).astype(jnp.int32)
        o_ref[...] = jnp.zeros_like(o_ref)

    posr = jnp.where(rrow_ref[1:2, :] == gf32, rrow_ref[0:1, :], -1.0)
    posc = jnp.where(comb_ref[:, GID_LANE:GID_LANE + 1] == gf32, rcol_ref[:, 0:1], -1.0)
    n_rows = cnt_ref[g]
    big = 2 * MOE_CHUNK
    mid = big + MOE_CHUNK // 2
    use_mid = jnp.logical_and(n_rows > big, n_rows <= mid)
    n_big = jnp.where(use_mid, 0, n_rows // big)
    n_mid = jnp.where(use_mid, 1, 0)
    n_small = jnp.where(use_mid, 0, (n_rows - n_big * big + (MOE_CHUNK - 1)) // MOE_CHUNK)

    def chunk(row0, ch):
        base = row0.astype(F32)
        rid = lax.broadcasted_iota(jnp.int32, (ch, tm), 0).astype(F32) + base
        cid = lax.broadcasted_iota(jnp.int32, (tm, ch), 1).astype(F32) + base
        sel = jnp.where(posr == rid, 1.0, 0.0).astype(BF16)
        selt = jnp.where(posc == cid, 1.0, 0.0).astype(BF16)
        xg = jnp.dot(sel, h_ref[...], preferred_element_type=F32).astype(BF16)
        cg = _nt_dot(sel, chi_ref[...]) + _nt_dot(sel, clo_ref[...])
        lane = lax.broadcasted_iota(jnp.int32, cg.shape, 1).astype(F32)
        z = jnp.zeros((ch, d), F32)
        for e in range(EXP_PER_GROUP):
            he = jax.nn.silu(jnp.dot(xg, wg_ref[e], preferred_element_type=F32))
            he = he * jnp.dot(xg, wu_ref[e], preferred_element_type=F32)
            ce = jnp.sum(jnp.where(lane == gf32 * EXP_PER_GROUP + e, cg, 0.0), axis=-1, keepdims=True)
            z = z + jnp.dot((he * ce).astype(BF16), wd_ref[e], preferred_element_type=F32)
        o_ref[...] += jnp.dot(selt, z.astype(BF16), preferred_element_type=F32)

    def big_body(k, carry):
        chunk(k * big, big)
        return carry

    def mid_body(k, carry):
        chunk(k * mid, mid)
        return carry

    def small_body(k, carry):
        chunk(n_big * big + k * MOE_CHUNK, MOE_CHUNK)
        return carry

    lax.fori_loop(0, n_big, big_body, 0)
    lax.fori_loop(0, n_mid, mid_body, 0)
    lax.fori_loop(0, n_small, small_body, 0)

    @pl.when(g == pl.num_programs(1) - 1)
    def _():
        x2 = x1_ref[...] + o_ref[...]
        o_ref[...] = x2 * lax.rsqrt(jnp.mean(x2 * x2, axis=-1, keepdims=True) + EPS) * gf_ref[...]


def _moe_grouped(h2, x1, comb, wg_bf, wu_bf, wd_bf, lnf_g, tm):
    t, d = x1.shape
    ne, _, de = wg_bf.shape
    ng = ne // EXP_PER_GROUP
    tri = jnp.arange(tm)[:, None] <= jnp.arange(tm)[None, :]
    ut = tri.astype(BF16)
    lt = tri.T.astype(BF16)
    once = pl.Buffered(1)
    tile = lambda i, g: (i, 0)
    fixed = lambda i, g: (0, 0)
    wmap = lambda i, g: (g, 0, 0)
    return pl.pallas_call(
        _moe_grouped_kernel,
        grid=(t // tm, ng),
        in_specs=[pl.BlockSpec((tm, d), tile), pl.BlockSpec((tm, d), tile),
                  pl.BlockSpec((tm, LANES), tile),
                  pl.BlockSpec((tm, tm), fixed, pipeline_mode=once),
                  pl.BlockSpec((tm, tm), fixed, pipeline_mode=once),
                  pl.BlockSpec((EXP_PER_GROUP, d, de), wmap),
                  pl.BlockSpec((EXP_PER_GROUP, d, de), wmap),
                  pl.BlockSpec((EXP_PER_GROUP, de, d), wmap),
                  pl.BlockSpec((1, d), fixed)],
        out_specs=pl.BlockSpec((tm, d), tile),
        out_shape=jax.ShapeDtypeStruct((t, d), F32),
        scratch_shapes=[pltpu.VMEM((SUBLANES, tm), F32), pltpu.VMEM((tm, LANES), F32),
                        pltpu.VMEM((LANES, tm), BF16), pltpu.VMEM((LANES, tm), BF16),
                        pltpu.SMEM((N_GROUPS_MOE,), jnp.int32)],
        compiler_params=_cparams(("parallel", "arbitrary")),
        name="moe_grouped",
    )(h2, x1, comb, ut, lt, wg_bf, wu_bf, wd_bf, lnf_g)


def _tile(n, want):
    t = min(n, want)
    while n % t:
        t //= 2
    return t


def kernel(x_prompt, x_sample, cache_k, cache_v, state_s5_re, state_s5_im, page_table, ln1_g, w_in, lambda_q1, lambda_k1, lambda_q2, lambda_k2, subln_g, s5_a_re, s5_a_im, s5_log_dt, s5_b_re, s5_b_im, s5_c_re, s5_c_im, s5_d, w_glu, b_glu, w_out, ln2_g, w_router_group, b_router_group, w_router_expert, b_router_expert, w_gate, w_up, w_down, ln_f_g):
    depth = ln1_g.shape[0]
    assert depth == 1, "single-layer step"
    b, s, d = x_prompt.shape
    bd, ds, _ = x_sample.shape
    assert ds == 1
    n_pages = page_table.shape[1]
    page = cache_k.shape[2]
    past_len = n_pages * page
    wq = w_in.shape[2] // 4
    n_heads = wq // V_DIM
    g = s5_a_re.shape[1]
    p = S5_STATE
    hg = S5_GROUP
    L = S5_CHUNK
    n = L * hg
    assert b % SUBLANES == 0 and s % L == 0 and g * hg == wq
    lambda_init = 0.8 - 0.6 * math.exp(-0.3 * 0)

    w_in_bf = w_in[0].astype(BF16)
    wglu_bf = w_glu[0].astype(BF16)
    wout_bf = w_out[0].astype(BF16)
    wg_bf, wu_bf, wd_bf = (w[0].astype(BF16) for w in (w_gate, w_up, w_down))
    ne = w_gate.shape[1]
    w_router = jnp.concatenate(
        [jnp.transpose(w_router_expert[0], (1, 0, 2)).reshape(d, ne), w_router_group[0],
         jnp.zeros((d, LANES - ne - N_GROUPS_MOE), F32)], axis=1)
    b_router = jnp.concatenate([b_router_expert[0].reshape(ne), b_router_group[0],
                                jnp.zeros((LANES - ne - N_GROUPS_MOE,), F32)])[None, :]
    lam4 = jnp.concatenate([lambda_q1, lambda_k1, lambda_q2, lambda_k2], axis=0)
    ln1 = ln1_g[0][None, :]
    ln2 = ln2_g[0][None, :]
    lnf = ln_f_g[None, :]
    subg = subln_g[0][None, :]
    bglu = b_glu[0][None, :]

    arow = jnp.stack([s5_a_re[0], s5_a_im[0]], axis=1)
    acol = jnp.stack([s5_a_re[0], s5_a_im[0]], axis=2)
    ldt = s5_log_dt[0].reshape(g, 1, 1)
    bt = jnp.stack([jnp.swapaxes(s5_b_re[0], 1, 2), jnp.swapaxes(s5_b_im[0], 1, 2)], axis=1)
    ct = jnp.stack([jnp.tile(jnp.swapaxes(s5_c_re[0], 1, 2), (1, 1, L)),
                    jnp.tile(jnp.swapaxes(s5_c_im[0], 1, 2), (1, 1, L))], axis=1)
    d_g = s5_d[0].reshape(g, 1, hg)
    mt, w_s5, vt, lam_s5 = _s5prep(arow, acol, ldt, bt, ct, jnp.tile(d_g, (1, 1, L)))

    tm = _tile(b * s, TOKEN_TILE)
    x2 = x_prompt.reshape(b * s, d)
    pos_p = jnp.arange(s, dtype=jnp.int32)
    rc, rs1, rs2 = _rope_tables(pos_p)
    rct, rst = _rope_tables_t(pos_p)
    w_qv_bf = jnp.concatenate([w_in_bf[:, :wq], w_in_bf[:, 2 * wq:3 * wq]], axis=1)
    wkut_bf = jnp.concatenate([w_in_bf[:, wq:2 * wq], w_in_bf[:, 3 * wq:]], axis=1).T
    tq = _tile(s, ATTN_TILE)
    q, vf, vb, ut, ktf, ktb = _inproj_prompt(x_prompt, ln1, w_qv_bf, wkut_bf, rc, rs1, rs2, rct, rst,
                                             _tile(s, tm), tq)
    attn = _attn(q, ktb, vb, lam4, subg, n_heads, lambda_init)
    kf = jnp.transpose(ktf.reshape(b, 2 * n_heads, HEAD_DIM, s), (0, 3, 1, 2))
    nbh, nc = b // SUBLANES, s // L
    seg = s // 2 if (s // 2) % (L * LANES // 2) == 0 else s
    tok = jnp.arange(seg)
    perm = (((tok % L) * (seg // L) + tok // L)[:, None] == tok[None, :]).astype(BF16)
    u5 = _s5in(ut, perm)
    wcat = jnp.concatenate([w_s5[:, 0], w_s5[:, 1]], axis=-1)
    wsw = jnp.concatenate([w_s5[:, 1], w_s5[:, 0]], axis=-1)
    vtcat = jnp.concatenate([vt[:, 0], vt[:, 1]], axis=1)
    lam2 = jnp.concatenate([lam_s5, lam_s5], axis=-1)
    y5, hl = _s5chunk(u5.reshape(nbh, g, SUBLANES * nc, n), mt, wcat, wsw, vtcat, lam2)
    ys5 = _s5out(y5.reshape(nbh, g, SUBLANES, nc, n), perm.T).reshape(b * s, wq)
    hl = hl.transpose(0, 2, 1, 3).reshape(b, g, 2, p).transpose(2, 0, 1, 3)
    x1, h2, comb = _postmix(x2, attn.reshape(b * s, wq), ys5, wglu_bf, bglu, wout_bf, ln2,
                            w_router, b_router, tm, False)
    y_prompt = _moe_grouped(h2, x1, comb, wg_bf, wu_bf, wd_bf, lnf, _tile(b * s, MOE_TILE)).reshape(b, s, d)

    xs2 = x_sample.reshape(bd, d)
    pos_s = jnp.full((bd,), past_len, jnp.int32)
    sc, ss1, ss2 = _rope_tables(pos_s)
    qs, kfs, vfs, us = _inproj(xs2, ln1, w_in[0], sc, ss1, ss2, bd)
    cache_kt = jnp.transpose(cache_k[0], (0, 2, 3, 1)).reshape(-1, wq, page)
    cache_vr = cache_v[0].reshape(-1, page * n_heads, V_DIM)
    qcol = jnp.broadcast_to(qs[:, :, None], (bd, wq, LANES))
    kn_col = jnp.broadcast_to(kfs[:, :, None], (bd, wq, LANES))
    vn2 = jnp.repeat(vfs.reshape(bd, n_heads, V_DIM), 2, axis=1)
    pp = _tile(n_pages, DECODE_PAGES)
    nb_dec = _tile(bd, DECODE_ROWS)
    attn_s = _decode_attn(page_table, qcol, kn_col, vn2, lam4, subg, cache_kt, cache_vr, pp, nb_dec,
                          lambda_init)
    u3 = us.reshape(bd, g, hg).transpose(1, 0, 2)
    h0 = jnp.stack([state_s5_re[0], state_s5_im[0]]).transpose(2, 0, 1, 3)
    bb = w_s5[:, :, (L - 1) * hg:, :]
    c2 = jnp.stack([s5_c_re[0], s5_c_im[0]], axis=1)
    ys3, hs = _s5step(u3, h0, bb, lam_s5, c2, d_g)
    ys_s5 = ys3.transpose(1, 0, 2).reshape(bd, wq)
    x1s, h2s, combs = _postmix(xs2, attn_s.reshape(bd, wq), ys_s5, w_glu[0], bglu, w_out[0], ln2,
                               w_router, b_router, bd, True)
    y_sample = _moe(h2s, x1s, combs, wg_bf, wu_bf, wd_bf, lnf, bd).reshape(bd, 1, d)
    hs = hs.transpose(1, 2, 0, 3)

    return (y_prompt, y_sample,
            kf.reshape(1, b, s, 2 * n_heads, HEAD_DIM), vf.reshape(1, b, s, n_heads, V_DIM),
            hl[0][None], hl[1][None],
            kfs.reshape(1, bd, 1, 2 * n_heads, HEAD_DIM), vfs.reshape(1, bd, 1, n_heads, V_DIM),
            hs[0][None], hs[1][None])
```
